```python
import math
import jax
import jax.numpy as jnp
from jax import lax
import numpy as np

D_MODEL = 1024
BATCH = 32
SEQ = 256
DEPTH = 4
DEC_BATCH = 4
DEC_SEQ = 2048
PAST_LEN = 512

GRID_W = 64
N_EVEN = (DEPTH + 1) // 2
N_ODD = DEPTH // 2
CHUNK = 64
Q_BLOCK = 128
CONV_K = 3
EPS = 1e-6
ROPE_BASE = 10000.0

SSD_HEADS = 16
SSD_HEAD_DIM = 64
SSD_WIDTH = SSD_HEADS * SSD_HEAD_DIM
SSD_GROUPS = 2
SSD_STATE = 64
SSD_XBC = SSD_WIDTH + 2 * SSD_GROUPS * SSD_STATE
DA_HEADS = 8
DA_HEAD_DIM = 64
DA_QK = DA_HEADS * 2 * DA_HEAD_DIM
DA_V_DIM = 2 * DA_HEAD_DIM
DA_V = DA_HEADS * DA_V_DIM
EVEN_IN = SSD_WIDTH + SSD_XBC + 2 * SSD_HEADS + 2 * DA_QK + DA_V
EVEN_OUT = SSD_WIDTH + DA_V
GDN_HEADS = 8
GDN_K = 128
GDN_V = 128
GDN_QKV = GDN_HEADS * (2 * GDN_K + GDN_V)
GDN_WIDTH = GDN_HEADS * GDN_V
ODD_IN = GDN_QKV + GDN_WIDTH + 4 * GDN_HEADS
D_FF = 2816

kernel_name = "hybrid_diffusion_ssd_diffattn_gdn_step"

F32 = jnp.float32


def _split(x, sizes):
    return jnp.split(x, [int(s) for s in np.cumsum(sizes)[:-1]], axis=-1)


def _rmsnorm(x, g):
    xf = x.astype(F32)
    y = xf * lax.rsqrt(jnp.mean(xf * xf, axis=-1, keepdims=True) + EPS)
    return (y * g.astype(F32)).astype(x.dtype)


def _l2norm(x):
    return x * lax.rsqrt(jnp.sum(x * x, axis=-1, keepdims=True) + EPS)


def _dwconv(x, w, bias):
    ch = x.shape[-1]
    y = lax.conv_general_dilated(x, w[:, None, :].astype(x.dtype), window_strides=(1,), padding='SAME',
                                 dimension_numbers=('NWC', 'WIO', 'NWC'), feature_group_count=ch)
    return y + bias.astype(x.dtype)


def _ada(cvec, w, b):
    m = jax.nn.silu(cvec) @ w + b
    return jnp.split(m.reshape(m.shape[0], 1, 6 * D_MODEL), 6, axis=-1)


def _adaln(x, g, shift, scale):
    return _rmsnorm(x, g) * (1 + scale) + shift


def _grid_angles(n_tokens):
    rows = n_tokens // GRID_W
    r = jnp.repeat(jnp.arange(rows, dtype=F32), GRID_W)
    col = jnp.tile(jnp.arange(GRID_W, dtype=F32), rows)
    n_freq = DA_HEAD_DIM // 4
    inv = ROPE_BASE ** (-jnp.arange(n_freq, dtype=F32) / n_freq)
    shp = (1, n_tokens, 1, 1, n_freq)
    return (r[:, None] * inv).reshape(shp), (col[:, None] * inv).reshape(shp)


def _rot(x, ang):
    x1, x2 = jnp.split(x, 2, axis=-1)
    cos, sin = jnp.cos(ang).astype(x.dtype), jnp.sin(ang).astype(x.dtype)
    return jnp.concatenate([x1 * cos - x2 * sin, x2 * cos + x1 * sin], axis=-1)


def _rope2d(x, ang_row, ang_col):
    xr, xc = jnp.split(x, 2, axis=-1)
    return jnp.concatenate([_rot(xr, ang_row), _rot(xc, ang_col)], axis=-1)


def _diff_attention(q, k, v, lam):
    b, lq, nh, _, d = q.shape
    nq = lq // Q_BLOCK
    qb = jnp.moveaxis(q.reshape(b, nq, Q_BLOCK, nh, 2, d), 1, 0)
    vf = v.astype(F32)
    scale = d ** -0.5

    def block(qblk):
        s = jnp.einsum('bqhmd,bkhmd->bhmqk', qblk, k).astype(F32) * scale
        p = jax.nn.softmax(s, axis=-1)
        pdiff = p[:, :, 0] - lam * p[:, :, 1]
        return jnp.einsum('bhqk,bkhe->bqhe', pdiff, vf)

    o = lax.map(block, qb)
    return jnp.moveaxis(o, 0, 1).reshape(b, lq, nh, v.shape[-1])


def _ssd_scan(x, dt, a_neg, bm, cm, h0):
    b, L, h, p = x.shape
    n = bm.shape[-1]
    nc = L // CHUNK
    xdt = (x * dt[..., None]).reshape(b, nc, CHUNK, h, p)
    la = (dt * a_neg).reshape(b, nc, CHUNK, h)
    bm = bm.reshape(b, nc, CHUNK, h, n)
    cm = cm.reshape(b, nc, CHUNK, h, n)
    cs = jnp.cumsum(la, axis=2)
    causal = jnp.tril(jnp.ones((CHUNK, CHUNK), bool))[None, None, :, :, None]
    seg = cs[:, :, :, None, :] - cs[:, :, None, :, :]
    lmat = jnp.exp(jnp.where(causal, seg, -jnp.inf))
    scores = jnp.einsum('bclhn,bcshn->bclsh', cm, bm) * lmat
    y_diag = jnp.einsum('bclsh,bcshp->bclhp', scores, xdt)
    decay_end = jnp.exp(cs[:, :, -1:, :] - cs)
    chunk_states = jnp.einsum('bclhn,bclh,bclhp->bchpn', bm, decay_end, xdt)
    chunk_decay = jnp.exp(cs[:, :, -1, :])

    def step(state, inp):
        st, dec, c_i, cs_i = inp
        y_off = jnp.einsum('blhn,bhpn->blhp', c_i, state) * jnp.exp(cs_i)[..., None]
        return state * dec[:, :, None, None] + st, y_off

    h_fin, y_off = lax.scan(step, h0, (jnp.moveaxis(chunk_states, 1, 0), jnp.moveaxis(chunk_decay, 1, 0),
                                       jnp.moveaxis(cm, 1, 0), jnp.moveaxis(cs, 1, 0)))
    y = y_diag + jnp.moveaxis(y_off, 0, 1)
    return y.reshape(b, L, h, p), h_fin


def _gdn_scan(q, k, v, g, beta, s0):
    b, L, h, dk = q.shape
    dv = v.shape[-1]
    nc = L // CHUNK

    def chunks(t):
        return t.reshape(b, nc, CHUNK, h, -1).transpose(0, 1, 3, 2, 4)

    qc, kc, vc = chunks(q), chunks(k), chunks(v)
    gc = chunks(g[..., None])[..., 0]
    bc = chunks(beta[..., None])[..., 0]
    gcum = jnp.cumsum(gc, axis=-1)
    incl = jnp.tril(jnp.ones((CHUNK, CHUNK), bool))
    strict = jnp.tril(jnp.ones((CHUNK, CHUNK), bool), -1)
    decay = jnp.exp(jnp.where(incl, gcum[..., :, None] - gcum[..., None, :], -jnp.inf))
    kb = kc * bc[..., None]
    m = jnp.where(strict, jnp.einsum('bnhid,bnhjd->bnhij', kb, kc) * decay, 0.0)
    a = m + jnp.eye(CHUNK, dtype=m.dtype)
    rhs = jnp.concatenate([vc * bc[..., None], kb * jnp.exp(gcum)[..., None]], axis=-1)
    sol = lax.linalg.triangular_solve(a, rhs, left_side=True, lower=True, unit_diagonal=True)
    u, w = sol[..., :dv], sol[..., dv:]
    attn = jnp.einsum('bnhid,bnhjd->bnhij', qc, kc) * decay
    qg = qc * jnp.exp(gcum)[..., None]
    kg = kc * jnp.exp(gcum[..., -1:] - gcum)[..., None]
    gl = jnp.exp(gcum[..., -1])

    def step(s, inp):
        u_i, w_i, a_i, qg_i, kg_i, gl_i = inp
        v_new = u_i - jnp.einsum('bhid,bhde->bhie', w_i, s)
        o_i = jnp.einsum('bhid,bhde->bhie', qg_i, s) + jnp.einsum('bhij,bhje->bhie', a_i, v_new)
        s = s * gl_i[..., None, None] + jnp.einsum('bhid,bhie->bhde', kg_i, v_new)
        return s, o_i

    s_fin, o = lax.scan(step, s0, tuple(jnp.moveaxis(t, 1, 0) for t in (u, w, attn, qg, kg, gl)))
    o = o.transpose(1, 0, 3, 2, 4).reshape(b, L, h, dv)
    return o, s_fin


def _even_mixer(h, w_in, conv_w, conv_b, dt_bias, a_log, d_skip, ssd_g, lam_p, da_g, w_out,
                lam_init, angles, ctx_k, ctx_v, h0):
    b, L, _ = h.shape
    z, xbc, dt_raw, q, k, v = _split(h @ w_in, (SSD_WIDTH, SSD_XBC, 2 * SSD_HEADS, DA_QK, DA_QK, DA_V))
    xbc = jax.nn.silu(_dwconv(xbc, conv_w, conv_b)).astype(F32)
    xs, bm, cm = _split(xbc, (SSD_WIDTH, SSD_GROUPS * SSD_STATE, SSD_GROUPS * SSD_STATE))
    xs = xs.reshape(b, L, SSD_HEADS, SSD_HEAD_DIM)
    rep = SSD_HEADS // SSD_GROUPS
    bm = jnp.repeat(bm.reshape(b, L, SSD_GROUPS, SSD_STATE), rep, axis=2)
    cm = jnp.repeat(cm.reshape(b, L, SSD_GROUPS, SSD_STATE), rep, axis=2)
    dt = jax.nn.softplus(dt_raw.astype(F32).reshape(b, L, 2, SSD_HEADS) + dt_bias.astype(F32))
    a_neg = -jnp.exp(a_log.astype(F32))
    if h0 is None:
        h0 = jnp.zeros((b, 2, SSD_HEADS, SSD_HEAD_DIM, SSD_STATE), F32)
    h0 = h0.astype(F32)
    y_f, s_f = _ssd_scan(xs, dt[:, :, 0], a_neg[0], bm, cm, h0[:, 0])
    y_b, s_b = _ssd_scan(jnp.flip(xs, 1), jnp.flip(dt[:, :, 1], 1), a_neg[1],
                         jnp.flip(bm, 1), jnp.flip(cm, 1), h0[:, 1])
    y = y_f + jnp.flip(y_b, 1) + d_skip.astype(F32)[:, None] * xs
    y = _rmsnorm(y.reshape(b, L, SSD_WIDTH) * jax.nn.silu(z.astype(F32)), ssd_g).astype(h.dtype)
    q = q.reshape(b, L, DA_HEADS, 2, DA_HEAD_DIM)
    k = k.reshape(b, L, DA_HEADS, 2, DA_HEAD_DIM)
    v = v.reshape(b, L, DA_HEADS, DA_V_DIM)
    k_store = k.reshape(b, L, DA_HEADS, 2 * DA_HEAD_DIM)
    v_store = v
    if angles is not None:
        q = _rope2d(q, *angles)
        k = _rope2d(k, *angles)
    if ctx_k is not None:
        k = jnp.concatenate([ctx_k.reshape(b, -1, DA_HEADS, 2, DA_HEAD_DIM).astype(k.dtype), k], axis=1)
        v = jnp.concatenate([ctx_v.astype(v.dtype), v], axis=1)
    lp = lam_p.astype(F32)
    lam = jnp.exp(jnp.sum(lp[0] * lp[1])) - jnp.exp(jnp.sum(lp[2] * lp[3])) + lam_init
    o = _diff_attention(q, k, v, lam)
    o = (_rmsnorm(o, da_g) * (1.0 - lam_init)).astype(h.dtype)
    out = jnp.concatenate([y, o.reshape(b, L, DA_V)], axis=-1) @ w_out
    return out, k_store, v_store, jnp.stack([s_f, s_b], axis=1)


def _odd_mixer(h, w_in, conv_w, conv_b, a_log, dt_bias, g_norm, w_out, s0):
    b, L, _ = h.shape
    qkv, z, ab = _split(h @ w_in, (GDN_QKV, GDN_WIDTH, 4 * GDN_HEADS))
    qkv = jax.nn.silu(_dwconv(qkv, conv_w, conv_b)).astype(F32)
    q, k, v = _split(qkv, (GDN_HEADS * GDN_K, GDN_HEADS * GDN_K, GDN_WIDTH))
    q = _l2norm(q.reshape(b, L, GDN_HEADS, GDN_K)) * (GDN_K ** -0.5)
    k = _l2norm(k.reshape(b, L, GDN_HEADS, GDN_K))
    v = v.reshape(b, L, GDN_HEADS, GDN_V)
    ab = ab.astype(F32).reshape(b, L, 2, 2, GDN_HEADS)
    g = -jnp.exp(a_log.astype(F32)) * jax.nn.softplus(ab[:, :, :, 0] + dt_bias.astype(F32))
    beta = jax.nn.sigmoid(ab[:, :, :, 1])
    if s0 is None:
        s0 = jnp.zeros((b, 2, GDN_HEADS, GDN_K, GDN_V), F32)
    s0 = s0.astype(F32)
    o_f, s_f = _gdn_scan(q, k, v, g[:, :, 0], beta[:, :, 0], s0[:, 0])
    o_b, s_b = _gdn_scan(jnp.flip(q, 1), jnp.flip(k, 1), jnp.flip(v, 1), jnp.flip(g[:, :, 1], 1),
                         jnp.flip(beta[:, :, 1], 1), s0[:, 1])
    o = o_f + jnp.flip(o_b, 1)
    o = _rmsnorm(o, g_norm) * jax.nn.silu(z.astype(F32).reshape(b, L, GDN_HEADS, GDN_V))
    out = o.reshape(b, L, GDN_WIDTH).astype(h.dtype) @ w_out
    return out, jnp.stack([s_f, s_b], axis=1)


def _conv_ffn(h, w_up, conv_w, conv_b, w_down):
    u = _dwconv(h @ w_up, conv_w, conv_b)
    gate, val = jnp.split(u, 2, axis=-1)
    return (jax.nn.silu(gate) * val) @ w_down


def setup_inputs(seed: int = 0) -> dict:
    key = jax.random.key(seed)
    ks = iter(jax.random.split(key, 40))

    def nrm(shape, scale):
        return scale * jax.random.normal(next(ks), shape, F32)

    def gain(shape):
        return 1.0 + nrm(shape, 0.02)

    def dt_bias(shape):
        dt = jnp.exp(jax.random.uniform(next(ks), shape, F32, math.log(1e-3), math.log(1e-1)))
        return dt + jnp.log(-jnp.expm1(-dt))

    def a_log(shape):
        return jnp.log(jax.random.uniform(next(ks), shape, F32, 1.0, 16.0))

    D = D_MODEL
    return {
        "x_prompt": nrm((BATCH, SEQ, D), 1.0),
        "x_sample": nrm((DEC_BATCH, DEC_SEQ, D), 1.0),
        "cache_attn_k": nrm((DEC_BATCH, N_EVEN, PAST_LEN, DA_HEADS, 2 * DA_HEAD_DIM), 1.0),
        "cache_attn_v": nrm((DEC_BATCH, N_EVEN, PAST_LEN, DA_HEADS, DA_V_DIM), 1.0),
        "state_ssd": nrm((DEC_BATCH, N_EVEN, 2, SSD_HEADS, SSD_HEAD_DIM, SSD_STATE), 0.1),
        "state_gdn": nrm((DEC_BATCH, N_ODD, 2, GDN_HEADS, GDN_K, GDN_V), 0.1),
        "c": nrm((DEC_BATCH, D), 1.0),
        "c_ctx": nrm((D,), 1.0),
        "ada_w": nrm((DEPTH, D, 6 * D), 0.5 * D ** -0.5),
        "ada_b": nrm((DEPTH, 6 * D), 0.02),
        "norm_g": gain((DEPTH, 2, D)),
        "ev_w_in": nrm((N_EVEN, D, EVEN_IN), D ** -0.5),
        "ev_conv_w": nrm((N_EVEN, CONV_K, SSD_XBC), CONV_K ** -0.5),
        "ev_conv_b": nrm((N_EVEN, SSD_XBC), 0.02),
        "ssd_dt_bias": dt_bias((N_EVEN, 2, SSD_HEADS)),
        "ssd_a_log": a_log((N_EVEN, 2, SSD_HEADS)),
        "ssd_d": gain((N_EVEN, SSD_HEADS)),
        "ssd_norm_g": gain((N_EVEN, SSD_WIDTH)),
        "da_lambda": nrm((N_EVEN, 4, DA_HEAD_DIM), 0.1),
        "da_norm_g": gain((N_EVEN, DA_V_DIM)),
        "ev_w_out": nrm((N_EVEN, EVEN_OUT, D), EVEN_OUT ** -0.5),
        "od_w_in": nrm((N_ODD, D, ODD_IN), D ** -0.5),
        "od_conv_w": nrm((N_ODD, CONV_K, GDN_QKV), CONV_K ** -0.5),
        "od_conv_b": nrm((N_ODD, GDN_QKV), 0.02),
        "gdn_a_log": a_log((N_ODD, 2, GDN_HEADS)),
        "gdn_dt_bias": dt_bias((N_ODD, 2, GDN_HEADS)),
        "gdn_norm_g": gain((N_ODD, GDN_V)),
        "od_w_out": nrm((N_ODD, GDN_WIDTH, D), GDN_WIDTH ** -0.5),
        "ffn_w_up": nrm((DEPTH, D, 2 * D_FF), D ** -0.5),
        "ffn_conv_w": nrm((DEPTH, CONV_K, 2 * D_FF), CONV_K ** -0.5),
        "ffn_conv_b": nrm((DEPTH, 2 * D_FF), 0.02),
        "ffn_w_down": nrm((DEPTH, D_FF, D), D_FF ** -0.5),
        "final_norm_g": gain((D,)),
    }


def reference(x_prompt, x_sample, cache_attn_k, cache_attn_v, state_ssd, state_gdn, c, c_ctx,
              ada_w, ada_b, norm_g,
              ev_w_in, ev_conv_w, ev_conv_b, ssd_dt_bias, ssd_a_log, ssd_d, ssd_norm_g,
              da_lambda, da_norm_g, ev_w_out,
              od_w_in, od_conv_w, od_conv_b, gdn_a_log, gdn_dt_bias, gdn_norm_g, od_w_out,
              ffn_w_up, ffn_conv_w, ffn_conv_b, ffn_w_down, final_norm_g):
    angles = _grid_angles(x_sample.shape[1])
    xc, xl = x_prompt, x_sample
    ks_out, vs_out, ssd_out, gdn_out = [], [], [], []
    for l in range(DEPTH):
        mc = _ada(c_ctx[None, :], ada_w[l], ada_b[l])
        ml = _ada(c, ada_w[l], ada_b[l])
        hc = _adaln(xc, norm_g[l, 0], mc[0], mc[1])
        hl = _adaln(xl, norm_g[l, 0], ml[0], ml[1])
        if l % 2 == 0:
            e = l // 2
            lam_init = 0.8 - 0.6 * math.exp(-0.3 * l)
            prm = (ev_w_in[e], ev_conv_w[e], ev_conv_b[e], ssd_dt_bias[e], ssd_a_log[e], ssd_d[e],
                   ssd_norm_g[e], da_lambda[e], da_norm_g[e], ev_w_out[e])
            oc, k_ctx, v_ctx, s_ctx = _even_mixer(hc, *prm, lam_init, None, None, None, None)
            ol = _even_mixer(hl, *prm, lam_init, angles, cache_attn_k[:, e], cache_attn_v[:, e],
                             state_ssd[:, e])[0]
            ks_out.append(k_ctx)
            vs_out.append(v_ctx)
            ssd_out.append(s_ctx)
        else:
            o = l // 2
            prm = (od_w_in[o], od_conv_w[o], od_conv_b[o], gdn_a_log[o], gdn_dt_bias[o],
                   gdn_norm_g[o], od_w_out[o])
            oc, s_ctx = _odd_mixer(hc, *prm, None)
            ol = _odd_mixer(hl, *prm, state_gdn[:, o])[0]
            gdn_out.append(s_ctx)
        xc = xc + mc[2] * oc
        xl = xl + ml[2] * ol
        fp = (ffn_w_up[l], ffn_conv_w[l], ffn_conv_b[l], ffn_w_down[l])
        xc = xc + mc[5] * _conv_ffn(_adaln(xc, norm_g[l, 1], mc[3], mc[4]), *fp)
        xl = xl + ml[5] * _conv_ffn(_adaln(xl, norm_g[l, 1], ml[3], ml[4]), *fp)
    y_prompt = _rmsnorm(xc, final_norm_g)
    y_sample = _rmsnorm(xl, final_norm_g)
    dtype = x_prompt.dtype
    new_attn_k = jnp.stack(ks_out, axis=1).astype(dtype)
    new_attn_v = jnp.stack(vs_out, axis=1).astype(dtype)
    new_state_ssd = jnp.stack(ssd_out, axis=1).astype(dtype)
    new_state_gdn = jnp.stack(gdn_out, axis=1).astype(dtype)
    return (y_prompt, y_sample, new_attn_k, new_attn_v, new_state_ssd, new_state_gdn)
```

```python
import functools
import math

import jax
import jax.numpy as jnp
from jax import lax
from jax.experimental import pallas as pl
from jax.experimental.pallas import tpu as pltpu

F32 = jnp.float32
BF16 = jnp.bfloat16
EPS = 1e-6
ROPE_BASE = 10000.0

LANES = 128
HALO = 8
VMEM_LIMIT_BYTES = 56 * 1024 * 1024

N_MOD = 6
MOD_ROWS = 8
SCAN_CHUNK = 256
GDN_CHUNK = 64
Q_TILE = 256
ROW_TILE = 1024
OUT_ROW_TILE = 512
FF_TILE = 256
IN_COL_TILE = 512


def _silu(x):
    return x / (1.0 + jnp.exp(-x))


def _softplus(x):
    return jnp.maximum(x, 0.0) + jnp.log1p(jnp.exp(-jnp.abs(x)))


def _dot(a, b):
    return jnp.dot(a.astype(BF16), b.astype(BF16), preferred_element_type=F32)


def _dot_nt(a, b):
    return lax.dot_general(a.astype(BF16), b.astype(BF16), (((1,), (1,)), ((), ())),
                           preferred_element_type=F32)


def _dot_tn(a, b):
    return lax.dot_general(a.astype(BF16), b.astype(BF16), (((0,), (0,)), ((), ())),
                           preferred_element_type=F32)


def _dot_f32(a, b):
    return jnp.dot(a, b, precision=lax.Precision.HIGHEST, preferred_element_type=F32)


def _expand(a, p):
    hi = a.astype(BF16)
    r1 = a - hi.astype(F32)
    mid = r1.astype(BF16)
    lo = (r1 - mid.astype(F32)).astype(BF16)
    d = functools.partial(jnp.dot, preferred_element_type=F32)
    return d(hi, p) + d(mid, p) + d(lo, p)


def _rms(x, g):
    ms = jnp.mean(x * x, axis=-1, keepdims=True)
    return x * lax.rsqrt(ms + EPS) * g


def _params(sem):
    return pltpu.CompilerParams(dimension_semantics=sem, vmem_limit_bytes=VMEM_LIMIT_BYTES)


def _conv3_silu(x, xp, xn, w_ref, b_ref, first, last):
    rows = x.shape[0]
    row = lax.broadcasted_iota(jnp.int32, x.shape, 0)
    prev_row = jnp.where(first, 0.0, xp[HALO - 1:HALO, :])
    next_row = jnp.where(last, 0.0, xn[0:1, :])
    x_prev = jnp.where(row == 0, prev_row, pltpu.roll(x, 1, 0))
    x_next = jnp.where(row == rows - 1, next_row, pltpu.roll(x, rows - 1, 0))
    y = x_prev * w_ref[0:1, :] + x * w_ref[1:2, :] + x_next * w_ref[2:3, :] + b_ref[...]
    return _silu(y)


def _mod_kernel(c_ref, w_ref, b_ref, o_ref):
    s = _silu(c_ref[...])
    o_ref[...] = _dot(s, w_ref[...]) + b_ref[...]


def _mod_call(cvec, ada_w, ada_b):
    depth, d, n = ada_w.shape
    tn = d
    return pl.pallas_call(
        _mod_kernel,
        grid=(depth, n // tn),
        in_specs=[pl.BlockSpec((MOD_ROWS, d), lambda l, j: (0, 0)),
                  pl.BlockSpec((None, d, tn), lambda l, j: (l, 0, j)),
                  pl.BlockSpec((None, 1, tn), lambda l, j: (l, 0, j))],
        out_specs=pl.BlockSpec((None, MOD_ROWS, tn), lambda l, j: (l, 0, j)),
        out_shape=jax.ShapeDtypeStruct((depth, MOD_ROWS, n), F32),
        compiler_params=_params(("arbitrary", "arbitrary")),
        name="ada_mod",
    )(cvec, ada_w, ada_b.reshape(depth, 1, n))


class _Geom:
    def __init__(self, n_ctx, s_ctx, n_lat, s_lat):
        self.n_ctx, self.s_ctx, self.n_lat, self.s_lat = n_ctx, s_ctx, n_lat, s_lat
        self.ctx_rows = n_ctx * s_ctx
        self.rows = self.ctx_rows + n_lat * s_lat

    def row_tile(self, want):
        t = want
        while self.ctx_rows % t or self.s_lat % t:
            t //= 2
        return t

    def group(self, tile):
        def fn(i):
            r = i * tile
            return jnp.where(r < self.ctx_rows, 0, 1 + jnp.maximum(r - self.ctx_rows, 0) // self.s_lat)
        return fn


def _mod_spec(d, layer, which, group_fn):
    return pl.BlockSpec((None, 1, d), lambda i, *_: ((layer * N_MOD + which) * MOD_ROWS + group_fn(i), 0, 0))


def _inproj_kernel(x_ref, g_ref, sh_ref, sc_ref, w_ref, o_ref, h_ref):
    @pl.when(pl.program_id(1) == 0)
    def _():
        h_ref[...] = (_rms(x_ref[...], g_ref[...]) * (1.0 + sc_ref[...]) + sh_ref[...]).astype(BF16)

    o_ref[...] = jnp.dot(h_ref[...], w_ref[...], preferred_element_type=F32)


def _inproj_call(x, mods, layer, norm_g, w, geom):
    rows, d = x.shape
    n = w.shape[1]
    tm = geom.row_tile(ROW_TILE)
    tn = IN_COL_TILE
    grp = geom.group(tm)
    return pl.pallas_call(
        _inproj_kernel,
        grid=(rows // tm, n // tn),
        in_specs=[pl.BlockSpec((tm, d), lambda i, j: (i, 0)),
                  pl.BlockSpec((1, d), lambda i, j: (0, 0)),
                  _mod_spec(d, layer, 0, grp),
                  _mod_spec(d, layer, 1, grp),
                  pl.BlockSpec((d, tn), lambda i, j: (0, j))],
        out_specs=pl.BlockSpec((tm, tn), lambda i, j: (i, j)),
        out_shape=jax.ShapeDtypeStruct((rows, n), F32),
        scratch_shapes=[pltpu.VMEM((tm, d), BF16)],
        compiler_params=_params(("arbitrary", "arbitrary")),
        name="in_proj",
    )(x, norm_g.reshape(1, d), mods, mods, w)


def _outproj_kernel(n_in, x_ref, gate_ref, *refs):
    a_refs, w_refs, o_ref = refs[:n_in], refs[n_in:2 * n_in], refs[2 * n_in]
    acc = jnp.dot(a_refs[0][...], w_refs[0][...], preferred_element_type=F32)
    for a_ref, w_ref in zip(a_refs[1:], w_refs[1:]):
        acc = acc + jnp.dot(a_ref[...], w_ref[...], preferred_element_type=F32)
    o_ref[...] = x_ref[...] + gate_ref[...] * acc


def _outproj_call(x, mods, layer, acts, ws, geom):
    rows, d = x.shape
    tm = geom.row_tile(OUT_ROW_TILE)
    grp = geom.group(tm)
    n_in = len(acts)
    in_specs = [pl.BlockSpec((tm, d), lambda i: (i, 0)), _mod_spec(d, layer, 2, grp)]
    in_specs += [pl.BlockSpec((tm, a.shape[1]), lambda i: (i, 0)) for a in acts]
    in_specs += [pl.BlockSpec(w.shape, lambda i: (0, 0)) for w in ws]
    return pl.pallas_call(
        functools.partial(_outproj_kernel, n_in),
        grid=(rows // tm,),
        in_specs=in_specs,
        out_specs=pl.BlockSpec((tm, d), lambda i: (i, 0)),
        out_shape=jax.ShapeDtypeStruct((rows, d), F32),
        compiler_params=_params(("arbitrary",)),
        name="out_proj",
    )(x, mods, *acts, *ws)


def _ffn_kernel(geom, x_ref, xp_ref, xn_ref, g_ref, sh_ref, sc_ref, gate_ref,
                wg_ref, wv_ref, cwg_ref, cwv_ref, cbg_ref, cbv_ref, wd_ref, o_ref, h_ref, acc_ref):
    i, j = pl.program_id(0), pl.program_id(1)
    tm = x_ref.shape[0]

    @pl.when(j == 0)
    def _():
        def hn(x):
            return (_rms(x, g_ref[...]) * (1.0 + sc_ref[...]) + sh_ref[...]).astype(BF16)
        h_ref[0:HALO, :] = hn(xp_ref[...])
        h_ref[HALO:HALO + tm, :] = hn(x_ref[...])
        h_ref[HALO + tm:, :] = hn(xn_ref[...])
        acc_ref[...] = jnp.zeros_like(acc_ref)

    h = h_ref[...]
    ug = jnp.dot(h, wg_ref[...], preferred_element_type=F32)
    uv = jnp.dot(h, wv_ref[...], preferred_element_type=F32)
    r0 = i * tm
    is_ctx = r0 < geom.ctx_rows
    base = jnp.where(is_ctx, r0, r0 - geom.ctx_rows)
    lseq = jnp.where(is_ctx, geom.s_ctx, geom.s_lat)
    pos = (base + lax.broadcasted_iota(jnp.int32, (tm, ug.shape[1]), 0)) & (lseq - 1)
    at_start, at_end = pos == 0, pos == lseq - 1

    def conv(u, cw_ref, cb_ref):
        up = jnp.where(at_start, 0.0, pltpu.roll(u, 1, 0)[HALO:HALO + tm])
        un = jnp.where(at_end, 0.0, pltpu.roll(u, tm + 2 * HALO - 1, 0)[HALO:HALO + tm])
        return up * cw_ref[0:1, :] + u[HALO:HALO + tm] * cw_ref[1:2, :] + un * cw_ref[2:3, :] + cb_ref[...]

    a = _silu(conv(ug, cwg_ref, cbg_ref)) * conv(uv, cwv_ref, cbv_ref)
    acc_ref[...] += jnp.dot(a.astype(BF16), wd_ref[...], preferred_element_type=F32)

    @pl.when(j == pl.num_programs(1) - 1)
    def _():
        o_ref[...] = x_ref[...] + gate_ref[...] * acc_ref[...]


def _ffn_call(x, mods, layer, norm_g, w_up, conv_w, conv_b, w_down, geom):
    rows, d = x.shape
    dff = w_down.shape[0]
    tm = geom.row_tile(ROW_TILE)
    tf = FF_TILE
    nf = dff // tf
    grp = geom.group(tm)
    nhalo = rows // HALO
    conv_b = conv_b.reshape(1, 2 * dff)
    return pl.pallas_call(
        functools.partial(_ffn_kernel, geom),
        grid=(rows // tm, nf),
        in_specs=[pl.BlockSpec((tm, d), lambda i, j: (i, 0)),
                  pl.BlockSpec((HALO, d), lambda i, j: (jnp.maximum(i * (tm // HALO) - 1, 0), 0)),
                  pl.BlockSpec((HALO, d), lambda i, j: (jnp.minimum((i + 1) * (tm // HALO), nhalo - 1), 0)),
                  pl.BlockSpec((1, d), lambda i, j: (0, 0)),
                  _mod_spec(d, layer, 3, grp), _mod_spec(d, layer, 4, grp), _mod_spec(d, layer, 5, grp),
                  pl.BlockSpec((d, tf), lambda i, j: (0, j)),
                  pl.BlockSpec((d, tf), lambda i, j: (0, nf + j)),
                  pl.BlockSpec((3, tf), lambda i, j: (0, j)),
                  pl.BlockSpec((3, tf), lambda i, j: (0, nf + j)),
                  pl.BlockSpec((1, tf), lambda i, j: (0, j)),
                  pl.BlockSpec((1, tf), lambda i, j: (0, nf + j)),
                  pl.BlockSpec((tf, d), lambda i, j: (j, 0))],
        out_specs=pl.BlockSpec((tm, d), lambda i, j: (i, 0)),
        out_shape=jax.ShapeDtypeStruct((rows, d), F32),
        scratch_shapes=[pltpu.VMEM((tm + 2 * HALO, d), BF16), pltpu.VMEM((tm, d), F32)],
        compiler_params=_params(("arbitrary", "arbitrary")),
        name="conv_ffn",
    )(x, x, x, norm_g.reshape(1, d), mods, mods, mods, w_up, w_up, conv_w, conv_w, conv_b, conv_b, w_down)


def _final_kernel(x_ref, g_ref, o_ref):
    o_ref[...] = _rms(x_ref[...], g_ref[...])


def _final_call(x, g, row0, nrows):
    d = x.shape[1]
    tm = OUT_ROW_TILE
    while nrows % tm or row0 % tm:
        tm //= 2
    return pl.pallas_call(
        _final_kernel,
        grid=(nrows // tm,),
        in_specs=[pl.BlockSpec((tm, d), lambda i: (row0 // tm + i, 0)), pl.BlockSpec((1, d), lambda i: (0, 0))],
        out_specs=pl.BlockSpec((tm, d), lambda i: (i, 0)),
        out_shape=jax.ShapeDtypeStruct((nrows, d), F32),
        compiler_params=_params(("arbitrary",)),
        name="final_norm",
    )(x, g.reshape(1, d))


def _ssd_kernel(fwd, nc, heads, *refs):
    (xs_ref, xsp_ref, xsn_ref, bc_ref, bcp_ref, bcn_ref, dt_ref, cwx_ref, cbx_ref, cwb_ref, cbb_ref,
     dtb_ref, alog_ref, pe_ref, bd_ref, h0_ref) = refs[:16]
    if fwd:
        z_ref, yb_ref, dx_ref, gn_ref, out_ref, st_ref, s_ref, yacc_ref = refs[16:]
    else:
        out_ref, st_ref, s_ref = refs[16:]
    j = pl.program_id(1)
    c = j if fwd else nc - 1 - j
    first, last = c == 0, c == nc - 1
    q = xs_ref.shape[0]
    half = LANES // 2

    @pl.when(j == 0)
    def _():
        s_ref[...] = h0_ref[...]

    xs = _conv3_silu(xs_ref[...], xsp_ref[...], xsn_ref[...], cwx_ref, cbx_ref, first, last)
    bc = _conv3_silu(bc_ref[...], bcp_ref[...], bcn_ref[...], cwb_ref, cbb_ref, first, last)
    bm, cm = bc[:, :LANES], bc[:, LANES:]
    dt = _softplus(dt_ref[...] + dtb_ref[...])
    la = dt * (-jnp.exp(alog_ref[...]))
    row = lax.broadcasted_iota(jnp.int32, (q, q), 0)
    col = lax.broadcasted_iota(jnp.int32, (q, q), 1)
    tril, triu = row >= col, row <= col
    cs_p = _dot_f32(tril.astype(F32), la)
    cs_s = _dot_f32(triu.astype(F32), la)
    cs = cs_p if fwd else cs_s
    edge = q - 1 if fwd else 0
    ecs_x = _expand(jnp.exp(cs), pe_ref[...])
    wd_x = _expand(dt * jnp.exp(cs[edge:edge + 1, :] - cs), pe_ref[...])
    s_old = s_ref[...]
    y_off = _dot(cm, s_old) * ecs_x
    s_new = (s_old * ecs_x[edge:edge + 1, :] + _dot_tn(bm, xs * wd_x)) * bd_ref[...]
    s_ref[...] = s_new

    @pl.when(j == nc - 1)
    def _():
        st_ref[...] = s_new

    if not fwd:
        out_ref[...] = y_off
        return

    cst_p, cst_s, dtt = cs_p.T, cs_s.T, dt.T
    lane = lax.broadcasted_iota(jnp.int32, (q, LANES), 1)
    lo, hi = lane < half, lane >= half
    gmat = (_dot_nt(jnp.where(lo, cm, 0.0), bm), _dot_nt(jnp.where(hi, cm, 0.0), bm))
    for p in range(heads // 2):
        xp = xs[:, p * LANES:(p + 1) * LANES]
        acc = None
        for hh in range(2):
            h = 2 * p + hh
            hb = heads + h
            lf = jnp.where(tril, jnp.exp(jnp.minimum(cs_p[:, h:h + 1] - cst_p[h:h + 1, :], 0.0)), 0.0)
            lb = jnp.where(triu, jnp.exp(jnp.minimum(cs_s[:, hb:hb + 1] - cst_s[hb:hb + 1, :], 0.0)), 0.0)
            w = gmat[h // (heads // 2)] * (lf * dtt[h:h + 1, :] + lb * dtt[hb:hb + 1, :])
            part = _dot(w, jnp.where(lo if hh == 0 else hi, xp, 0.0))
            acc = part if acc is None else acc + part
        yacc_ref[:, p * LANES:(p + 1) * LANES] = acc
    y = yacc_ref[...] + y_off + yb_ref[...] + dx_ref[...] * xs
    yz = y * _silu(z_ref[...])
    out_ref[...] = _rms(yz, gn_ref[...]).astype(BF16)


def _ssd_call(fwd, proj, yb, prm, h0, row0, n_seq, seq_len, heads, width):
    rows = proj.shape[0]
    q = min(SCAN_CHUNK, seq_len)
    nc = seq_len // q
    qb = row0 // q
    nhalo = rows // HALO
    bc_w = 2 * LANES
    xs_blk, z_blk, bc_blk, dt_blk = 0, 1, 5 * width // bc_w, (5 * width + bc_w) // LANES

    def cidx(j):
        return j if fwd else nc - 1 - j

    def rowblk(s, j):
        return qb + s * nc + cidx(j)

    def prev(s, j):
        return jnp.maximum(rowblk(s, j) * (q // HALO) - 1, 0)

    def nxt(s, j):
        return jnp.minimum((rowblk(s, j) + 1) * (q // HALO), nhalo - 1)

    const = lambda s, j: (0, 0)
    in_specs = [pl.BlockSpec((q, width), lambda s, j: (rowblk(s, j), xs_blk)),
                pl.BlockSpec((HALO, width), lambda s, j: (prev(s, j), xs_blk)),
                pl.BlockSpec((HALO, width), lambda s, j: (nxt(s, j), xs_blk)),
                pl.BlockSpec((q, bc_w), lambda s, j: (rowblk(s, j), bc_blk)),
                pl.BlockSpec((HALO, bc_w), lambda s, j: (prev(s, j), bc_blk)),
                pl.BlockSpec((HALO, bc_w), lambda s, j: (nxt(s, j), bc_blk)),
                pl.BlockSpec((q, LANES), lambda s, j: (rowblk(s, j), dt_blk)),
                pl.BlockSpec((3, width), const), pl.BlockSpec((1, width), const),
                pl.BlockSpec((3, bc_w), const), pl.BlockSpec((1, bc_w), const),
                pl.BlockSpec((1, LANES), const), pl.BlockSpec((1, LANES), const),
                pl.BlockSpec((LANES, width), const), pl.BlockSpec((LANES, width), const),
                pl.BlockSpec((None, LANES, width), lambda s, j: (s, 0, 0))]
    args = [proj, proj, proj, proj, proj, proj, proj, prm["cwx"], prm["cbx"], prm["cwb"], prm["cbb"],
            prm["dtb"], prm["alog"], prm["pf"] if fwd else prm["pb"], prm["bd"], h0]
    scratch = [pltpu.VMEM((LANES, width), F32)]
    if fwd:
        in_specs += [pl.BlockSpec((q, width), lambda s, j: (rowblk(s, j), z_blk)),
                     pl.BlockSpec((q, width), lambda s, j: (s * nc + j, 0)),
                     pl.BlockSpec((1, width), const), pl.BlockSpec((1, width), const)]
        args += [proj, yb, prm["dx"], prm["gn"]]
        scratch.append(pltpu.VMEM((q, width), F32))
        out_dtype = BF16
    else:
        out_dtype = F32
    return pl.pallas_call(
        functools.partial(_ssd_kernel, fwd, nc, heads),
        grid=(n_seq, nc),
        in_specs=in_specs,
        out_specs=[pl.BlockSpec((q, width), lambda s, j: (s * nc + cidx(j), 0)),
                   pl.BlockSpec((None, LANES, width), lambda s, j: (s, 0, 0))],
        out_shape=[jax.ShapeDtypeStruct((n_seq * seq_len, width), out_dtype),
                   jax.ShapeDtypeStruct((n_seq, LANES, width), F32)],
        scratch_shapes=scratch,
        compiler_params=_params(("arbitrary", "arbitrary")),
        name="ssd_fwd" if fwd else "ssd_bwd",
    )(*args)


def _rope(x, cos, sin_signed):
    lane = lax.broadcasted_iota(jnp.int32, x.shape, 1)
    partner = jnp.where((lane & 31) < 16, pltpu.roll(x, LANES - 16, 1), pltpu.roll(x, 16, 1))
    return x * cos + partner * sin_signed


def _attn_kernel(latent, lam_init, *refs):
    if latent:
        (q_ref, k_ref, v_ref, kc_ref, vc_ref, cq_ref, sq_ref, ck_ref, sk_ref, lp_ref, g_ref,
         o_ref, kr_ref, vr_ref) = refs
    else:
        q_ref, k_ref, v_ref, lp_ref, g_ref, o_ref, kr_ref, vr_ref = refs

    @pl.when(pl.program_id(2) == 0)
    def _():
        k = k_ref[...]
        if latent:
            k = _rope(k, ck_ref[...], sk_ref[...])
        kr_ref[...] = k.astype(BF16)
        vr_ref[...] = v_ref[...].astype(BF16)

    q = q_ref[...]
    if latent:
        q = _rope(q, cq_ref[...], sq_ref[...])
    q = q * (float(LANES // 2) ** -0.5)
    lane = lax.broadcasted_iota(jnp.int32, q.shape, 1)
    qs = (jnp.where(lane < LANES // 2, q, 0.0).astype(BF16), jnp.where(lane >= LANES // 2, q, 0.0).astype(BF16))
    keys = [kr_ref[...]] + ([kc_ref[...].astype(BF16)] if latent else [])
    vals = [vr_ref[...]] + ([vc_ref[...].astype(BF16)] if latent else [])
    lp = lp_ref[...]
    lam = (jnp.exp(jnp.sum(lp[0:1] * lp[1:2], axis=1, keepdims=True))
           - jnp.exp(jnp.sum(lp[2:3] * lp[3:4], axis=1, keepdims=True)) + lam_init)
    probs, coef = [], []
    for m in range(2):
        s = [lax.dot_general(qs[m], kk, (((1,), (1,)), ((), ())), preferred_element_type=F32) for kk in keys]
        mx = functools.reduce(jnp.maximum, [jnp.max(t, axis=1, keepdims=True) for t in s])
        p = [jnp.exp(t - mx) for t in s]
        den = functools.reduce(jnp.add, [jnp.sum(t, axis=1, keepdims=True) for t in p])
        probs.append(p)
        coef.append(1.0 / den if m == 0 else lam / den)
    o = None
    for seg in range(len(keys)):
        pd = (probs[0][seg] * coef[0] - probs[1][seg] * coef[1]).astype(BF16)
        part = jnp.dot(pd, vals[seg], preferred_element_type=F32)
        o = part if o is None else o + part
    o_ref[...] = (_rms(o, g_ref[...]) * (1.0 - lam_init)).astype(BF16)


def _attn_call(latent, lam_init, proj, cache_k, cache_v, tables, lam_p, norm_g, row0, n_seq, seq_len, heads, width):
    tq = min(Q_TILE, seq_len)
    nq = seq_len // tq
    hpw = width // LANES
    q_blk, k_blk, v_blk = 2 * hpw, 3 * hpw, 4 * hpw
    qb, sb = row0 // tq, row0 // seq_len
    in_specs = [pl.BlockSpec((tq, LANES), lambda b, h, i: (qb + b * nq + i, q_blk + h)),
                pl.BlockSpec((seq_len, LANES), lambda b, h, i: (sb + b, k_blk + h)),
                pl.BlockSpec((seq_len, LANES), lambda b, h, i: (sb + b, v_blk + h))]
    args = [proj, proj, proj]
    if latent:
        past = cache_k.shape[1]
        in_specs += [pl.BlockSpec((None, past, LANES), lambda b, h, i: (b, 0, h)),
                     pl.BlockSpec((None, past, LANES), lambda b, h, i: (b, 0, h)),
                     pl.BlockSpec((tq, LANES), lambda b, h, i: (i, 0)),
                     pl.BlockSpec((tq, LANES), lambda b, h, i: (i, 0)),
                     pl.BlockSpec((seq_len, LANES), lambda b, h, i: (0, 0)),
                     pl.BlockSpec((seq_len, LANES), lambda b, h, i: (0, 0))]
        args += [cache_k, cache_v, tables[0], tables[1], tables[0], tables[1]]
    in_specs += [pl.BlockSpec(lam_p.shape, lambda b, h, i: (0, 0)), pl.BlockSpec((1, LANES), lambda b, h, i: (0, 0))]
    args += [lam_p, norm_g.reshape(1, LANES)]
    return pl.pallas_call(
        functools.partial(_attn_kernel, latent, lam_init),
        grid=(n_seq, heads, nq),
        in_specs=in_specs,
        out_specs=pl.BlockSpec((tq, LANES), lambda b, h, i: (b * nq + i, h)),
        out_shape=jax.ShapeDtypeStruct((n_seq * seq_len, width), BF16),
        scratch_shapes=[pltpu.VMEM((seq_len, LANES), BF16), pltpu.VMEM((seq_len, LANES), BF16)],
        compiler_params=_params(("arbitrary", "arbitrary", "arbitrary")),
        name="diff_attn_lat" if latent else "diff_attn_ctx",
    )(*args)


def _rope_tables(n_tokens, grid_w):
    n_freq = LANES // 8
    pos = jnp.arange(n_tokens)
    r = (pos // grid_w).astype(F32)
    cpos = (pos % grid_w).astype(F32)
    inv = ROPE_BASE ** (-jnp.arange(n_freq, dtype=F32) / n_freq)
    ang_r, ang_c = r[:, None] * inv, cpos[:, None] * inv
    cos32 = lambda a: jnp.concatenate([jnp.cos(a), jnp.cos(a)], axis=1)
    sin32 = lambda a: jnp.concatenate([-jnp.sin(a), jnp.sin(a)], axis=1)
    cos64 = jnp.concatenate([cos32(ang_r), cos32(ang_c)], axis=1)
    sin64 = jnp.concatenate([sin32(ang_r), sin32(ang_c)], axis=1)
    return jnp.concatenate([cos64, cos64], axis=1), jnp.concatenate([sin64, sin64], axis=1)


def _gdn_kernel(fwd, nb, heads, *refs):
    (q_ref, qp_ref, qn_ref, k_ref, kp_ref, kn_ref, v_ref, vp_ref, vn_ref, ab_ref,
     cwq_ref, cbq_ref, cwk_ref, cbk_ref, cwv_ref, cbv_ref, alog_ref, dtb_ref, s0_ref) = refs[:19]
    if fwd:
        z_ref, ob_ref, gn_ref, out_ref, st_ref, s_ref = refs[19:]
    else:
        out_ref, st_ref, s_ref = refs[19:]
    hd, j = pl.program_id(1), pl.program_id(2)
    c = j if fwd else nb - 1 - j
    first, last = c == 0, c == nb - 1
    rb, dk = q_ref.shape
    ch = GDN_CHUNK
    nsub = rb // ch
    shift = ch.bit_length() - 1

    @pl.when(j == 0)
    def _():
        s_ref[...] = s0_ref[...]

    qc = _conv3_silu(q_ref[...], qp_ref[...], qn_ref[...], cwq_ref, cbq_ref, first, last)
    kc = _conv3_silu(k_ref[...], kp_ref[...], kn_ref[...], cwk_ref, cbk_ref, first, last)
    v = _conv3_silu(v_ref[...], vp_ref[...], vn_ref[...], cwv_ref, cbv_ref, first, last)
    qn = qc * lax.rsqrt(jnp.sum(qc * qc, axis=1, keepdims=True) + EPS) * (float(dk) ** -0.5)
    kn = kc * lax.rsqrt(jnp.sum(kc * kc, axis=1, keepdims=True) + EPS)

    ab = ab_ref[...]
    lane = lax.broadcasted_iota(jnp.int32, ab.shape, 1)
    d0 = 0 if fwd else 2 * heads
    g_all = -jnp.exp(alog_ref[...]) * _softplus(ab + dtb_ref[...])
    b_all = 1.0 / (1.0 + jnp.exp(-ab))
    g = jnp.sum(jnp.where(lane == d0 + hd, g_all, 0.0), axis=1, keepdims=True)
    beta = jnp.sum(jnp.where(lane == d0 + heads + hd, b_all, 0.0), axis=1, keepdims=True)

    row = lax.broadcasted_iota(jnp.int32, (rb, rb), 0)
    col = lax.broadcasted_iota(jnp.int32, (rb, rb), 1)
    blk = (row >> shift) == (col >> shift)
    incl = blk & ((row >= col) if fwd else (row <= col))
    strict = blk & ((row > col) if fwd else (row < col))
    gmat = jnp.broadcast_to(g, (rb, LANES))
    gc_m = _dot_f32(incl.astype(F32), gmat)
    gt_m = _dot_f32(blk.astype(F32), gmat)
    gc, gtot = gc_m[:, 0:1], gt_m[:, 0:1]
    g_row = gc_m.T[0:1, :]
    dec = jnp.where(incl, jnp.exp(jnp.minimum(gc - g_row, 0.0)), 0.0)
    kk = _dot_nt(kn, kn)
    qk = _dot_nt(qn, kn)
    m = jnp.where(strict, kk * dec, 0.0) * beta
    t = (row == col).astype(F32) - jnp.where((row >> 1) == (col >> 1), m, 0.0)
    lvl = 1
    while (1 << lvl) < ch:
        off = ((row >> (lvl + 1)) == (col >> (lvl + 1))) & ((row >> lvl) != (col >> lvl))
        t = t - _dot(_dot(t, jnp.where(off, m, 0.0)), t)
        lvl += 1
    eg = jnp.exp(gc)
    sol = _dot(t, jnp.concatenate([v * beta, kn * (beta * eg)], axis=1))
    u, w = sol[:, :dk], sol[:, dk:]
    attn = qk * dec
    qg = qn * eg
    kg = kn * jnp.exp(gtot - gc)
    egl = jnp.exp(gtot)
    s = s_ref[...]
    vnew, qs = [None] * nsub, [None] * nsub
    for i in (range(nsub) if fwd else reversed(range(nsub))):
        r = slice(i * ch, (i + 1) * ch)
        ws = _dot(jnp.concatenate([w[r], qg[r]], axis=0), s)
        vnew[i] = u[r] - ws[:ch]
        qs[i] = ws[ch:]
        s = s * egl[i * ch:i * ch + 1, :] + _dot_tn(kg[r], vnew[i])
    s_ref[...] = s

    @pl.when(j == nb - 1)
    def _():
        st_ref[...] = s

    o = jnp.concatenate(qs, axis=0) + _dot(attn, jnp.concatenate(vnew, axis=0))
    if fwd:
        o = o + ob_ref[...]
        out_ref[...] = (_rms(o, gn_ref[...]) * _silu(z_ref[...])).astype(BF16)
    else:
        out_ref[...] = o


def _gdn_call(fwd, proj, ob, prm, s0, row0, n_seq, seq_len, heads, width):
    rows = proj.shape[0]
    rb = min(SCAN_CHUNK, seq_len)
    nb = seq_len // rb
    qb = row0 // rb
    nhalo = rows // HALO
    dk = width // heads
    ab_blk = 4 * width // LANES

    def cidx(j):
        return j if fwd else nb - 1 - j

    def rowblk(s, j):
        return qb + s * nb + cidx(j)

    def prev(s, j):
        return jnp.maximum(rowblk(s, j) * (rb // HALO) - 1, 0)

    def nxt(s, j):
        return jnp.minimum((rowblk(s, j) + 1) * (rb // HALO), nhalo - 1)

    in_specs, args = [], []
    for sec in range(3):
        in_specs += [pl.BlockSpec((rb, dk), lambda s, h, j, sec=sec: (rowblk(s, j), sec * heads + h)),
                     pl.BlockSpec((HALO, dk), lambda s, h, j, sec=sec: (prev(s, j), sec * heads + h)),
                     pl.BlockSpec((HALO, dk), lambda s, h, j, sec=sec: (nxt(s, j), sec * heads + h))]
        args += [proj, proj, proj]
    in_specs.append(pl.BlockSpec((rb, LANES), lambda s, h, j: (rowblk(s, j), ab_blk)))
    args.append(proj)
    for sec in range(3):
        in_specs += [pl.BlockSpec((3, dk), lambda s, h, j, sec=sec: (0, sec * heads + h)),
                     pl.BlockSpec((1, dk), lambda s, h, j, sec=sec: (0, sec * heads + h))]
        args += [prm["cw"], prm["cb"]]
    in_specs += [pl.BlockSpec((1, LANES), lambda s, h, j: (0, 0)), pl.BlockSpec((1, LANES), lambda s, h, j: (0, 0)),
                 pl.BlockSpec((None, None, dk, dk), lambda s, h, j: (s, h, 0, 0))]
    args += [prm["alog"], prm["dtb"], s0]
    if fwd:
        in_specs += [pl.BlockSpec((rb, dk), lambda s, h, j: (rowblk(s, j), 3 * heads + h)),
                     pl.BlockSpec((rb, dk), lambda s, h, j: (s * nb + j, h)),
                     pl.BlockSpec((1, dk), lambda s, h, j: (0, 0))]
        args += [proj, ob, prm["gn"]]
    return pl.pallas_call(
        functools.partial(_gdn_kernel, fwd, nb, heads),
        grid=(n_seq, heads, nb),
        in_specs=in_specs,
        out_specs=[pl.BlockSpec((rb, dk), lambda s, h, j: (s * nb + cidx(j), h)),
                   pl.BlockSpec((None, None, dk, dk), lambda s, h, j: (s, h, 0, 0))],
        out_shape=[jax.ShapeDtypeStruct((n_seq * seq_len, width), BF16 if fwd else F32),
                   jax.ShapeDtypeStruct((n_seq, heads, dk, dk), F32)],
        scratch_shapes=[pltpu.VMEM((dk, dk), F32)],
        compiler_params=_params(("arbitrary", "arbitrary", "arbitrary")),
        name="gdn_fwd" if fwd else "gdn_bwd",
    )(*args)


def _pad_cols(w, n):
    return jnp.pad(w, ((0, 0), (0, n - w.shape[1])))


def _lane_row(v):
    v = v.reshape(1, -1).astype(F32)
    return _pad_cols(v, LANES)


def _even_layer(x, mods, layer, e, geom, norm_g, w_in, conv_w, conv_b, dt_bias, a_log, d_skip, ssd_g,
                lam_p, da_g, w_out, cache_k, cache_v, state_ssd, tables):
    d = x.shape[1]
    heads, p_dim, n_state = state_ssd.shape[3], state_ssd.shape[4], state_ssd.shape[5]
    width = heads * p_dim
    da_heads = cache_k.shape[3]
    assert width == d and da_heads * LANES == d and 2 * n_state == LANES and 2 * p_dim == LANES
    bc_w = 2 * LANES
    n_dt = 2 * heads
    o_z, o_xs, o_bc, o_dt = 0, width, 2 * width, 2 * width + bc_w
    o_q = o_dt + n_dt
    cols = [w_in[:, o_xs:o_xs + width], w_in[:, o_z:o_z + width], w_in[:, o_q:o_q + 3 * d],
            w_in[:, o_bc:o_bc + bc_w], w_in[:, o_dt:o_dt + n_dt]]
    n_used = 5 * width + bc_w + n_dt
    n_pad = -(-n_used // IN_COL_TILE) * IN_COL_TILE
    w_perm = _pad_cols(jnp.concatenate(cols, axis=1), n_pad).astype(BF16)
    proj = _inproj_call(x, mods, layer, norm_g, w_perm, geom)

    hl = jnp.arange(LANES)[:, None]
    cl = jnp.arange(width)[None, :]
    prm = {
        "cwx": conv_w[:, :width], "cbx": conv_b[None, :width],
        "cwb": conv_w[:, width:], "cbb": conv_b[None, width:],
        "dtb": _lane_row(dt_bias), "alog": _lane_row(a_log),
        "pf": (hl == cl // p_dim).astype(BF16), "pb": (hl == heads + cl // p_dim).astype(BF16),
        "bd": (hl // n_state == cl // (width // 2)).astype(F32),
        "dx": jnp.repeat(d_skip, p_dim)[None, :], "gn": ssd_g[None, :],
    }

    def state_in(st):
        t = st.transpose(0, 3, 1, 2).reshape(st.shape[0], n_state, width)
        return jnp.concatenate([t, t], axis=1) * prm["bd"]

    def state_out(st):
        per_h = st.reshape(st.shape[0], 2, n_state, heads, p_dim)
        sel = jnp.concatenate([per_h[:, 0, :, :heads // 2], per_h[:, 1, :, heads // 2:]], axis=2)
        return sel.transpose(0, 2, 3, 1)

    ys, os_, new_states = [], [], None
    for part in range(2):
        latent = part == 1
        n_seq, seq_len = (geom.n_lat, geom.s_lat) if latent else (geom.n_ctx, geom.s_ctx)
        row0 = geom.ctx_rows if latent else 0
        if latent:
            h0f, h0b = state_in(state_ssd[:, e, 0].astype(F32)), state_in(state_ssd[:, e, 1].astype(F32))
        else:
            h0f = h0b = jnp.zeros((n_seq, LANES, width), F32)
        yb, st_b = _ssd_call(False, proj, None, prm, h0b, row0, n_seq, seq_len, heads, width)
        y, st_f = _ssd_call(True, proj, yb, prm, h0f, row0, n_seq, seq_len, heads, width)
        ys.append(y)
        if not latent:
            new_states = jnp.stack([state_out(st_f), state_out(st_b)], axis=1)
        lam_init = 0.8 - 0.6 * math.exp(-0.3 * layer)
        ck = cache_k[:, e].reshape(geom.n_lat, -1, d) if latent else None
        cv = cache_v[:, e].reshape(geom.n_lat, -1, d) if latent else None
        os_.append(_attn_call(latent, lam_init, proj, ck, cv, tables, lam_p, da_g, row0, n_seq, seq_len,
                              da_heads, d))
    y_all = jnp.concatenate(ys, axis=0)
    o_all = jnp.concatenate(os_, axis=0)
    w_o = w_out.astype(BF16)
    x = _outproj_call(x, mods, layer, [y_all, o_all], [w_o[:width], w_o[width:]], geom)
    k_new = proj[:geom.ctx_rows, 3 * d:4 * d].reshape(geom.n_ctx, geom.s_ctx, da_heads, LANES)
    v_new = proj[:geom.ctx_rows, 4 * d:5 * d].reshape(geom.n_ctx, geom.s_ctx, da_heads, LANES)
    return x, k_new, v_new, new_states


def _odd_layer(x, mods, layer, o, geom, norm_g, w_in, conv_w, conv_b, a_log, dt_bias, g_norm, w_out, state_gdn):
    d = x.shape[1]
    heads, dk, dv = state_gdn.shape[3], state_gdn.shape[4], state_gdn.shape[5]
    width = heads * dv
    assert width == d and dk == LANES and dv == LANES and 4 * heads <= LANES
    n_used = 4 * width + 4 * heads
    n_pad = -(-n_used // IN_COL_TILE) * IN_COL_TILE
    w_perm = _pad_cols(w_in, n_pad).astype(BF16)
    proj = _inproj_call(x, mods, layer, norm_g, w_perm, geom)

    def ab_row(v):
        r = jnp.zeros((2, 2 * heads), F32).at[:, :heads].set(v.astype(F32))
        return _lane_row(r)

    prm = {"cw": conv_w, "cb": conv_b[None, :], "alog": ab_row(a_log), "dtb": ab_row(dt_bias), "gn": g_norm[None, :]}
    outs, new_states = [], None
    for part in range(2):
        latent = part == 1
        n_seq, seq_len = (geom.n_lat, geom.s_lat) if latent else (geom.n_ctx, geom.s_ctx)
        row0 = geom.ctx_rows if latent else 0
        if latent:
            s0f, s0b = state_gdn[:, o, 0].astype(F32), state_gdn[:, o, 1].astype(F32)
        else:
            s0f = s0b = jnp.zeros((n_seq, heads, dk, dv), F32)
        ob, st_b = _gdn_call(False, proj, None, prm, s0b, row0, n_seq, seq_len, heads, width)
        og, st_f = _gdn_call(True, proj, ob, prm, s0f, row0, n_seq, seq_len, heads, width)
        outs.append(og)
        if not latent:
            new_states = jnp.stack([st_f, st_b], axis=1)
    x = _outproj_call(x, mods, layer, [jnp.concatenate(outs, axis=0)], [w_out.astype(BF16)], geom)
    return x, new_states


def kernel(x_prompt, x_sample, cache_attn_k, cache_attn_v, state_ssd, state_gdn, c, c_ctx, ada_w, ada_b, norm_g, ev_w_in, ev_conv_w, ev_conv_b, ssd_dt_bias, ssd_a_log, ssd_d, ssd_norm_g, da_lambda, da_norm_g, ev_w_out, od_w_in, od_conv_w, od_conv_b, gdn_a_log, gdn_dt_bias, gdn_norm_g, od_w_out, ffn_w_up, ffn_conv_w, ffn_conv_b, ffn_w_down, final_norm_g):
    n_ctx, s_ctx, d = x_prompt.shape
    n_lat, s_lat, _ = x_sample.shape
    depth = ada_w.shape[0]
    assert n_lat + 1 <= MOD_ROWS and s_ctx & (s_ctx - 1) == 0 and s_lat & (s_lat - 1) == 0
    geom = _Geom(n_ctx, s_ctx, n_lat, s_lat)
    dtype = x_prompt.dtype
    x = jnp.concatenate([x_prompt.reshape(-1, d), x_sample.reshape(-1, d)], axis=0).astype(F32)

    cvec = jnp.zeros((MOD_ROWS, d), F32).at[0].set(c_ctx).at[1:1 + n_lat].set(c)
    mods = _mod_call(cvec, ada_w, ada_b)
    mods = mods.reshape(depth, MOD_ROWS, N_MOD, d).transpose(0, 2, 1, 3).reshape(depth * N_MOD * MOD_ROWS, 1, d)

    grid_w = 64
    tables = _rope_tables(s_lat, grid_w)
    ks_out, vs_out, ssd_out, gdn_out = [], [], [], []
    for l in range(depth):
        if l % 2 == 0:
            e = l // 2
            x, k_new, v_new, st = _even_layer(
                x, mods, l, e, geom, norm_g[l, 0], ev_w_in[e], ev_conv_w[e], ev_conv_b[e], ssd_dt_bias[e],
                ssd_a_log[e], ssd_d[e], ssd_norm_g[e], da_lambda[e], da_norm_g[e], ev_w_out[e],
                cache_attn_k, cache_attn_v, state_ssd, tables)
            ks_out.append(k_new)
            vs_out.append(v_new)
            ssd_out.append(st)
        else:
            o = l // 2
            x, st = _odd_layer(x, mods, l, o, geom, norm_g[l, 0], od_w_in[o], od_conv_w[o], od_conv_b[o],
                               gdn_a_log[o], gdn_dt_bias[o], gdn_norm_g[o], od_w_out[o], state_gdn)
            gdn_out.append(st)
        x = _ffn_call(x, mods, l, norm_g[l, 1], ffn_w_up[l].astype(BF16), ffn_conv_w[l], ffn_conv_b[l],
                      ffn_w_down[l].astype(BF16), geom)
    y_prompt = _final_call(x, final_norm_g, 0, geom.ctx_rows).reshape(n_ctx, s_ctx, d)
    y_sample = _final_call(x, final_norm_g, geom.ctx_rows, n_lat * s_lat).reshape(n_lat, s_lat, d)
    return (y_prompt.astype(dtype), y_sample.astype(dtype),
            jnp.stack(ks_out, axis=1).astype(dtype), jnp.stack(vs_out, axis=1).astype(dtype),
            jnp.stack(ssd_out, axis=1).astype(dtype), jnp.stack(gdn_out, axis=1).astype(dtype))
```

```python
import functools
import math

import jax
import jax.numpy as jnp
from jax import lax
from jax.experimental import pallas as pl
from jax.experimental.pallas import tpu as pltpu

F32 = jnp.float32
BF16 = jnp.bfloat16
EPS = 1e-6
ROPE_BASE = 10000.0

LANES = 128
HALO = 8
VMEM_LIMIT_BYTES = 56 * 1024 * 1024

N_MOD = 6
MOD_ROWS = 8
SCAN_CHUNK = 256
GDN_CHUNK = 64
GDN_HEADS_PER_STEP = 4
Q_TILE = 256
ROW_TILE = 1024
OUT_ROW_TILE = 512
FF_TILE = 256
IN_COL_TILE = 512


def _silu(x):
    return x / (1.0 + jnp.exp(-x))


def _softplus(x):
    return jnp.maximum(x, 0.0) + jnp.log1p(jnp.exp(-jnp.abs(x)))


def _dot(a, b):
    return jnp.dot(a.astype(BF16), b.astype(BF16), preferred_element_type=F32)


def _dot_nt(a, b):
    return lax.dot_general(a.astype(BF16), b.astype(BF16), (((1,), (1,)), ((), ())),
                           preferred_element_type=F32)


def _dot_tn(a, b):
    return lax.dot_general(a.astype(BF16), b.astype(BF16), (((0,), (0,)), ((), ())),
                           preferred_element_type=F32)


def _split3(a):
    hi = a.astype(BF16)
    r1 = a - hi.astype(F32)
    mid = r1.astype(BF16)
    return hi, mid, (r1 - mid.astype(F32)).astype(BF16)


def _expand(a, p):
    d = functools.partial(jnp.dot, preferred_element_type=F32)
    hi, mid, lo = _split3(a)
    return d(hi, p) + d(mid, p) + d(lo, p)


def _mask_sum(m, a):
    d = functools.partial(jnp.dot, preferred_element_type=F32)
    hi, mid, lo = _split3(a)
    return d(m, hi) + d(m, mid) + d(m, lo)


def _rms(x, g):
    ms = jnp.mean(x * x, axis=-1, keepdims=True)
    return x * lax.rsqrt(ms + EPS) * g


def _params(sem):
    return pltpu.CompilerParams(dimension_semantics=sem, vmem_limit_bytes=VMEM_LIMIT_BYTES)


def _conv3_silu(x, xp, xn, w, b, first, last):
    rows = x.shape[0]
    row = lax.broadcasted_iota(jnp.int32, x.shape, 0)
    prev_row = jnp.where(first, 0.0, xp[HALO - 1:HALO, :])
    next_row = jnp.where(last, 0.0, xn[0:1, :])
    x_prev = jnp.where(row == 0, prev_row, pltpu.roll(x, 1, 0))
    x_next = jnp.where(row == rows - 1, next_row, pltpu.roll(x, rows - 1, 0))
    y = x_prev * w[0:1, :] + x * w[1:2, :] + x_next * w[2:3, :] + b
    return _silu(y)


def _mod_kernel(c_ref, w_ref, b_ref, o_ref):
    s = _silu(c_ref[...])
    o_ref[...] = _dot(s, w_ref[...]) + b_ref[...]


def _mod_call(cvec, ada_w, ada_b):
    depth, d, n = ada_w.shape
    tn = d
    return pl.pallas_call(
        _mod_kernel,
        grid=(depth, n // tn),
        in_specs=[pl.BlockSpec((MOD_ROWS, d), lambda l, j: (0, 0)),
                  pl.BlockSpec((None, d, tn), lambda l, j: (l, 0, j)),
                  pl.BlockSpec((None, 1, tn), lambda l, j: (l, 0, j))],
        out_specs=pl.BlockSpec((None, MOD_ROWS, tn), lambda l, j: (l, 0, j)),
        out_shape=jax.ShapeDtypeStruct((depth, MOD_ROWS, n), F32),
        compiler_params=_params(("arbitrary", "arbitrary")),
        name="ada_mod",
    )(cvec, ada_w, ada_b.reshape(depth, 1, n))


class _Geom:
    def __init__(self, n_ctx, s_ctx, n_lat, s_lat):
        self.n_ctx, self.s_ctx, self.n_lat, self.s_lat = n_ctx, s_ctx, n_lat, s_lat
        self.ctx_rows = n_ctx * s_ctx
        self.rows = self.ctx_rows + n_lat * s_lat

    def row_tile(self, want):
        t = want
        while self.ctx_rows % t or self.s_lat % t:
            t //= 2
        return t

    def group(self, tile):
        def fn(i):
            r = i * tile
            return jnp.where(r < self.ctx_rows, 0, 1 + jnp.maximum(r - self.ctx_rows, 0) // self.s_lat)
        return fn


def _mod_spec(d, layer, which, group_fn):
    return pl.BlockSpec((None, 1, d), lambda i, *_: ((layer * N_MOD + which) * MOD_ROWS + group_fn(i), 0, 0))


def _inproj_kernel(x_ref, g_ref, sh_ref, sc_ref, w_ref, o_ref, h_ref):
    @pl.when(pl.program_id(1) == 0)
    def _():
        h_ref[...] = (_rms(x_ref[...], g_ref[...]) * (1.0 + sc_ref[...]) + sh_ref[...]).astype(BF16)

    o_ref[...] = jnp.dot(h_ref[...], w_ref[...], preferred_element_type=F32)


def _inproj_call(x, mods, layer, norm_g, w, geom):
    rows, d = x.shape
    n = w.shape[1]
    tm = geom.row_tile(ROW_TILE)
    tn = IN_COL_TILE
    grp = geom.group(tm)
    return pl.pallas_call(
        _inproj_kernel,
        grid=(rows // tm, n // tn),
        in_specs=[pl.BlockSpec((tm, d), lambda i, j: (i, 0)),
                  pl.BlockSpec((1, d), lambda i, j: (0, 0)),
                  _mod_spec(d, layer, 0, grp),
                  _mod_spec(d, layer, 1, grp),
                  pl.BlockSpec((d, tn), lambda i, j: (0, j))],
        out_specs=pl.BlockSpec((tm, tn), lambda i, j: (i, j)),
        out_shape=jax.ShapeDtypeStruct((rows, n), F32),
        scratch_shapes=[pltpu.VMEM((tm, d), BF16)],
        compiler_params=_params(("arbitrary", "arbitrary")),
        name="in_proj",
    )(x, norm_g.reshape(1, d), mods, mods, w)


def _outproj_kernel(n_in, x_ref, gate_ref, *refs):
    a_refs, w_refs, o_ref = refs[:n_in], refs[n_in:2 * n_in], refs[2 * n_in]
    acc = jnp.dot(a_refs[0][...], w_refs[0][...], preferred_element_type=F32)
    for a_ref, w_ref in zip(a_refs[1:], w_refs[1:]):
        acc = acc + jnp.dot(a_ref[...], w_ref[...], preferred_element_type=F32)
    o_ref[...] = x_ref[...] + gate_ref[...] * acc


def _outproj_call(x, mods, layer, acts, ws, geom):
    rows, d = x.shape
    tm = geom.row_tile(OUT_ROW_TILE)
    grp = geom.group(tm)
    n_in = len(acts)
    in_specs = [pl.BlockSpec((tm, d), lambda i: (i, 0)), _mod_spec(d, layer, 2, grp)]
    in_specs += [pl.BlockSpec((tm, a.shape[1]), lambda i: (i, 0)) for a in acts]
    in_specs += [pl.BlockSpec(w.shape, lambda i: (0, 0)) for w in ws]
    return pl.pallas_call(
        functools.partial(_outproj_kernel, n_in),
        grid=(rows // tm,),
        in_specs=in_specs,
        out_specs=pl.BlockSpec((tm, d), lambda i: (i, 0)),
        out_shape=jax.ShapeDtypeStruct((rows, d), F32),
        compiler_params=_params(("arbitrary",)),
        name="out_proj",
    )(x, mods, *acts, *ws)


def _ffn_kernel(geom, x_ref, xp_ref, xn_ref, g_ref, sh_ref, sc_ref, gate_ref,
                wg_ref, wv_ref, cwg_ref, cwv_ref, cbg_ref, cbv_ref, wd_ref, o_ref, h_ref, acc_ref):
    i, j = pl.program_id(0), pl.program_id(1)
    tm = x_ref.shape[0]

    @pl.when(j == 0)
    def _():
        def hn(x):
            return (_rms(x, g_ref[...]) * (1.0 + sc_ref[...]) + sh_ref[...]).astype(BF16)
        h_ref[0:HALO, :] = hn(xp_ref[...])
        h_ref[HALO:HALO + tm, :] = hn(x_ref[...])
        h_ref[HALO + tm:, :] = hn(xn_ref[...])
        acc_ref[...] = jnp.zeros_like(acc_ref)

    h = h_ref[...]
    ug = jnp.dot(h, wg_ref[...], preferred_element_type=F32)
    uv = jnp.dot(h, wv_ref[...], preferred_element_type=F32)
    r0 = i * tm
    is_ctx = r0 < geom.ctx_rows
    base = jnp.where(is_ctx, r0, r0 - geom.ctx_rows)
    lseq = jnp.where(is_ctx, geom.s_ctx, geom.s_lat)
    pos = (base + lax.broadcasted_iota(jnp.int32, (tm, ug.shape[1]), 0)) & (lseq - 1)
    at_start, at_end = pos == 0, pos == lseq - 1

    def conv(u, cw_ref, cb_ref):
        up = jnp.where(at_start, 0.0, pltpu.roll(u, 1, 0)[HALO:HALO + tm])
        un = jnp.where(at_end, 0.0, pltpu.roll(u, tm + 2 * HALO - 1, 0)[HALO:HALO + tm])
        return up * cw_ref[0:1, :] + u[HALO:HALO + tm] * cw_ref[1:2, :] + un * cw_ref[2:3, :] + cb_ref[...]

    a = _silu(conv(ug, cwg_ref, cbg_ref)) * conv(uv, cwv_ref, cbv_ref)
    acc_ref[...] += jnp.dot(a.astype(BF16), wd_ref[...], preferred_element_type=F32)

    @pl.when(j == pl.num_programs(1) - 1)
    def _():
        o_ref[...] = x_ref[...] + gate_ref[...] * acc_ref[...]


def _ffn_call(x, mods, layer, norm_g, w_up, conv_w, conv_b, w_down, geom):
    rows, d = x.shape
    dff = w_down.shape[0]
    tm = geom.row_tile(ROW_TILE)
    tf = FF_TILE
    nf = dff // tf
    grp = geom.group(tm)
    nhalo = rows // HALO
    conv_b = conv_b.reshape(1, 2 * dff)
    return pl.pallas_call(
        functools.partial(_ffn_kernel, geom),
        grid=(rows // tm, nf),
        in_specs=[pl.BlockSpec((tm, d), lambda i, j: (i, 0)),
                  pl.BlockSpec((HALO, d), lambda i, j: (jnp.maximum(i * (tm // HALO) - 1, 0), 0)),
                  pl.BlockSpec((HALO, d), lambda i, j: (jnp.minimum((i + 1) * (tm // HALO), nhalo - 1), 0)),
                  pl.BlockSpec((1, d), lambda i, j: (0, 0)),
                  _mod_spec(d, layer, 3, grp), _mod_spec(d, layer, 4, grp), _mod_spec(d, layer, 5, grp),
                  pl.BlockSpec((d, tf), lambda i, j: (0, j)),
                  pl.BlockSpec((d, tf), lambda i, j: (0, nf + j)),
                  pl.BlockSpec((3, tf), lambda i, j: (0, j)),
                  pl.BlockSpec((3, tf), lambda i, j: (0, nf + j)),
                  pl.BlockSpec((1, tf), lambda i, j: (0, j)),
                  pl.BlockSpec((1, tf), lambda i, j: (0, nf + j)),
                  pl.BlockSpec((tf, d), lambda i, j: (j, 0))],
        out_specs=pl.BlockSpec((tm, d), lambda i, j: (i, 0)),
        out_shape=jax.ShapeDtypeStruct((rows, d), F32),
        scratch_shapes=[pltpu.VMEM((tm + 2 * HALO, d), BF16), pltpu.VMEM((tm, d), F32)],
        compiler_params=_params(("arbitrary", "arbitrary")),
        name="conv_ffn",
    )(x, x, x, norm_g.reshape(1, d), mods, mods, mods, w_up, w_up, conv_w, conv_w, conv_b, conv_b, w_down)


def _final_kernel(x_ref, g_ref, o_ref):
    o_ref[...] = _rms(x_ref[...], g_ref[...])


def _final_call(x, g, row0, nrows):
    d = x.shape[1]
    tm = OUT_ROW_TILE
    while nrows % tm or row0 % tm:
        tm //= 2
    return pl.pallas_call(
        _final_kernel,
        grid=(nrows // tm,),
        in_specs=[pl.BlockSpec((tm, d), lambda i: (row0 // tm + i, 0)), pl.BlockSpec((1, d), lambda i: (0, 0))],
        out_specs=pl.BlockSpec((tm, d), lambda i: (i, 0)),
        out_shape=jax.ShapeDtypeStruct((nrows, d), F32),
        compiler_params=_params(("arbitrary",)),
        name="final_norm",
    )(x, g.reshape(1, d))


def _ssd_kernel(fwd, nc, heads, *refs):
    (xs_ref, xsp_ref, xsn_ref, bc_ref, bcp_ref, bcn_ref, dt_ref, cwx_ref, cbx_ref, cwb_ref, cbb_ref,
     dtb_ref, alog_ref, pe_ref, bd_ref, h0_ref) = refs[:16]
    if fwd:
        z_ref, yb_ref, dx_ref, gn_ref, out_ref, st_ref, s_ref, yacc_ref = refs[16:]
    else:
        out_ref, st_ref, s_ref = refs[16:]
    j = pl.program_id(1)
    c = j if fwd else nc - 1 - j
    first, last = c == 0, c == nc - 1
    q = xs_ref.shape[0]
    half = LANES // 2

    @pl.when(j == 0)
    def _():
        s_ref[...] = h0_ref[...]

    xs = _conv3_silu(xs_ref[...], xsp_ref[...], xsn_ref[...], cwx_ref[...], cbx_ref[...], first, last)
    bc = _conv3_silu(bc_ref[...], bcp_ref[...], bcn_ref[...], cwb_ref[...], cbb_ref[...], first, last)
    bm, cm = bc[:, :LANES], bc[:, LANES:]
    dt = _softplus(dt_ref[...] + dtb_ref[...])
    la = dt * (-jnp.exp(alog_ref[...]))
    row = lax.broadcasted_iota(jnp.int32, (q, q), 0)
    col = lax.broadcasted_iota(jnp.int32, (q, q), 1)
    tril, triu = row >= col, row <= col
    cs_p = _mask_sum(tril.astype(F32).astype(BF16), la)
    cs_s = _mask_sum(triu.astype(F32).astype(BF16), la)
    cs = cs_p if fwd else cs_s
    edge = q - 1 if fwd else 0
    ecs_x = _expand(jnp.exp(cs), pe_ref[...])
    wd_x = _expand(dt * jnp.exp(cs[edge:edge + 1, :] - cs), pe_ref[...])
    s_old = s_ref[...]
    y_off = _dot(cm, s_old) * ecs_x
    s_new = (s_old * ecs_x[edge:edge + 1, :] + _dot_tn(bm, xs * wd_x)) * bd_ref[...]
    s_ref[...] = s_new

    @pl.when(j == nc - 1)
    def _():
        st_ref[...] = s_new

    if not fwd:
        out_ref[...] = y_off
        return

    cst_p, cst_s, dtt = cs_p.T, cs_s.T, dt.T
    lane = lax.broadcasted_iota(jnp.int32, (q, LANES), 1)
    lo, hi = lane < half, lane >= half
    gmat = (_dot_nt(jnp.where(lo, cm, 0.0), bm), _dot_nt(jnp.where(hi, cm, 0.0), bm))
    for p in range(heads // 2):
        xp = xs[:, p * LANES:(p + 1) * LANES]
        acc = None
        for hh in range(2):
            h = 2 * p + hh
            hb = heads + h
            lf = jnp.where(tril, jnp.exp(jnp.minimum(cs_p[:, h:h + 1] - cst_p[h:h + 1, :], 0.0)), 0.0)
            lb = jnp.where(triu, jnp.exp(jnp.minimum(cs_s[:, hb:hb + 1] - cst_s[hb:hb + 1, :], 0.0)), 0.0)
            w = gmat[h // (heads // 2)] * (lf * dtt[h:h + 1, :] + lb * dtt[hb:hb + 1, :])
            part = _dot(w, jnp.where(lo if hh == 0 else hi, xp, 0.0))
            acc = part if acc is None else acc + part
        yacc_ref[:, p * LANES:(p + 1) * LANES] = acc
    y = yacc_ref[...] + y_off + yb_ref[...] + dx_ref[...] * xs
    yz = y * _silu(z_ref[...])
    out_ref[...] = _rms(yz, gn_ref[...]).astype(BF16)


def _ssd_call(fwd, proj, yb, prm, h0, row0, n_seq, seq_len, heads, width):
    rows = proj.shape[0]
    q = min(SCAN_CHUNK, seq_len)
    nc = seq_len // q
    qb = row0 // q
    nhalo = rows // HALO
    bc_w = 2 * LANES
    xs_blk, z_blk, bc_blk, dt_blk = 0, 1, 5 * width // bc_w, (5 * width + bc_w) // LANES

    def cidx(j):
        return j if fwd else nc - 1 - j

    def rowblk(s, j):
        return qb + s * nc + cidx(j)

    def prev(s, j):
        return jnp.maximum(rowblk(s, j) * (q // HALO) - 1, 0)

    def nxt(s, j):
        return jnp.minimum((rowblk(s, j) + 1) * (q // HALO), nhalo - 1)

    const = lambda s, j: (0, 0)
    in_specs = [pl.BlockSpec((q, width), lambda s, j: (rowblk(s, j), xs_blk)),
                pl.BlockSpec((HALO, width), lambda s, j: (prev(s, j), xs_blk)),
                pl.BlockSpec((HALO, width), lambda s, j: (nxt(s, j), xs_blk)),
                pl.BlockSpec((q, bc_w), lambda s, j: (rowblk(s, j), bc_blk)),
                pl.BlockSpec((HALO, bc_w), lambda s, j: (prev(s, j), bc_blk)),
                pl.BlockSpec((HALO, bc_w), lambda s, j: (nxt(s, j), bc_blk)),
                pl.BlockSpec((q, LANES), lambda s, j: (rowblk(s, j), dt_blk)),
                pl.BlockSpec((3, width), const), pl.BlockSpec((1, width), const),
                pl.BlockSpec((3, bc_w), const), pl.BlockSpec((1, bc_w), const),
                pl.BlockSpec((1, LANES), const), pl.BlockSpec((1, LANES), const),
                pl.BlockSpec((LANES, width), const), pl.BlockSpec((LANES, width), const),
                pl.BlockSpec((None, LANES, width), lambda s, j: (s, 0, 0))]
    args = [proj, proj, proj, proj, proj, proj, proj, prm["cwx"], prm["cbx"], prm["cwb"], prm["cbb"],
            prm["dtb"], prm["alog"], prm["pf"] if fwd else prm["pb"], prm["bd"], h0]
    scratch = [pltpu.VMEM((LANES, width), F32)]
    if fwd:
        in_specs += [pl.BlockSpec((q, width), lambda s, j: (rowblk(s, j), z_blk)),
                     pl.BlockSpec((q, width), lambda s, j: (s * nc + j, 0)),
                     pl.BlockSpec((1, width), const), pl.BlockSpec((1, width), const)]
        args += [proj, yb, prm["dx"], prm["gn"]]
        scratch.append(pltpu.VMEM((q, width), F32))
        out_dtype = BF16
    else:
        out_dtype = F32
    return pl.pallas_call(
        functools.partial(_ssd_kernel, fwd, nc, heads),
        grid=(n_seq, nc),
        in_specs=in_specs,
        out_specs=[pl.BlockSpec((q, width), lambda s, j: (s * nc + cidx(j), 0)),
                   pl.BlockSpec((None, LANES, width), lambda s, j: (s, 0, 0))],
        out_shape=[jax.ShapeDtypeStruct((n_seq * seq_len, width), out_dtype),
                   jax.ShapeDtypeStruct((n_seq, LANES, width), F32)],
        scratch_shapes=scratch,
        compiler_params=_params(("arbitrary", "arbitrary")),
        name="ssd_fwd" if fwd else "ssd_bwd",
    )(*args)


def _rope(x, cos, sin_signed):
    lane = lax.broadcasted_iota(jnp.int32, x.shape, 1)
    partner = jnp.where((lane & 31) < 16, pltpu.roll(x, LANES - 16, 1), pltpu.roll(x, 16, 1))
    return x * cos + partner * sin_signed


def _attn_kernel(latent, lam_init, *refs):
    if latent:
        (q_ref, k_ref, v_ref, kc_ref, vc_ref, cq_ref, sq_ref, ck_ref, sk_ref, lp_ref, g_ref,
         o_ref, kr_ref, vr_ref) = refs
    else:
        q_ref, k_ref, v_ref, lp_ref, g_ref, o_ref, kr_ref, vr_ref = refs

    @pl.when(pl.program_id(2) == 0)
    def _():
        k = k_ref[...]
        if latent:
            k = _rope(k, ck_ref[...], sk_ref[...])
        kr_ref[...] = k.astype(BF16)
        vr_ref[...] = v_ref[...].astype(BF16)

    q = q_ref[...]
    if latent:
        q = _rope(q, cq_ref[...], sq_ref[...])
    q = q * (float(LANES // 2) ** -0.5)
    lane = lax.broadcasted_iota(jnp.int32, q.shape, 1)
    qs = (jnp.where(lane < LANES // 2, q, 0.0).astype(BF16), jnp.where(lane >= LANES // 2, q, 0.0).astype(BF16))
    keys = [kr_ref[...]] + ([kc_ref[...].astype(BF16)] if latent else [])
    vals = [vr_ref[...]] + ([vc_ref[...].astype(BF16)] if latent else [])
    lp = lp_ref[...]
    lam = (jnp.exp(jnp.sum(lp[0:1] * lp[1:2], axis=1, keepdims=True))
           - jnp.exp(jnp.sum(lp[2:3] * lp[3:4], axis=1, keepdims=True)) + lam_init)
    probs, coef = [], []
    for m in range(2):
        s = [lax.dot_general(qs[m], kk, (((1,), (1,)), ((), ())), preferred_element_type=F32) for kk in keys]
        mx = functools.reduce(jnp.maximum, [jnp.max(t, axis=1, keepdims=True) for t in s])
        p = [jnp.exp(t - mx) for t in s]
        den = functools.reduce(jnp.add, [jnp.sum(t, axis=1, keepdims=True) for t in p])
        probs.append(p)
        coef.append(1.0 / den if m == 0 else lam / den)
    o = None
    for seg in range(len(keys)):
        pd = (probs[0][seg] * coef[0] - probs[1][seg] * coef[1]).astype(BF16)
        part = jnp.dot(pd, vals[seg], preferred_element_type=F32)
        o = part if o is None else o + part
    o_ref[...] = (_rms(o, g_ref[...]) * (1.0 - lam_init)).astype(BF16)


def _attn_call(latent, lam_init, proj, cache_k, cache_v, tables, lam_p, norm_g, row0, n_seq, seq_len, heads, width):
    tq = min(Q_TILE, seq_len)
    nq = seq_len // tq
    hpw = width // LANES
    q_blk, k_blk, v_blk = 2 * hpw, 3 * hpw, 4 * hpw
    qb, sb = row0 // tq, row0 // seq_len
    in_specs = [pl.BlockSpec((tq, LANES), lambda b, h, i: (qb + b * nq + i, q_blk + h)),
                pl.BlockSpec((seq_len, LANES), lambda b, h, i: (sb + b, k_blk + h)),
                pl.BlockSpec((seq_len, LANES), lambda b, h, i: (sb + b, v_blk + h))]
    args = [proj, proj, proj]
    if latent:
        past = cache_k.shape[1]
        in_specs += [pl.BlockSpec((None, past, LANES), lambda b, h, i: (b, 0, h)),
                     pl.BlockSpec((None, past, LANES), lambda b, h, i: (b, 0, h)),
                     pl.BlockSpec((tq, LANES), lambda b, h, i: (i, 0)),
                     pl.BlockSpec((tq, LANES), lambda b, h, i: (i, 0)),
                     pl.BlockSpec((seq_len, LANES), lambda b, h, i: (0, 0)),
                     pl.BlockSpec((seq_len, LANES), lambda b, h, i: (0, 0))]
        args += [cache_k, cache_v, tables[0], tables[1], tables[0], tables[1]]
    in_specs += [pl.BlockSpec(lam_p.shape, lambda b, h, i: (0, 0)), pl.BlockSpec((1, LANES), lambda b, h, i: (0, 0))]
    args += [lam_p, norm_g.reshape(1, LANES)]
    return pl.pallas_call(
        functools.partial(_attn_kernel, latent, lam_init),
        grid=(n_seq, heads, nq),
        in_specs=in_specs,
        out_specs=pl.BlockSpec((tq, LANES), lambda b, h, i: (b * nq + i, h)),
        out_shape=jax.ShapeDtypeStruct((n_seq * seq_len, width), BF16),
        scratch_shapes=[pltpu.VMEM((seq_len, LANES), BF16), pltpu.VMEM((seq_len, LANES), BF16)],
        compiler_params=_params(("arbitrary", "arbitrary", "arbitrary")),
        name="diff_attn_lat" if latent else "diff_attn_ctx",
    )(*args)


def _rope_tables(n_tokens, grid_w):
    n_freq = LANES // 8
    pos = jnp.arange(n_tokens)
    r = (pos // grid_w).astype(F32)
    cpos = (pos % grid_w).astype(F32)
    inv = ROPE_BASE ** (-jnp.arange(n_freq, dtype=F32) / n_freq)
    ang_r, ang_c = r[:, None] * inv, cpos[:, None] * inv
    cos32 = lambda a: jnp.concatenate([jnp.cos(a), jnp.cos(a)], axis=1)
    sin32 = lambda a: jnp.concatenate([-jnp.sin(a), jnp.sin(a)], axis=1)
    cos64 = jnp.concatenate([cos32(ang_r), cos32(ang_c)], axis=1)
    sin64 = jnp.concatenate([sin32(ang_r), sin32(ang_c)], axis=1)
    return jnp.concatenate([cos64, cos64], axis=1), jnp.concatenate([sin64, sin64], axis=1)


def _gdn_masks(rb, fwd):
    row = jnp.arange(rb)[:, None]
    col = jnp.arange(rb)[None, :]
    shift = GDN_CHUNK.bit_length() - 1
    blk = (row >> shift) == (col >> shift)
    incl = blk & ((row >= col) if fwd else (row <= col))
    strict = blk & ((row > col) if fwd else (row < col))
    ms = [incl, strict, row == col, (row >> 1) == (col >> 1)]
    for lvl in range(1, shift):
        ms.append(((row >> (lvl + 1)) == (col >> (lvl + 1))) & ((row >> lvl) != (col >> lvl)))
    return jnp.stack(ms).astype(F32), jnp.stack([incl, blk]).astype(BF16)


def _gdn_kernel(fwd, nb, heads, hps, *refs):
    (q_ref, qp_ref, qn_ref, k_ref, kp_ref, kn_ref, v_ref, vp_ref, vn_ref, ab_ref,
     cwq_ref, cbq_ref, cwk_ref, cbk_ref, cwv_ref, cbv_ref, alog_ref, dtb_ref, mf_ref, mb_ref, s0_ref) = refs[:21]
    if fwd:
        z_ref, ob_ref, gn_ref, out_ref, st_ref, s_ref = refs[21:]
    else:
        out_ref, st_ref, s_ref = refs[21:]
    hg, j = pl.program_id(1), pl.program_id(2)
    c = j if fwd else nb - 1 - j
    first, last = c == 0, c == nb - 1
    rb = q_ref.shape[0]
    dk = q_ref.shape[1] // hps
    ch = GDN_CHUNK
    nsub = rb // ch
    n_lvl = ch.bit_length() - 1

    @pl.when(j == 0)
    def _():
        s_ref[...] = s0_ref[...]

    ab = ab_ref[...]
    d0 = 0 if fwd else 2 * heads
    g_all = -jnp.exp(alog_ref[...]) * _softplus(ab + dtb_ref[...])
    b_all = 1.0 / (1.0 + jnp.exp(-ab))
    gc_all = _mask_sum(mb_ref[0], g_all)
    gt_all = _mask_sum(mb_ref[1], g_all)
    gct_all = gc_all.T
    lane = lax.broadcasted_iota(jnp.int32, ab.shape, 1)
    sub = lax.broadcasted_iota(jnp.int32, gct_all.shape, 0)
    incl_f, strict_f, eye, pair = mf_ref[0], mf_ref[1], mf_ref[2], mf_ref[3]

    def pick(a, l):
        return jnp.sum(jnp.where(lane == l, a, 0.0), axis=1, keepdims=True)

    hs = range(hps)
    sls = [slice(hh * dk, (hh + 1) * dk) for hh in hs]
    kn, qn, v, gc, gtot, beta, dec, m, t = ([None] * hps for _ in range(9))
    for hh in hs:
        hd = hg * hps + hh
        sl = sls[hh]

        def conv(x_ref, xp_ref, xn_ref, cw_ref, cb_ref):
            return _conv3_silu(x_ref[:, sl], xp_ref[:, sl], xn_ref[:, sl], cw_ref[:, sl], cb_ref[:, sl], first, last)

        qc = conv(q_ref, qp_ref, qn_ref, cwq_ref, cbq_ref)
        kc = conv(k_ref, kp_ref, kn_ref, cwk_ref, cbk_ref)
        v[hh] = conv(v_ref, vp_ref, vn_ref, cwv_ref, cbv_ref)
        qn[hh] = qc * lax.rsqrt(jnp.sum(qc * qc, axis=1, keepdims=True) + EPS) * (float(dk) ** -0.5)
        kn[hh] = kc * lax.rsqrt(jnp.sum(kc * kc, axis=1, keepdims=True) + EPS)
        gc[hh], gtot[hh] = pick(gc_all, d0 + hd), pick(gt_all, d0 + hd)
        beta[hh] = pick(b_all, d0 + heads + hd)
        g_row = jnp.sum(jnp.where(sub == d0 + hd, gct_all, 0.0), axis=0, keepdims=True)
        dec[hh] = jnp.exp(jnp.minimum(gc[hh] - g_row, 0.0)) * incl_f
    kk = [_dot_nt(kn[hh], kn[hh]) for hh in hs]
    for hh in hs:
        m[hh] = (kk[hh] * beta[hh]) * (dec[hh] * strict_f)
        t[hh] = eye - m[hh] * pair
    for lvl in range(1, n_lvl):
        off = mf_ref[3 + lvl]
        tm = [_dot(t[hh], m[hh] * off) for hh in hs]
        tmt = [_dot(tm[hh], t[hh]) for hh in hs]
        t = [t[hh] - tmt[hh] for hh in hs]
    eg = [jnp.exp(gc[hh]) for hh in hs]
    sol = [_dot(t[hh], jnp.concatenate([v[hh] * beta[hh], kn[hh] * (beta[hh] * eg[hh])], axis=1)) for hh in hs]
    qk = [_dot_nt(qn[hh], kn[hh]) for hh in hs]
    qg = [qn[hh] * eg[hh] for hh in hs]
    kg = [kn[hh] * jnp.exp(gtot[hh] - gc[hh]) for hh in hs]
    egl = [jnp.exp(gtot[hh]) for hh in hs]
    s = [s_ref[hh] for hh in hs]
    vnew = [[None] * nsub for _ in hs]
    qs = [[None] * nsub for _ in hs]
    for i in (range(nsub) if fwd else reversed(range(nsub))):
        r = slice(i * ch, (i + 1) * ch)
        ws = [_dot(jnp.concatenate([sol[hh][r, dk:], qg[hh][r]], axis=0), s[hh]) for hh in hs]
        for hh in hs:
            vnew[hh][i] = sol[hh][r, :dk] - ws[hh][:ch]
            qs[hh][i] = ws[hh][ch:]
        upd = [_dot_tn(kg[hh][r], vnew[hh][i]) for hh in hs]
        s = [s[hh] * egl[hh][i * ch:i * ch + 1, :] + upd[hh] for hh in hs]
    intra = [_dot(qk[hh] * dec[hh], jnp.concatenate(vnew[hh], axis=0)) for hh in hs]
    for hh in hs:
        sl = sls[hh]
        s_ref[hh] = s[hh]
        o = jnp.concatenate(qs[hh], axis=0) + intra[hh]
        if fwd:
            o = o + ob_ref[:, sl]
            out_ref[:, sl] = (_rms(o, gn_ref[...]) * _silu(z_ref[:, sl])).astype(BF16)
        else:
            out_ref[:, sl] = o

    @pl.when(j == nb - 1)
    def _():
        st_ref[...] = s_ref[...]


def _gdn_call(fwd, proj, ob, prm, s0, row0, n_seq, seq_len, heads, width):
    rows = proj.shape[0]
    rb = min(SCAN_CHUNK, seq_len)
    nb = seq_len // rb
    qb = row0 // rb
    nhalo = rows // HALO
    dk = width // heads
    hps = GDN_HEADS_PER_STEP
    hw = hps * dk
    ngrp = heads // hps
    ab_blk = 4 * width // LANES
    mf, mb = _gdn_masks(rb, fwd)

    def cidx(j):
        return j if fwd else nb - 1 - j

    def rowblk(s, j):
        return qb + s * nb + cidx(j)

    def prev(s, j):
        return jnp.maximum(rowblk(s, j) * (rb // HALO) - 1, 0)

    def nxt(s, j):
        return jnp.minimum((rowblk(s, j) + 1) * (rb // HALO), nhalo - 1)

    in_specs, args = [], []
    for sec in range(3):
        in_specs += [pl.BlockSpec((rb, hw), lambda s, h, j, sec=sec: (rowblk(s, j), sec * ngrp + h)),
                     pl.BlockSpec((HALO, hw), lambda s, h, j, sec=sec: (prev(s, j), sec * ngrp + h)),
                     pl.BlockSpec((HALO, hw), lambda s, h, j, sec=sec: (nxt(s, j), sec * ngrp + h))]
        args += [proj, proj, proj]
    in_specs.append(pl.BlockSpec((rb, LANES), lambda s, h, j: (rowblk(s, j), ab_blk)))
    args.append(proj)
    for sec in range(3):
        in_specs += [pl.BlockSpec((3, hw), lambda s, h, j, sec=sec: (0, sec * ngrp + h)),
                     pl.BlockSpec((1, hw), lambda s, h, j, sec=sec: (0, sec * ngrp + h))]
        args += [prm["cw"], prm["cb"]]
    in_specs += [pl.BlockSpec((1, LANES), lambda s, h, j: (0, 0)), pl.BlockSpec((1, LANES), lambda s, h, j: (0, 0)),
                 pl.BlockSpec(mf.shape, lambda s, h, j: (0, 0, 0)), pl.BlockSpec(mb.shape, lambda s, h, j: (0, 0, 0)),
                 pl.BlockSpec((None, hps, dk, dk), lambda s, h, j: (s, h, 0, 0))]
    args += [prm["alog"], prm["dtb"], mf, mb, s0]
    if fwd:
        in_specs += [pl.BlockSpec((rb, hw), lambda s, h, j: (rowblk(s, j), 3 * ngrp + h)),
                     pl.BlockSpec((rb, hw), lambda s, h, j: (s * nb + j, h)),
                     pl.BlockSpec((1, dk), lambda s, h, j: (0, 0))]
        args += [proj, ob, prm["gn"]]
    return pl.pallas_call(
        functools.partial(_gdn_kernel, fwd, nb, heads, hps),
        grid=(n_seq, ngrp, nb),
        in_specs=in_specs,
        out_specs=[pl.BlockSpec((rb, hw), lambda s, h, j: (s * nb + cidx(j), h)),
                   pl.BlockSpec((None, hps, dk, dk), lambda s, h, j: (s, h, 0, 0))],
        out_shape=[jax.ShapeDtypeStruct((n_seq * seq_len, width), BF16 if fwd else F32),
                   jax.ShapeDtypeStruct((n_seq, heads, dk, dk), F32)],
        scratch_shapes=[pltpu.VMEM((hps, dk, dk), F32)],
        compiler_params=_params(("arbitrary", "arbitrary", "arbitrary")),
        name="gdn_fwd" if fwd else "gdn_bwd",
    )(*args)


def _pad_cols(w, n):
    return jnp.pad(w, ((0, 0), (0, n - w.shape[1])))


def _lane_row(v):
    v = v.reshape(1, -1).astype(F32)
    return _pad_cols(v, LANES)


def _even_layer(x, mods, layer, e, geom, norm_g, w_in, conv_w, conv_b, dt_bias, a_log, d_skip, ssd_g,
                lam_p, da_g, w_out, cache_k, cache_v, state_ssd, tables):
    d = x.shape[1]
    heads, p_dim, n_state = state_ssd.shape[3], state_ssd.shape[4], state_ssd.shape[5]
    width = heads * p_dim
    da_heads = cache_k.shape[3]
    assert width == d and da_heads * LANES == d and 2 * n_state == LANES and 2 * p_dim == LANES
    bc_w = 2 * LANES
    n_dt = 2 * heads
    o_z, o_xs, o_bc, o_dt = 0, width, 2 * width, 2 * width + bc_w
    o_q = o_dt + n_dt
    cols = [w_in[:, o_xs:o_xs + width], w_in[:, o_z:o_z + width], w_in[:, o_q:o_q + 3 * d],
            w_in[:, o_bc:o_bc + bc_w], w_in[:, o_dt:o_dt + n_dt]]
    n_used = 5 * width + bc_w + n_dt
    n_pad = -(-n_used // IN_COL_TILE) * IN_COL_TILE
    w_perm = _pad_cols(jnp.concatenate(cols, axis=1), n_pad).astype(BF16)
    proj = _inproj_call(x, mods, layer, norm_g, w_perm, geom)

    hl = jnp.arange(LANES)[:, None]
    cl = jnp.arange(width)[None, :]
    prm = {
        "cwx": conv_w[:, :width], "cbx": conv_b[None, :width],
        "cwb": conv_w[:, width:], "cbb": conv_b[None, width:],
        "dtb": _lane_row(dt_bias), "alog": _lane_row(a_log),
        "pf": (hl == cl // p_dim).astype(BF16), "pb": (hl == heads + cl // p_dim).astype(BF16),
        "bd": (hl // n_state == cl // (width // 2)).astype(F32),
        "dx": jnp.repeat(d_skip, p_dim)[None, :], "gn": ssd_g[None, :],
    }

    def state_in(st):
        t = st.transpose(0, 3, 1, 2).reshape(st.shape[0], n_state, width)
        return jnp.concatenate([t, t], axis=1) * prm["bd"]

    def state_out(st):
        per_h = st.reshape(st.shape[0], 2, n_state, heads, p_dim)
        sel = jnp.concatenate([per_h[:, 0, :, :heads // 2], per_h[:, 1, :, heads // 2:]], axis=2)
        return sel.transpose(0, 2, 3, 1)

    ys, os_, new_states = [], [], None
    for part in range(2):
        latent = part == 1
        n_seq, seq_len = (geom.n_lat, geom.s_lat) if latent else (geom.n_ctx, geom.s_ctx)
        row0 = geom.ctx_rows if latent else 0
        if latent:
            h0f, h0b = state_in(state_ssd[:, e, 0].astype(F32)), state_in(state_ssd[:, e, 1].astype(F32))
        else:
            h0f = h0b = jnp.zeros((n_seq, LANES, width), F32)
        yb, st_b = _ssd_call(False, proj, None, prm, h0b, row0, n_seq, seq_len, heads, width)
        y, st_f = _ssd_call(True, proj, yb, prm, h0f, row0, n_seq, seq_len, heads, width)
        ys.append(y)
        if not latent:
            new_states = jnp.stack([state_out(st_f), state_out(st_b)], axis=1)
        lam_init = 0.8 - 0.6 * math.exp(-0.3 * layer)
        ck = cache_k[:, e].reshape(geom.n_lat, -1, d) if latent else None
        cv = cache_v[:, e].reshape(geom.n_lat, -1, d) if latent else None
        os_.append(_attn_call(latent, lam_init, proj, ck, cv, tables, lam_p, da_g, row0, n_seq, seq_len,
                              da_heads, d))
    y_all = jnp.concatenate(ys, axis=0)
    o_all = jnp.concatenate(os_, axis=0)
    w_o = w_out.astype(BF16)
    x = _outproj_call(x, mods, layer, [y_all, o_all], [w_o[:width], w_o[width:]], geom)
    k_new = proj[:geom.ctx_rows, 3 * d:4 * d].reshape(geom.n_ctx, geom.s_ctx, da_heads, LANES)
    v_new = proj[:geom.ctx_rows, 4 * d:5 * d].reshape(geom.n_ctx, geom.s_ctx, da_heads, LANES)
    return x, k_new, v_new, new_states


def _odd_layer(x, mods, layer, o, geom, norm_g, w_in, conv_w, conv_b, a_log, dt_bias, g_norm, w_out, state_gdn):
    d = x.shape[1]
    heads, dk, dv = state_gdn.shape[3], state_gdn.shape[4], state_gdn.shape[5]
    width = heads * dv
    assert width == d and dk == LANES and dv == LANES and 4 * heads <= LANES
    n_used = 4 * width + 4 * heads
    n_pad = -(-n_used // IN_COL_TILE) * IN_COL_TILE
    w_perm = _pad_cols(w_in, n_pad).astype(BF16)
    proj = _inproj_call(x, mods, layer, norm_g, w_perm, geom)

    def ab_row(v):
        r = jnp.zeros((2, 2 * heads), F32).at[:, :heads].set(v.astype(F32))
        return _lane_row(r)

    prm = {"cw": conv_w, "cb": conv_b[None, :], "alog": ab_row(a_log), "dtb": ab_row(dt_bias), "gn": g_norm[None, :]}
    outs, new_states = [], None
    for part in range(2):
        latent = part == 1
        n_seq, seq_len = (geom.n_lat, geom.s_lat) if latent else (geom.n_ctx, geom.s_ctx)
        row0 = geom.ctx_rows if latent else 0
        if latent:
            s0f, s0b = state_gdn[:, o, 0].astype(F32), state_gdn[:, o, 1].astype(F32)
        else:
            s0f = s0b = jnp.zeros((n_seq, heads, dk, dv), F32)
        ob, st_b = _gdn_call(False, proj, None, prm, s0b, row0, n_seq, seq_len, heads, width)
        og, st_f = _gdn_call(True, proj, ob, prm, s0f, row0, n_seq, seq_len, heads, width)
        outs.append(og)
        if not latent:
            new_states = jnp.stack([st_f, st_b], axis=1)
    x = _outproj_call(x, mods, layer, [jnp.concatenate(outs, axis=0)], [w_out.astype(BF16)], geom)
    return x, new_states


def kernel(x_prompt, x_sample, cache_attn_k, cache_attn_v, state_ssd, state_gdn, c, c_ctx, ada_w, ada_b, norm_g, ev_w_in, ev_conv_w, ev_conv_b, ssd_dt_bias, ssd_a_log, ssd_d, ssd_norm_g, da_lambda, da_norm_g, ev_w_out, od_w_in, od_conv_w, od_conv_b, gdn_a_log, gdn_dt_bias, gdn_norm_g, od_w_out, ffn_w_up, ffn_conv_w, ffn_conv_b, ffn_w_down, final_norm_g):
    n_ctx, s_ctx, d = x_prompt.shape
    n_lat, s_lat, _ = x_sample.shape
    depth = ada_w.shape[0]
    assert n_lat + 1 <= MOD_ROWS and s_ctx & (s_ctx - 1) == 0 and s_lat & (s_lat - 1) == 0
    geom = _Geom(n_ctx, s_ctx, n_lat, s_lat)
    dtype = x_prompt.dtype
    x = jnp.concatenate([x_prompt.reshape(-1, d), x_sample.reshape(-1, d)], axis=0).astype(F32)

    cvec = jnp.zeros((MOD_ROWS, d), F32).at[0].set(c_ctx).at[1:1 + n_lat].set(c)
    mods = _mod_call(cvec, ada_w, ada_b)
    mods = mods.reshape(depth, MOD_ROWS, N_MOD, d).transpose(0, 2, 1, 3).reshape(depth * N_MOD * MOD_ROWS, 1, d)

    grid_w = 64
    tables = _rope_tables(s_lat, grid_w)
    ks_out, vs_out, ssd_out, gdn_out = [], [], [], []
    for l in range(depth):
        if l % 2 == 0:
            e = l // 2
            x, k_new, v_new, st = _even_layer(
                x, mods, l, e, geom, norm_g[l, 0], ev_w_in[e], ev_conv_w[e], ev_conv_b[e], ssd_dt_bias[e],
                ssd_a_log[e], ssd_d[e], ssd_norm_g[e], da_lambda[e], da_norm_g[e], ev_w_out[e],
                cache_attn_k, cache_attn_v, state_ssd, tables)
            ks_out.append(k_new)
            vs_out.append(v_new)
            ssd_out.append(st)
        else:
            o = l // 2
            x, st = _odd_layer(x, mods, l, o, geom, norm_g[l, 0], od_w_in[o], od_conv_w[o], od_conv_b[o],
                               gdn_a_log[o], gdn_dt_bias[o], gdn_norm_g[o], od_w_out[o], state_gdn)
            gdn_out.append(st)
        x = _ffn_call(x, mods, l, norm_g[l, 1], ffn_w_up[l].astype(BF16), ffn_conv_w[l], ffn_conv_b[l],
                      ffn_w_down[l].astype(BF16), geom)
    y_prompt = _final_call(x, final_norm_g, 0, geom.ctx_rows).reshape(n_ctx, s_ctx, d)
    y_sample = _final_call(x, final_norm_g, geom.ctx_rows, n_lat * s_lat).reshape(n_lat, s_lat, d)
    return (y_prompt.astype(dtype), y_sample.astype(dtype),
            jnp.stack(ks_out, axis=1).astype(dtype), jnp.stack(vs_out, axis=1).astype(dtype),
            jnp.stack(ssd_out, axis=1).astype(dtype), jnp.stack(gdn_out, axis=1).astype(dtype))
```

```python
import functools
import math

import jax
import jax.numpy as jnp
from jax import lax
from jax.experimental import pallas as pl
from jax.experimental.pallas import tpu as pltpu

F32 = jnp.float32
BF16 = jnp.bfloat16
EPS = 1e-6
ROPE_BASE = 10000.0

LANES = 128
HALO = 8
VMEM_LIMIT_BYTES = 56 * 1024 * 1024

N_MOD = 6
MOD_ROWS = 8
SCAN_CHUNK = 256
GDN_CHUNK = 64
GDN_HEADS_PER_STEP = 4
Q_TILE = 256
ROW_TILE = 1024
OUT_ROW_TILE = 512
FF_TILE = 256
FFN_ROW_CHUNK = 256
IN_COL_TILE = 512


def _silu(x):
    return x / (1.0 + jnp.exp(-x))


def _softplus(x):
    return jnp.maximum(x, 0.0) + jnp.log1p(jnp.exp(-jnp.abs(x)))


def _dot(a, b):
    return jnp.dot(a.astype(BF16), b.astype(BF16), preferred_element_type=F32)


def _dot_nt(a, b):
    return lax.dot_general(a.astype(BF16), b.astype(BF16), (((1,), (1,)), ((), ())),
                           preferred_element_type=F32)


def _dot_tn(a, b):
    return lax.dot_general(a.astype(BF16), b.astype(BF16), (((0,), (0,)), ((), ())),
                           preferred_element_type=F32)


def _split3(a):
    hi = a.astype(BF16)
    r1 = a - hi.astype(F32)
    mid = r1.astype(BF16)
    return hi, mid, (r1 - mid.astype(F32)).astype(BF16)


def _expand(a, p):
    d = functools.partial(jnp.dot, preferred_element_type=F32)
    hi, mid, lo = _split3(a)
    return d(hi, p) + d(mid, p) + d(lo, p)


def _mask_sum(m, a):
    d = functools.partial(jnp.dot, preferred_element_type=F32)
    hi, mid, lo = _split3(a)
    return d(m, hi) + d(m, mid) + d(m, lo)


def _rms(x, g):
    ms = jnp.mean(x * x, axis=-1, keepdims=True)
    return x * lax.rsqrt(ms + EPS) * g


def _params(sem):
    return pltpu.CompilerParams(dimension_semantics=sem, vmem_limit_bytes=VMEM_LIMIT_BYTES)


def _conv3_silu(x, xp, xn, w, b, first, last):
    rows = x.shape[0]
    row = lax.broadcasted_iota(jnp.int32, x.shape, 0)
    prev_row = jnp.where(first, 0.0, xp[HALO - 1:HALO, :])
    next_row = jnp.where(last, 0.0, xn[0:1, :])
    x_prev = jnp.where(row == 0, prev_row, pltpu.roll(x, 1, 0))
    x_next = jnp.where(row == rows - 1, next_row, pltpu.roll(x, rows - 1, 0))
    y = x_prev * w[0:1, :] + x * w[1:2, :] + x_next * w[2:3, :] + b
    return _silu(y)


def _mod_kernel(c_ref, w_ref, b_ref, o_ref):
    s = _silu(c_ref[...])
    o_ref[...] = _dot(s, w_ref[...]) + b_ref[...]


def _mod_call(cvec, ada_w, ada_b):
    depth, d, n = ada_w.shape
    tn = d
    return pl.pallas_call(
        _mod_kernel,
        grid=(depth, n // tn),
        in_specs=[pl.BlockSpec((MOD_ROWS, d), lambda l, j: (0, 0)),
                  pl.BlockSpec((None, d, tn), lambda l, j: (l, 0, j)),
                  pl.BlockSpec((None, 1, tn), lambda l, j: (l, 0, j))],
        out_specs=pl.BlockSpec((None, MOD_ROWS, tn), lambda l, j: (l, 0, j)),
        out_shape=jax.ShapeDtypeStruct((depth, MOD_ROWS, n), F32),
        compiler_params=_params(("arbitrary", "arbitrary")),
        name="ada_mod",
    )(cvec, ada_w, ada_b.reshape(depth, 1, n))


class _Geom:
    def __init__(self, n_ctx, s_ctx, n_lat, s_lat):
        self.n_ctx, self.s_ctx, self.n_lat, self.s_lat = n_ctx, s_ctx, n_lat, s_lat
        self.ctx_rows = n_ctx * s_ctx
        self.rows = self.ctx_rows + n_lat * s_lat

    def row_tile(self, want):
        t = want
        while self.ctx_rows % t or self.s_lat % t:
            t //= 2
        return t

    def scan_chunk(self):
        return min(SCAN_CHUNK, self.s_ctx, self.s_lat)

    def chunk_pos(self, t, q, fwd):
        ctx_blocks = self.ctx_rows // q
        nc_ctx, nc_lat = self.s_ctx // q, self.s_lat // q
        is_ctx = t < ctx_blocks
        tl = jnp.maximum(t - ctx_blocks, 0)
        seq = jnp.where(is_ctx, t // nc_ctx, self.n_ctx + tl // nc_lat)
        cpos = jnp.where(is_ctx, t % nc_ctx, tl % nc_lat)
        nc = jnp.where(is_ctx, nc_ctx, nc_lat)
        c = cpos if fwd else nc - 1 - cpos
        return seq, cpos, nc, c, t - cpos + c

    def group(self, tile):
        def fn(i):
            r = i * tile
            return jnp.where(r < self.ctx_rows, 0, 1 + jnp.maximum(r - self.ctx_rows, 0) // self.s_lat)
        return fn


def _mod_spec(d, layer, which, group_fn):
    return pl.BlockSpec((None, 1, d), lambda i, *_: ((layer * N_MOD + which) * MOD_ROWS + group_fn(i), 0, 0))


def _inproj_kernel(x_ref, g_ref, sh_ref, sc_ref, w_ref, o_ref, h_ref):
    @pl.when(pl.program_id(1) == 0)
    def _():
        h_ref[...] = (_rms(x_ref[...], g_ref[...]) * (1.0 + sc_ref[...]) + sh_ref[...]).astype(BF16)

    o_ref[...] = jnp.dot(h_ref[...], w_ref[...], preferred_element_type=F32)


def _inproj_call(x, mods, layer, norm_g, w, geom):
    rows, d = x.shape
    n = w.shape[1]
    tm = geom.row_tile(ROW_TILE)
    tn = IN_COL_TILE
    grp = geom.group(tm)
    return pl.pallas_call(
        _inproj_kernel,
        grid=(rows // tm, n // tn),
        in_specs=[pl.BlockSpec((tm, d), lambda i, j: (i, 0)),
                  pl.BlockSpec((1, d), lambda i, j: (0, 0)),
                  _mod_spec(d, layer, 0, grp),
                  _mod_spec(d, layer, 1, grp),
                  pl.BlockSpec((d, tn), lambda i, j: (0, j))],
        out_specs=pl.BlockSpec((tm, tn), lambda i, j: (i, j)),
        out_shape=jax.ShapeDtypeStruct((rows, n), F32),
        scratch_shapes=[pltpu.VMEM((tm, d), BF16)],
        compiler_params=_params(("arbitrary", "arbitrary")),
        name="in_proj",
    )(x, norm_g.reshape(1, d), mods, mods, w)


def _outproj_kernel(split, ctx_tiles, x_ref, gate_ref, *refs):
    n_a = sum(2 if s else 1 for s in split)
    a_refs, w_refs, o_ref = list(refs[:n_a]), refs[n_a:n_a + len(split)], refs[n_a + len(split)]
    is_ctx = pl.program_id(0) < ctx_tiles
    acc = None
    for s, w_ref in zip(split, w_refs):
        a = a_refs.pop(0)[...]
        if s:
            a = jnp.where(is_ctx, a, a_refs.pop(0)[...])
        part = jnp.dot(a, w_ref[...], preferred_element_type=F32)
        acc = part if acc is None else acc + part
    o_ref[...] = x_ref[...] + gate_ref[...] * acc


def _outproj_call(x, mods, layer, acts, ws, geom):
    rows, d = x.shape
    tm = geom.row_tile(OUT_ROW_TILE)
    grp = geom.group(tm)
    ctx_tiles = geom.ctx_rows // tm
    split = tuple(isinstance(a, tuple) for a in acts)
    in_specs = [pl.BlockSpec((tm, d), lambda i: (i, 0)), _mod_spec(d, layer, 2, grp)]
    flat = []
    for a in acts:
        if isinstance(a, tuple):
            in_specs += [pl.BlockSpec((tm, a[0].shape[1]), lambda i: (jnp.minimum(i, ctx_tiles - 1), 0)),
                         pl.BlockSpec((tm, a[1].shape[1]), lambda i: (jnp.maximum(i - ctx_tiles, 0), 0))]
            flat += list(a)
        else:
            in_specs.append(pl.BlockSpec((tm, a.shape[1]), lambda i: (i, 0)))
            flat.append(a)
    in_specs += [pl.BlockSpec(w.shape, lambda i: (0, 0)) for w in ws]
    return pl.pallas_call(
        functools.partial(_outproj_kernel, split, ctx_tiles),
        grid=(rows // tm,),
        in_specs=in_specs,
        out_specs=pl.BlockSpec((tm, d), lambda i: (i, 0)),
        out_shape=jax.ShapeDtypeStruct((rows, d), F32),
        compiler_params=_params(("arbitrary",)),
        name="out_proj",
    )(x, mods, *flat, *ws)


def _ffn_kernel(geom, nf, x_ref, xp_ref, xn_ref, g_ref, sh_ref, sc_ref, gate_ref,
                wg_ref, wv_ref, cwg_ref, cwv_ref, cbg_ref, cbv_ref, wd_ref, o_ref, h_ref, acc_ref, *u_refs):
    i, j = pl.program_id(0), pl.program_id(1)
    tm = x_ref.shape[0]
    bufs = (u_refs[0:2], u_refs[2:4])

    def up(buf):
        h = h_ref[...]
        buf[0][...] = jnp.dot(h, wg_ref[...].astype(BF16), preferred_element_type=F32)
        buf[1][...] = jnp.dot(h, wv_ref[...].astype(BF16), preferred_element_type=F32)

    def down(buf):
        rc = FFN_ROW_CHUNK
        row0 = i * tm
        is_ctx = row0 < geom.ctx_rows
        base = jnp.where(is_ctx, row0, row0 - geom.ctx_rows)
        lseq = jnp.where(is_ctx, geom.s_ctx, geom.s_lat)
        rowi = lax.broadcasted_iota(jnp.int32, (rc, LANES), 0)
        wd = wd_ref[...].astype(BF16)
        for r in range(0, tm, rc):
            at_start = (rowi == 0) & (((base + r) & (lseq - 1)) == 0)
            at_end = (rowi == rc - 1) & (((base + r + rc) & (lseq - 1)) == 0)

            def conv(u_ref, cw_ref, cb_ref):
                cols = []
                for c in range(0, u_ref.shape[1], LANES):
                    cs = slice(c, c + LANES)
                    up_ = jnp.where(at_start, 0.0, u_ref[HALO + r - 1:HALO + r - 1 + rc, cs])
                    un = jnp.where(at_end, 0.0, u_ref[HALO + r + 1:HALO + r + 1 + rc, cs])
                    cols.append(up_ * cw_ref[0:1, cs] + u_ref[HALO + r:HALO + r + rc, cs] * cw_ref[1:2, cs]
                                + un * cw_ref[2:3, cs] + cb_ref[:, cs])
                return jnp.concatenate(cols, axis=1)

            a = _silu(conv(buf[0], cwg_ref, cbg_ref)) * conv(buf[1], cwv_ref, cbv_ref)
            acc_ref[r:r + rc, :] += jnp.dot(a.astype(BF16), wd, preferred_element_type=F32)

    @pl.when(j == 0)
    def _():
        def hn(x):
            return (_rms(x, g_ref[...]) * (1.0 + sc_ref[...]) + sh_ref[...]).astype(BF16)
        h_ref[0:HALO, :] = hn(xp_ref[...])
        h_ref[HALO:HALO + tm, :] = hn(x_ref[...])
        h_ref[HALO + tm:, :] = hn(xn_ref[...])
        acc_ref[...] = jnp.zeros_like(acc_ref)
        up(bufs[0])

    for parity in range(2):
        @pl.when((j > 0) & (j < nf) & (j % 2 == parity))
        def _():
            up(bufs[parity])
            down(bufs[1 - parity])

    @pl.when(j == nf)
    def _():
        down(bufs[(nf - 1) % 2])
        o_ref[...] = x_ref[...] + gate_ref[...] * acc_ref[...]


def _ffn_call(x, mods, layer, norm_g, w_up, conv_w, conv_b, w_down, geom):
    rows, d = x.shape
    dff = w_down.shape[0]
    tm = geom.row_tile(ROW_TILE)
    tf = FF_TILE
    nf = dff // tf
    grp = geom.group(tm)
    nhalo = rows // HALO
    conv_b = conv_b.reshape(1, 2 * dff)
    cur = lambda j: jnp.minimum(j, nf - 1)
    prv = lambda j: jnp.maximum(j - 1, 0)
    return pl.pallas_call(
        functools.partial(_ffn_kernel, geom, nf),
        grid=(rows // tm, nf + 1),
        in_specs=[pl.BlockSpec((tm, d), lambda i, j: (i, 0)),
                  pl.BlockSpec((HALO, d), lambda i, j: (jnp.maximum(i * (tm // HALO) - 1, 0), 0)),
                  pl.BlockSpec((HALO, d), lambda i, j: (jnp.minimum((i + 1) * (tm // HALO), nhalo - 1), 0)),
                  pl.BlockSpec((1, d), lambda i, j: (0, 0)),
                  _mod_spec(d, layer, 3, grp), _mod_spec(d, layer, 4, grp), _mod_spec(d, layer, 5, grp),
                  pl.BlockSpec((d, tf), lambda i, j: (0, cur(j))),
                  pl.BlockSpec((d, tf), lambda i, j: (0, nf + cur(j))),
                  pl.BlockSpec((3, tf), lambda i, j: (0, prv(j))),
                  pl.BlockSpec((3, tf), lambda i, j: (0, nf + prv(j))),
                  pl.BlockSpec((1, tf), lambda i, j: (0, prv(j))),
                  pl.BlockSpec((1, tf), lambda i, j: (0, nf + prv(j))),
                  pl.BlockSpec((tf, d), lambda i, j: (prv(j), 0))],
        out_specs=pl.BlockSpec((tm, d), lambda i, j: (i, 0)),
        out_shape=jax.ShapeDtypeStruct((rows, d), F32),
        scratch_shapes=[pltpu.VMEM((tm + 2 * HALO, d), BF16), pltpu.VMEM((tm, d), F32)]
        + [pltpu.VMEM((tm + 2 * HALO, tf), F32) for _ in range(4)],
        compiler_params=_params(("arbitrary", "arbitrary")),
        name="conv_ffn",
    )(x, x, x, norm_g.reshape(1, d), mods, mods, mods, w_up, w_up, conv_w, conv_w, conv_b, conv_b, w_down)


def _final_kernel(x_ref, g_ref, o_ref):
    o_ref[...] = _rms(x_ref[...], g_ref[...])


def _final_call(x, g, row0, nrows):
    d = x.shape[1]
    tm = OUT_ROW_TILE
    while nrows % tm or row0 % tm:
        tm //= 2
    return pl.pallas_call(
        _final_kernel,
        grid=(nrows // tm,),
        in_specs=[pl.BlockSpec((tm, d), lambda i: (row0 // tm + i, 0)), pl.BlockSpec((1, d), lambda i: (0, 0))],
        out_specs=pl.BlockSpec((tm, d), lambda i: (i, 0)),
        out_shape=jax.ShapeDtypeStruct((nrows, d), F32),
        compiler_params=_params(("arbitrary",)),
        name="final_norm",
    )(x, g.reshape(1, d))


def _ssd_kernel(fwd, geom, heads, *refs):
    (xs_ref, xsp_ref, xsn_ref, bc_ref, bcp_ref, bcn_ref, dt_ref, cwx_ref, cbx_ref, cwb_ref, cbb_ref,
     dtb_ref, alog_ref, pe_ref, bd_ref, h0_ref) = refs[:16]
    if fwd:
        z_ref, yb_ref, dx_ref, gn_ref, out_ref, st_ref, s_ref, yacc_ref = refs[16:]
    else:
        out_ref, st_ref, s_ref = refs[16:]
    q = xs_ref.shape[0]
    _, cpos, nc, c, _ = geom.chunk_pos(pl.program_id(0), q, fwd)
    first, last = c == 0, c == nc - 1
    half = LANES // 2

    @pl.when(cpos == 0)
    def _():
        s_ref[...] = h0_ref[...]

    xs = _conv3_silu(xs_ref[...], xsp_ref[...], xsn_ref[...], cwx_ref[...], cbx_ref[...], first, last)
    bc = _conv3_silu(bc_ref[...], bcp_ref[...], bcn_ref[...], cwb_ref[...], cbb_ref[...], first, last)
    bm, cm = bc[:, :LANES], bc[:, LANES:]
    dt = _softplus(dt_ref[...] + dtb_ref[...])
    la = dt * (-jnp.exp(alog_ref[...]))
    row = lax.broadcasted_iota(jnp.int32, (q, q), 0)
    col = lax.broadcasted_iota(jnp.int32, (q, q), 1)
    tril, triu = row >= col, row <= col
    cs_p = _mask_sum(tril.astype(F32).astype(BF16), la)
    cs_s = _mask_sum(triu.astype(F32).astype(BF16), la)
    cs = cs_p if fwd else cs_s
    edge = q - 1 if fwd else 0
    ecs_x = _expand(jnp.exp(cs), pe_ref[...])
    wd_x = _expand(dt * jnp.exp(cs[edge:edge + 1, :] - cs), pe_ref[...])
    s_old = s_ref[...]
    y_off = _dot(cm, s_old) * ecs_x
    s_new = (s_old * ecs_x[edge:edge + 1, :] + _dot_tn(bm, xs * wd_x)) * bd_ref[...]
    s_ref[...] = s_new

    @pl.when(cpos == nc - 1)
    def _():
        st_ref[...] = s_new

    if not fwd:
        out_ref[...] = y_off
        return

    cst_p, cst_s, dtt = cs_p.T, cs_s.T, dt.T
    lane = lax.broadcasted_iota(jnp.int32, (q, LANES), 1)
    lo, hi = lane < half, lane >= half
    gmat = (_dot_nt(jnp.where(lo, cm, 0.0), bm), _dot_nt(jnp.where(hi, cm, 0.0), bm))
    for p in range(heads // 2):
        xp = xs[:, p * LANES:(p + 1) * LANES]
        acc = None
        for hh in range(2):
            h = 2 * p + hh
            hb = heads + h
            lf = jnp.where(tril, jnp.exp(jnp.minimum(cs_p[:, h:h + 1] - cst_p[h:h + 1, :], 0.0)), 0.0)
            lb = jnp.where(triu, jnp.exp(jnp.minimum(cs_s[:, hb:hb + 1] - cst_s[hb:hb + 1, :], 0.0)), 0.0)
            w = gmat[h // (heads // 2)] * (lf * dtt[h:h + 1, :] + lb * dtt[hb:hb + 1, :])
            part = _dot(w, jnp.where(lo if hh == 0 else hi, xp, 0.0))
            acc = part if acc is None else acc + part
        yacc_ref[:, p * LANES:(p + 1) * LANES] = acc
    y = yacc_ref[...] + y_off + yb_ref[...] + dx_ref[...] * xs
    yz = y * _silu(z_ref[...])
    out_ref[...] = _rms(yz, gn_ref[...]).astype(BF16)


def _ssd_call(fwd, proj, yb, prm, h0, geom, heads, width):
    rows = proj.shape[0]
    q = geom.scan_chunk()
    nhalo = rows // HALO
    bc_w = 2 * LANES
    xs_blk, z_blk, bc_blk, dt_blk = 0, 1, 5 * width // bc_w, (5 * width + bc_w) // LANES

    def seq(t):
        return geom.chunk_pos(t, q, fwd)[0]

    def rowblk(t):
        return geom.chunk_pos(t, q, fwd)[4]

    def prev(t):
        return jnp.maximum(rowblk(t) * (q // HALO) - 1, 0)

    def nxt(t):
        return jnp.minimum((rowblk(t) + 1) * (q // HALO), nhalo - 1)

    const = lambda t: (0, 0)
    in_specs = [pl.BlockSpec((q, width), lambda t: (rowblk(t), xs_blk)),
                pl.BlockSpec((HALO, width), lambda t: (prev(t), xs_blk)),
                pl.BlockSpec((HALO, width), lambda t: (nxt(t), xs_blk)),
                pl.BlockSpec((q, bc_w), lambda t: (rowblk(t), bc_blk)),
                pl.BlockSpec((HALO, bc_w), lambda t: (prev(t), bc_blk)),
                pl.BlockSpec((HALO, bc_w), lambda t: (nxt(t), bc_blk)),
                pl.BlockSpec((q, LANES), lambda t: (rowblk(t), dt_blk)),
                pl.BlockSpec((3, width), const), pl.BlockSpec((1, width), const),
                pl.BlockSpec((3, bc_w), const), pl.BlockSpec((1, bc_w), const),
                pl.BlockSpec((1, LANES), const), pl.BlockSpec((1, LANES), const),
                pl.BlockSpec((LANES, width), const), pl.BlockSpec((LANES, width), const),
                pl.BlockSpec((None, LANES, width), lambda t: (seq(t), 0, 0))]
    args = [proj, proj, proj, proj, proj, proj, proj, prm["cwx"], prm["cbx"], prm["cwb"], prm["cbb"],
            prm["dtb"], prm["alog"], prm["pf"] if fwd else prm["pb"], prm["bd"], h0]
    scratch = [pltpu.VMEM((LANES, width), F32)]
    if fwd:
        in_specs += [pl.BlockSpec((q, width), lambda t: (rowblk(t), z_blk)),
                     pl.BlockSpec((q, width), lambda t: (rowblk(t), 0)),
                     pl.BlockSpec((1, width), const), pl.BlockSpec((1, width), const)]
        args += [proj, yb, prm["dx"], prm["gn"]]
        scratch.append(pltpu.VMEM((q, width), F32))
        out_dtype = BF16
    else:
        out_dtype = F32
    return pl.pallas_call(
        functools.partial(_ssd_kernel, fwd, geom, heads),
        grid=(rows // q,),
        in_specs=in_specs,
        out_specs=[pl.BlockSpec((q, width), lambda t: (rowblk(t), 0)),
                   pl.BlockSpec((None, LANES, width), lambda t: (seq(t), 0, 0))],
        out_shape=[jax.ShapeDtypeStruct((rows, width), out_dtype),
                   jax.ShapeDtypeStruct((geom.n_ctx + geom.n_lat, LANES, width), F32)],
        scratch_shapes=scratch,
        compiler_params=_params(("arbitrary",)),
        name="ssd_fwd" if fwd else "ssd_bwd",
    )(*args)


def _rope(x, cos, sin_signed):
    lane = lax.broadcasted_iota(jnp.int32, x.shape, 1)
    partner = jnp.where((lane & 31) < 16, pltpu.roll(x, LANES - 16, 1), pltpu.roll(x, 16, 1))
    return x * cos + partner * sin_signed


def _attn_kernel(latent, lam_init, *refs):
    if latent:
        (q_ref, k_ref, v_ref, kc_ref, vc_ref, cq_ref, sq_ref, ck_ref, sk_ref, lp_ref, g_ref,
         o_ref, kr_ref, vr_ref) = refs
    else:
        q_ref, k_ref, v_ref, lp_ref, g_ref, o_ref, kr_ref, vr_ref = refs

    @pl.when(pl.program_id(2) == 0)
    def _():
        k = k_ref[...]
        if latent:
            k = _rope(k, ck_ref[...], sk_ref[...])
        kr_ref[...] = k.astype(BF16)
        vr_ref[...] = v_ref[...].astype(BF16)

    q = q_ref[...]
    if latent:
        q = _rope(q, cq_ref[...], sq_ref[...])
    q = q * (float(LANES // 2) ** -0.5)
    lane = lax.broadcasted_iota(jnp.int32, q.shape, 1)
    qs = (jnp.where(lane < LANES // 2, q, 0.0).astype(BF16), jnp.where(lane >= LANES // 2, q, 0.0).astype(BF16))
    keys = [kr_ref[...]] + ([kc_ref[...].astype(BF16)] if latent else [])
    vals = [vr_ref[...]] + ([vc_ref[...].astype(BF16)] if latent else [])
    lp = lp_ref[...]
    lam = (jnp.exp(jnp.sum(lp[0:1] * lp[1:2], axis=1, keepdims=True))
           - jnp.exp(jnp.sum(lp[2:3] * lp[3:4], axis=1, keepdims=True)) + lam_init)
    probs, coef = [], []
    for m in range(2):
        s = [lax.dot_general(qs[m], kk, (((1,), (1,)), ((), ())), preferred_element_type=F32) for kk in keys]
        mx = functools.reduce(jnp.maximum, [jnp.max(t, axis=1, keepdims=True) for t in s])
        p = [jnp.exp(t - mx) for t in s]
        den = functools.reduce(jnp.add, [jnp.sum(t, axis=1, keepdims=True) for t in p])
        probs.append(p)
        coef.append(1.0 / den if m == 0 else lam / den)
    o = None
    for seg in range(len(keys)):
        pd = (probs[0][seg] * coef[0] - probs[1][seg] * coef[1]).astype(BF16)
        part = jnp.dot(pd, vals[seg], preferred_element_type=F32)
        o = part if o is None else o + part
    o_ref[...] = (_rms(o, g_ref[...]) * (1.0 - lam_init)).astype(BF16)


def _attn_ctx_kernel(lam_init, heads, q_ref, k_ref, v_ref, lp_ref, g_ref, o_ref):
    rows = q_ref.shape[0]
    lane = lax.broadcasted_iota(jnp.int32, (rows, LANES), 1)
    lo = lane < LANES // 2
    scale = float(LANES // 2) ** -0.5
    lp = lp_ref[...]
    lam = (jnp.exp(jnp.sum(lp[0:1] * lp[1:2], axis=1, keepdims=True))
           - jnp.exp(jnp.sum(lp[2:3] * lp[3:4], axis=1, keepdims=True)) + lam_init)
    hs = range(heads)
    sls = [slice(h * LANES, (h + 1) * LANES) for h in hs]
    q = [q_ref[:, sl] * scale for sl in sls]
    k = [k_ref[:, sl].astype(BF16) for sl in sls]
    s0 = [_dot_nt(jnp.where(lo, q[h], 0.0), k[h]) for h in hs]
    s1 = [_dot_nt(jnp.where(lo, 0.0, q[h]), k[h]) for h in hs]
    pd = []
    for h in hs:
        p0 = jnp.exp(s0[h] - jnp.max(s0[h], axis=1, keepdims=True))
        p1 = jnp.exp(s1[h] - jnp.max(s1[h], axis=1, keepdims=True))
        c0 = 1.0 / jnp.sum(p0, axis=1, keepdims=True)
        c1 = lam / jnp.sum(p1, axis=1, keepdims=True)
        pd.append(p0 * c0 - p1 * c1)
    o = [_dot(pd[h], v_ref[:, sls[h]]) for h in hs]
    for h in hs:
        o_ref[:, sls[h]] = (_rms(o[h], g_ref[...]) * (1.0 - lam_init)).astype(BF16)


def _attn_ctx_call(lam_init, proj, lam_p, norm_g, n_seq, seq_len, heads, width):
    spec = lambda blk: pl.BlockSpec((seq_len, width), lambda b: (b, blk))
    return pl.pallas_call(
        functools.partial(_attn_ctx_kernel, lam_init, heads),
        grid=(n_seq,),
        in_specs=[spec(2), spec(3), spec(4), pl.BlockSpec(lam_p.shape, lambda b: (0, 0)),
                  pl.BlockSpec((1, LANES), lambda b: (0, 0))],
        out_specs=pl.BlockSpec((seq_len, width), lambda b: (b, 0)),
        out_shape=jax.ShapeDtypeStruct((n_seq * seq_len, width), BF16),
        compiler_params=_params(("arbitrary",)),
        name="diff_attn_ctx",
    )(proj, proj, proj, lam_p, norm_g.reshape(1, LANES))


def _attn_call(latent, lam_init, proj, cache_k, cache_v, tables, lam_p, norm_g, row0, n_seq, seq_len, heads, width):
    tq = min(Q_TILE, seq_len)
    nq = seq_len // tq
    hpw = width // LANES
    q_blk, k_blk, v_blk = 2 * hpw, 3 * hpw, 4 * hpw
    qb, sb = row0 // tq, row0 // seq_len
    in_specs = [pl.BlockSpec((tq, LANES), lambda b, h, i: (qb + b * nq + i, q_blk + h)),
                pl.BlockSpec((seq_len, LANES), lambda b, h, i: (sb + b, k_blk + h)),
                pl.BlockSpec((seq_len, LANES), lambda b, h, i: (sb + b, v_blk + h))]
    args = [proj, proj, proj]
    if latent:
        past = cache_k.shape[1]
        in_specs += [pl.BlockSpec((None, past, LANES), lambda b, h, i: (b, 0, h)),
                     pl.BlockSpec((None, past, LANES), lambda b, h, i: (b, 0, h)),
                     pl.BlockSpec((tq, LANES), lambda b, h, i: (i, 0)),
                     pl.BlockSpec((tq, LANES), lambda b, h, i: (i, 0)),
                     pl.BlockSpec((seq_len, LANES), lambda b, h, i: (0, 0)),
                     pl.BlockSpec((seq_len, LANES), lambda b, h, i: (0, 0))]
        args += [cache_k, cache_v, tables[0], tables[1], tables[0], tables[1]]
    in_specs += [pl.BlockSpec(lam_p.shape, lambda b, h, i: (0, 0)), pl.BlockSpec((1, LANES), lambda b, h, i: (0, 0))]
    args += [lam_p, norm_g.reshape(1, LANES)]
    return pl.pallas_call(
        functools.partial(_attn_kernel, latent, lam_init),
        grid=(n_seq, heads, nq),
        in_specs=in_specs,
        out_specs=pl.BlockSpec((tq, LANES), lambda b, h, i: (b * nq + i, h)),
        out_shape=jax.ShapeDtypeStruct((n_seq * seq_len, width), BF16),
        scratch_shapes=[pltpu.VMEM((seq_len, LANES), BF16), pltpu.VMEM((seq_len, LANES), BF16)],
        compiler_params=_params(("arbitrary", "arbitrary", "arbitrary")),
        name="diff_attn_lat" if latent else "diff_attn_ctx",
    )(*args)


def _rope_tables(n_tokens, grid_w):
    n_freq = LANES // 8
    pos = jnp.arange(n_tokens)
    r = (pos // grid_w).astype(F32)
    cpos = (pos % grid_w).astype(F32)
    inv = ROPE_BASE ** (-jnp.arange(n_freq, dtype=F32) / n_freq)
    ang_r, ang_c = r[:, None] * inv, cpos[:, None] * inv
    cos32 = lambda a: jnp.concatenate([jnp.cos(a), jnp.cos(a)], axis=1)
    sin32 = lambda a: jnp.concatenate([-jnp.sin(a), jnp.sin(a)], axis=1)
    cos64 = jnp.concatenate([cos32(ang_r), cos32(ang_c)], axis=1)
    sin64 = jnp.concatenate([sin32(ang_r), sin32(ang_c)], axis=1)
    return jnp.concatenate([cos64, cos64], axis=1), jnp.concatenate([sin64, sin64], axis=1)


def _gdn_masks(rb, fwd):
    row = jnp.arange(rb)[:, None]
    col = jnp.arange(rb)[None, :]
    shift = GDN_CHUNK.bit_length() - 1
    blk = (row >> shift) == (col >> shift)
    incl = blk & ((row >= col) if fwd else (row <= col))
    strict = blk & ((row > col) if fwd else (row < col))
    ms = [incl, strict, row == col, (row >> 1) == (col >> 1)]
    for lvl in range(1, shift):
        ms.append(((row >> (lvl + 1)) == (col >> (lvl + 1))) & ((row >> lvl) != (col >> lvl)))
    return jnp.stack(ms).astype(F32), jnp.stack([incl, blk]).astype(BF16)


def _gdn_kernel(fwd, geom, heads, hps, *refs):
    (q_ref, qp_ref, qn_ref, k_ref, kp_ref, kn_ref, v_ref, vp_ref, vn_ref, ab_ref,
     cwq_ref, cbq_ref, cwk_ref, cbk_ref, cwv_ref, cbv_ref, alog_ref, dtb_ref, mf_ref, mb_ref, s0_ref) = refs[:21]
    if fwd:
        z_ref, ob_ref, gn_ref, out_ref, st_ref, s_ref = refs[21:]
    else:
        out_ref, st_ref, s_ref = refs[21:]
    hg = pl.program_id(0)
    rb = q_ref.shape[0]
    _, cpos, nb, c, _ = geom.chunk_pos(pl.program_id(1), rb, fwd)
    first, last = c == 0, c == nb - 1
    dk = q_ref.shape[1] // hps
    ch = GDN_CHUNK
    nsub = rb // ch
    n_lvl = ch.bit_length() - 1

    @pl.when(cpos == 0)
    def _():
        s_ref[...] = s0_ref[...]

    ab = ab_ref[...]
    d0 = 0 if fwd else 2 * heads
    g_all = -jnp.exp(alog_ref[...]) * _softplus(ab + dtb_ref[...])
    b_all = 1.0 / (1.0 + jnp.exp(-ab))
    gc_all = _mask_sum(mb_ref[0], g_all)
    gt_all = _mask_sum(mb_ref[1], g_all)
    gct_all = gc_all.T
    lane = lax.broadcasted_iota(jnp.int32, ab.shape, 1)
    sub = lax.broadcasted_iota(jnp.int32, gct_all.shape, 0)
    incl_f, strict_f, eye, pair = mf_ref[0], mf_ref[1], mf_ref[2], mf_ref[3]

    def pick(a, l):
        return jnp.sum(jnp.where(lane == l, a, 0.0), axis=1, keepdims=True)

    hs = range(hps)
    sls = [slice(hh * dk, (hh + 1) * dk) for hh in hs]
    kn, qn, v, gc, gtot, beta, dec, m, t = ([None] * hps for _ in range(9))
    for hh in hs:
        hd = hg * hps + hh
        sl = sls[hh]

        def conv(x_ref, xp_ref, xn_ref, cw_ref, cb_ref):
            return _conv3_silu(x_ref[:, sl], xp_ref[:, sl], xn_ref[:, sl], cw_ref[:, sl], cb_ref[:, sl], first, last)

        qc = conv(q_ref, qp_ref, qn_ref, cwq_ref, cbq_ref)
        kc = conv(k_ref, kp_ref, kn_ref, cwk_ref, cbk_ref)
        v[hh] = conv(v_ref, vp_ref, vn_ref, cwv_ref, cbv_ref)
        qn[hh] = qc * lax.rsqrt(jnp.sum(qc * qc, axis=1, keepdims=True) + EPS) * (float(dk) ** -0.5)
        kn[hh] = kc * lax.rsqrt(jnp.sum(kc * kc, axis=1, keepdims=True) + EPS)
        gc[hh], gtot[hh] = pick(gc_all, d0 + hd), pick(gt_all, d0 + hd)
        beta[hh] = pick(b_all, d0 + heads + hd)
        g_row = jnp.sum(jnp.where(sub == d0 + hd, gct_all, 0.0), axis=0, keepdims=True)
        dec[hh] = jnp.exp(jnp.minimum(gc[hh] - g_row, 0.0)) * incl_f
    kk = [_dot_nt(kn[hh], kn[hh]) for hh in hs]
    for hh in hs:
        m[hh] = (kk[hh] * beta[hh]) * (dec[hh] * strict_f)
        t[hh] = eye - m[hh] * pair
    for lvl in range(1, n_lvl):
        off = mf_ref[3 + lvl]
        tm = [_dot(t[hh], m[hh] * off) for hh in hs]
        tmt = [_dot(tm[hh], t[hh]) for hh in hs]
        t = [t[hh] - tmt[hh] for hh in hs]
    eg = [jnp.exp(gc[hh]) for hh in hs]
    sol = [_dot(t[hh], jnp.concatenate([v[hh] * beta[hh], kn[hh] * (beta[hh] * eg[hh])], axis=1)) for hh in hs]
    qk = [_dot_nt(qn[hh], kn[hh]) for hh in hs]
    qg = [qn[hh] * eg[hh] for hh in hs]
    kg = [kn[hh] * jnp.exp(gtot[hh] - gc[hh]) for hh in hs]
    egl = [jnp.exp(gtot[hh]) for hh in hs]
    s = [s_ref[hh] for hh in hs]
    vnew = [[None] * nsub for _ in hs]
    qs = [[None] * nsub for _ in hs]
    for i in (range(nsub) if fwd else reversed(range(nsub))):
        r = slice(i * ch, (i + 1) * ch)
        ws = [_dot(jnp.concatenate([sol[hh][r, dk:], qg[hh][r]], axis=0), s[hh]) for hh in hs]
        for hh in hs:
            vnew[hh][i] = sol[hh][r, :dk] - ws[hh][:ch]
            qs[hh][i] = ws[hh][ch:]
        upd = [_dot_tn(kg[hh][r], vnew[hh][i]) for hh in hs]
        s = [s[hh] * egl[hh][i * ch:i * ch + 1, :] + upd[hh] for hh in hs]
    intra = [_dot(qk[hh] * dec[hh], jnp.concatenate(vnew[hh], axis=0)) for hh in hs]
    for hh in hs:
        sl = sls[hh]
        s_ref[hh] = s[hh]
        o = jnp.concatenate(qs[hh], axis=0) + intra[hh]
        if fwd:
            o = o + ob_ref[:, sl]
            out_ref[:, sl] = (_rms(o, gn_ref[...]) * _silu(z_ref[:, sl])).astype(BF16)
        else:
            out_ref[:, sl] = o

    @pl.when(cpos == nb - 1)
    def _():
        st_ref[...] = s_ref[...]


def _gdn_call(fwd, proj, ob, prm, s0, geom, heads, width):
    rows = proj.shape[0]
    rb = geom.scan_chunk()
    nhalo = rows // HALO
    dk = width // heads
    hps = GDN_HEADS_PER_STEP
    hw = hps * dk
    ngrp = heads // hps
    ab_blk = 4 * width // LANES
    mf, mb = _gdn_masks(rb, fwd)

    def seq(t):
        return geom.chunk_pos(t, rb, fwd)[0]

    def rowblk(t):
        return geom.chunk_pos(t, rb, fwd)[4]

    def prev(t):
        return jnp.maximum(rowblk(t) * (rb // HALO) - 1, 0)

    def nxt(t):
        return jnp.minimum((rowblk(t) + 1) * (rb // HALO), nhalo - 1)

    in_specs, args = [], []
    for sec in range(3):
        in_specs += [pl.BlockSpec((rb, hw), lambda h, t, sec=sec: (rowblk(t), sec * ngrp + h)),
                     pl.BlockSpec((HALO, hw), lambda h, t, sec=sec: (prev(t), sec * ngrp + h)),
                     pl.BlockSpec((HALO, hw), lambda h, t, sec=sec: (nxt(t), sec * ngrp + h))]
        args += [proj, proj, proj]
    in_specs.append(pl.BlockSpec((rb, LANES), lambda h, t: (rowblk(t), ab_blk)))
    args.append(proj)
    for sec in range(3):
        in_specs += [pl.BlockSpec((3, hw), lambda h, t, sec=sec: (0, sec * ngrp + h)),
                     pl.BlockSpec((1, hw), lambda h, t, sec=sec: (0, sec * ngrp + h))]
        args += [prm["cw"], prm["cb"]]
    in_specs += [pl.BlockSpec((1, LANES), lambda h, t: (0, 0)), pl.BlockSpec((1, LANES), lambda h, t: (0, 0)),
                 pl.BlockSpec(mf.shape, lambda h, t: (0, 0, 0)), pl.BlockSpec(mb.shape, lambda h, t: (0, 0, 0)),
                 pl.BlockSpec((None, hps, dk, dk), lambda h, t: (seq(t), h, 0, 0))]
    args += [prm["alog"], prm["dtb"], mf, mb, s0]
    if fwd:
        in_specs += [pl.BlockSpec((rb, hw), lambda h, t: (rowblk(t), 3 * ngrp + h)),
                     pl.BlockSpec((rb, hw), lambda h, t: (rowblk(t), h)),
                     pl.BlockSpec((1, dk), lambda h, t: (0, 0))]
        args += [proj, ob, prm["gn"]]
    return pl.pallas_call(
        functools.partial(_gdn_kernel, fwd, geom, heads, hps),
        grid=(ngrp, rows // rb),
        in_specs=in_specs,
        out_specs=[pl.BlockSpec((rb, hw), lambda h, t: (rowblk(t), h)),
                   pl.BlockSpec((None, hps, dk, dk), lambda h, t: (seq(t), h, 0, 0))],
        out_shape=[jax.ShapeDtypeStruct((rows, width), BF16 if fwd else F32),
                   jax.ShapeDtypeStruct((geom.n_ctx + geom.n_lat, heads, dk, dk), F32)],
        scratch_shapes=[pltpu.VMEM((hps, dk, dk), F32)],
        compiler_params=_params(("arbitrary", "arbitrary")),
        name="gdn_fwd" if fwd else "gdn_bwd",
    )(*args)


def _pad_cols(w, n):
    return jnp.pad(w, ((0, 0), (0, n - w.shape[1])))


def _lane_row(v):
    v = v.reshape(1, -1).astype(F32)
    return _pad_cols(v, LANES)


def _even_layer(x, mods, layer, e, geom, norm_g, w_in, conv_w, conv_b, dt_bias, a_log, d_skip, ssd_g,
                lam_p, da_g, w_out, cache_k, cache_v, state_ssd, tables):
    d = x.shape[1]
    heads, p_dim, n_state = state_ssd.shape[3], state_ssd.shape[4], state_ssd.shape[5]
    width = heads * p_dim
    da_heads = cache_k.shape[3]
    assert width == d and da_heads * LANES == d and 2 * n_state == LANES and 2 * p_dim == LANES
    bc_w = 2 * LANES
    n_dt = 2 * heads
    o_z, o_xs, o_bc, o_dt = 0, width, 2 * width, 2 * width + bc_w
    o_q = o_dt + n_dt
    cols = [w_in[:, o_xs:o_xs + width], w_in[:, o_z:o_z + width], w_in[:, o_q:o_q + 3 * d],
            w_in[:, o_bc:o_bc + bc_w], w_in[:, o_dt:o_dt + n_dt]]
    n_used = 5 * width + bc_w + n_dt
    n_pad = -(-n_used // IN_COL_TILE) * IN_COL_TILE
    w_perm = _pad_cols(jnp.concatenate(cols, axis=1), n_pad).astype(BF16)
    proj = _inproj_call(x, mods, layer, norm_g, w_perm, geom)

    hl = jnp.arange(LANES)[:, None]
    cl = jnp.arange(width)[None, :]
    prm = {
        "cwx": conv_w[:, :width], "cbx": conv_b[None, :width],
        "cwb": conv_w[:, width:], "cbb": conv_b[None, width:],
        "dtb": _lane_row(dt_bias), "alog": _lane_row(a_log),
        "pf": (hl == cl // p_dim).astype(BF16), "pb": (hl == heads + cl // p_dim).astype(BF16),
        "bd": (hl // n_state == cl // (width // 2)).astype(F32),
        "dx": jnp.repeat(d_skip, p_dim)[None, :], "gn": ssd_g[None, :],
    }

    def state_in(st):
        t = st.transpose(0, 3, 1, 2).reshape(st.shape[0], n_state, width)
        return jnp.concatenate([t, t], axis=1) * prm["bd"]

    def state_out(st):
        per_h = st.reshape(st.shape[0], 2, n_state, heads, p_dim)
        sel = jnp.concatenate([per_h[:, 0, :, :heads // 2], per_h[:, 1, :, heads // 2:]], axis=2)
        return sel.transpose(0, 2, 3, 1)

    zeros = jnp.zeros((geom.n_ctx, LANES, width), F32)
    h0f = jnp.concatenate([zeros, state_in(state_ssd[:, e, 0].astype(F32))], axis=0)
    h0b = jnp.concatenate([zeros, state_in(state_ssd[:, e, 1].astype(F32))], axis=0)
    yb, st_b = _ssd_call(False, proj, None, prm, h0b, geom, heads, width)
    y, st_f = _ssd_call(True, proj, yb, prm, h0f, geom, heads, width)
    new_states = jnp.stack([state_out(st_f[:geom.n_ctx]), state_out(st_b[:geom.n_ctx])], axis=1)
    lam_init = 0.8 - 0.6 * math.exp(-0.3 * layer)
    o_ctx = _attn_ctx_call(lam_init, proj, lam_p, da_g, geom.n_ctx, geom.s_ctx, da_heads, d)
    o_lat = _attn_call(True, lam_init, proj, cache_k[:, e].reshape(geom.n_lat, -1, d),
                       cache_v[:, e].reshape(geom.n_lat, -1, d), tables, lam_p, da_g, geom.ctx_rows,
                       geom.n_lat, geom.s_lat, da_heads, d)
    w_o = w_out.astype(BF16)
    x = _outproj_call(x, mods, layer, [y, (o_ctx, o_lat)], [w_o[:width], w_o[width:]], geom)
    k_new = proj[:geom.ctx_rows, 3 * d:4 * d].reshape(geom.n_ctx, geom.s_ctx, da_heads, LANES)
    v_new = proj[:geom.ctx_rows, 4 * d:5 * d].reshape(geom.n_ctx, geom.s_ctx, da_heads, LANES)
    return x, k_new, v_new, new_states


def _odd_layer(x, mods, layer, o, geom, norm_g, w_in, conv_w, conv_b, a_log, dt_bias, g_norm, w_out, state_gdn):
    d = x.shape[1]
    heads, dk, dv = state_gdn.shape[3], state_gdn.shape[4], state_gdn.shape[5]
    width = heads * dv
    assert width == d and dk == LANES and dv == LANES and 4 * heads <= LANES
    n_used = 4 * width + 4 * heads
    n_pad = -(-n_used // IN_COL_TILE) * IN_COL_TILE
    w_perm = _pad_cols(w_in, n_pad).astype(BF16)
    proj = _inproj_call(x, mods, layer, norm_g, w_perm, geom)

    def ab_row(v):
        r = jnp.zeros((2, 2 * heads), F32).at[:, :heads].set(v.astype(F32))
        return _lane_row(r)

    prm = {"cw": conv_w, "cb": conv_b[None, :], "alog": ab_row(a_log), "dtb": ab_row(dt_bias), "gn": g_norm[None, :]}
    zeros = jnp.zeros((geom.n_ctx, heads, dk, dv), F32)
    s0f = jnp.concatenate([zeros, state_gdn[:, o, 0].astype(F32)], axis=0)
    s0b = jnp.concatenate([zeros, state_gdn[:, o, 1].astype(F32)], axis=0)
    ob, st_b = _gdn_call(False, proj, None, prm, s0b, geom, heads, width)
    og, st_f = _gdn_call(True, proj, ob, prm, s0f, geom, heads, width)
    new_states = jnp.stack([st_f[:geom.n_ctx], st_b[:geom.n_ctx]], axis=1)
    x = _outproj_call(x, mods, layer, [og], [w_out.astype(BF16)], geom)
    return x, new_states


def kernel(x_prompt, x_sample, cache_attn_k, cache_attn_v, state_ssd, state_gdn, c, c_ctx, ada_w, ada_b, norm_g, ev_w_in, ev_conv_w, ev_conv_b, ssd_dt_bias, ssd_a_log, ssd_d, ssd_norm_g, da_lambda, da_norm_g, ev_w_out, od_w_in, od_conv_w, od_conv_b, gdn_a_log, gdn_dt_bias, gdn_norm_g, od_w_out, ffn_w_up, ffn_conv_w, ffn_conv_b, ffn_w_down, final_norm_g):
    n_ctx, s_ctx, d = x_prompt.shape
    n_lat, s_lat, _ = x_sample.shape
    depth = ada_w.shape[0]
    assert n_lat + 1 <= MOD_ROWS and s_ctx & (s_ctx - 1) == 0 and s_lat & (s_lat - 1) == 0
    geom = _Geom(n_ctx, s_ctx, n_lat, s_lat)
    dtype = x_prompt.dtype
    x = jnp.concatenate([x_prompt.reshape(-1, d), x_sample.reshape(-1, d)], axis=0).astype(F32)

    cvec = jnp.zeros((MOD_ROWS, d), F32).at[0].set(c_ctx).at[1:1 + n_lat].set(c)
    mods = _mod_call(cvec, ada_w, ada_b)
    mods = mods.reshape(depth, MOD_ROWS, N_MOD, d).transpose(0, 2, 1, 3).reshape(depth * N_MOD * MOD_ROWS, 1, d)

    grid_w = 64
    tables = _rope_tables(s_lat, grid_w)
    ks_out, vs_out, ssd_out, gdn_out = [], [], [], []
    for l in range(depth):
        if l % 2 == 0:
            e = l // 2
            x, k_new, v_new, st = _even_layer(
                x, mods, l, e, geom, norm_g[l, 0], ev_w_in[e], ev_conv_w[e], ev_conv_b[e], ssd_dt_bias[e],
                ssd_a_log[e], ssd_d[e], ssd_norm_g[e], da_lambda[e], da_norm_g[e], ev_w_out[e],
                cache_attn_k, cache_attn_v, state_ssd, tables)
            ks_out.append(k_new)
            vs_out.append(v_new)
            ssd_out.append(st)
        else:
            o = l // 2
            x, st = _odd_layer(x, mods, l, o, geom, norm_g[l, 0], od_w_in[o], od_conv_w[o], od_conv_b[o],
                               gdn_a_log[o], gdn_dt_bias[o], gdn_norm_g[o], od_w_out[o], state_gdn)
            gdn_out.append(st)
        x = _ffn_call(x, mods, l, norm_g[l, 1], ffn_w_up[l], ffn_conv_w[l], ffn_conv_b[l], ffn_w_down[l], geom)
    y_prompt = _final_call(x, final_norm_g, 0, geom.ctx_rows).reshape(n_ctx, s_ctx, d)
    y_sample = _final_call(x, final_norm_g, geom.ctx_rows, n_lat * s_lat).reshape(n_lat, s_lat, d)
    return (y_prompt.astype(dtype), y_sample.astype(dtype),
            jnp.stack(ks_out, axis=1).astype(dtype), jnp.stack(vs_out, axis=1).astype(dtype),
            jnp.stack(ssd_out, axis=1).astype(dtype), jnp.stack(gdn_out, axis=1).astype(dtype))
```

```python
import functools
import math

import jax
import jax.numpy as jnp
from jax import lax
from jax.experimental import pallas as pl
from jax.experimental.pallas import tpu as pltpu

F32 = jnp.float32
BF16 = jnp.bfloat16
EPS = 1e-6
ROPE_BASE = 10000.0

LANES = 128
HALO = 8
VMEM_LIMIT_BYTES = 56 * 1024 * 1024

N_MOD = 6
MOD_ROWS = 8
SCAN_CHUNK = 256
GDN_CHUNK = 64
GDN_HEADS_PER_STEP = 4
GDN_SYSTEM_ROWS = 128
Q_TILE = 256
ROW_TILE = 1024
OUT_ROW_TILE = 512
FF_TILE = 256
FFN_ROW_CHUNK = 256
IN_COL_TILE = 512


def _silu(x):
    return x / (1.0 + jnp.exp(-x))


def _softplus(x):
    return jnp.maximum(x, 0.0) + jnp.log1p(jnp.exp(-jnp.abs(x)))


def _dot(a, b):
    return jnp.dot(a.astype(BF16), b.astype(BF16), preferred_element_type=F32)


def _dot_nt(a, b):
    return lax.dot_general(a.astype(BF16), b.astype(BF16), (((1,), (1,)), ((), ())),
                           preferred_element_type=F32)


def _dot_tn(a, b):
    return lax.dot_general(a.astype(BF16), b.astype(BF16), (((0,), (0,)), ((), ())),
                           preferred_element_type=F32)


def _split3(a):
    hi = a.astype(BF16)
    r1 = a - hi.astype(F32)
    mid = r1.astype(BF16)
    return hi, mid, (r1 - mid.astype(F32)).astype(BF16)


def _expand(a, p):
    d = functools.partial(jnp.dot, preferred_element_type=F32)
    hi, mid, lo = _split3(a)
    return d(hi, p) + d(mid, p) + d(lo, p)


def _mask_sum(m, a):
    d = functools.partial(jnp.dot, preferred_element_type=F32)
    hi, mid, lo = _split3(a)
    return d(m, hi) + d(m, mid) + d(m, lo)


def _rms(x, g):
    ms = jnp.mean(x * x, axis=-1, keepdims=True)
    return x * lax.rsqrt(ms + EPS) * g


def _params(sem):
    return pltpu.CompilerParams(dimension_semantics=sem, vmem_limit_bytes=VMEM_LIMIT_BYTES)


def _conv3_silu(x, xp, xn, w, b, first, last):
    rows = x.shape[0]
    row = lax.broadcasted_iota(jnp.int32, x.shape, 0)
    prev_row = jnp.where(first, 0.0, xp[HALO - 1:HALO, :])
    next_row = jnp.where(last, 0.0, xn[0:1, :])
    x_prev = jnp.where(row == 0, prev_row, pltpu.roll(x, 1, 0))
    x_next = jnp.where(row == rows - 1, next_row, pltpu.roll(x, rows - 1, 0))
    y = x_prev * w[0:1, :] + x * w[1:2, :] + x_next * w[2:3, :] + b
    return _silu(y)


def _mod_kernel(c_ref, w_ref, b_ref, o_ref):
    s = _silu(c_ref[...])
    o_ref[...] = _dot(s, w_ref[...]) + b_ref[...]


def _mod_call(cvec, ada_w, ada_b):
    depth, d, n = ada_w.shape
    tn = d
    return pl.pallas_call(
        _mod_kernel,
        grid=(depth, n // tn),
        in_specs=[pl.BlockSpec((MOD_ROWS, d), lambda l, j: (0, 0)),
                  pl.BlockSpec((None, d, tn), lambda l, j: (l, 0, j)),
                  pl.BlockSpec((None, 1, tn), lambda l, j: (l, 0, j))],
        out_specs=pl.BlockSpec((None, MOD_ROWS, tn), lambda l, j: (l, 0, j)),
        out_shape=jax.ShapeDtypeStruct((depth, MOD_ROWS, n), F32),
        compiler_params=_params(("arbitrary", "arbitrary")),
        name="ada_mod",
    )(cvec, ada_w, ada_b.reshape(depth, 1, n))


class _Geom:
    def __init__(self, n_ctx, s_ctx, n_lat, s_lat):
        self.n_ctx, self.s_ctx, self.n_lat, self.s_lat = n_ctx, s_ctx, n_lat, s_lat
        self.ctx_rows = n_ctx * s_ctx
        self.rows = self.ctx_rows + n_lat * s_lat

    def row_tile(self, want):
        t = want
        while self.ctx_rows % t or self.s_lat % t:
            t //= 2
        return t

    def scan_chunk(self):
        return min(SCAN_CHUNK, self.s_ctx, self.s_lat)

    def chunk_pos(self, t, q, fwd):
        ctx_blocks = self.ctx_rows // q
        nc_ctx, nc_lat = self.s_ctx // q, self.s_lat // q
        is_ctx = t < ctx_blocks
        tl = jnp.maximum(t - ctx_blocks, 0)
        seq = jnp.where(is_ctx, t >> (nc_ctx.bit_length() - 1), self.n_ctx + (tl >> (nc_lat.bit_length() - 1)))
        cpos = jnp.where(is_ctx, t & (nc_ctx - 1), tl & (nc_lat - 1))
        nc = jnp.where(is_ctx, nc_ctx, nc_lat)
        c = cpos if fwd else nc - 1 - cpos
        return seq, cpos, nc, c, t - cpos + c

    def group(self, tile):
        def fn(i):
            r = i * tile
            return jnp.where(r < self.ctx_rows, 0,
                             1 + (jnp.maximum(r - self.ctx_rows, 0) >> (self.s_lat.bit_length() - 1)))
        return fn


def _mod_spec(d, layer, which, group_fn):
    return pl.BlockSpec((None, 1, d), lambda i, *_: ((layer * N_MOD + which) * MOD_ROWS + group_fn(i), 0, 0))


def _inproj_kernel(x_ref, g_ref, sh_ref, sc_ref, w_ref, o_ref, h_ref):
    @pl.when(pl.program_id(1) == 0)
    def _():
        h_ref[...] = (_rms(x_ref[...], g_ref[...]) * (1.0 + sc_ref[...]) + sh_ref[...]).astype(BF16)

    o_ref[...] = jnp.dot(h_ref[...], w_ref[...], preferred_element_type=F32)


def _inproj_call(x, mods, layer, norm_g, w, geom):
    rows, d = x.shape
    n = w.shape[1]
    tm = geom.row_tile(ROW_TILE)
    tn = IN_COL_TILE
    grp = geom.group(tm)
    return pl.pallas_call(
        _inproj_kernel,
        grid=(rows // tm, n // tn),
        in_specs=[pl.BlockSpec((tm, d), lambda i, j: (i, 0)),
                  pl.BlockSpec((1, d), lambda i, j: (0, 0)),
                  _mod_spec(d, layer, 0, grp),
                  _mod_spec(d, layer, 1, grp),
                  pl.BlockSpec((d, tn), lambda i, j: (0, j))],
        out_specs=pl.BlockSpec((tm, tn), lambda i, j: (i, j)),
        out_shape=jax.ShapeDtypeStruct((rows, n), F32),
        scratch_shapes=[pltpu.VMEM((tm, d), BF16)],
        compiler_params=_params(("arbitrary", "arbitrary")),
        name="in_proj",
    )(x, norm_g.reshape(1, d), mods, mods, w)


def _outproj_kernel(split, ctx_tiles, x_ref, gate_ref, *refs):
    n_a = sum(2 if s else 1 for s in split)
    a_refs, w_refs, o_ref = list(refs[:n_a]), refs[n_a:n_a + len(split)], refs[n_a + len(split)]
    is_ctx = pl.program_id(0) < ctx_tiles
    acc = None
    for s, w_ref in zip(split, w_refs):
        a = a_refs.pop(0)[...]
        if s:
            a = jnp.where(is_ctx, a, a_refs.pop(0)[...])
        part = jnp.dot(a, w_ref[...], preferred_element_type=F32)
        acc = part if acc is None else acc + part
    o_ref[...] = x_ref[...] + gate_ref[...] * acc


def _outproj_call(x, mods, layer, acts, ws, geom):
    rows, d = x.shape
    tm = geom.row_tile(OUT_ROW_TILE)
    grp = geom.group(tm)
    ctx_tiles = geom.ctx_rows // tm
    split = tuple(isinstance(a, tuple) for a in acts)
    in_specs = [pl.BlockSpec((tm, d), lambda i: (i, 0)), _mod_spec(d, layer, 2, grp)]
    flat = []
    for a in acts:
        if isinstance(a, tuple):
            in_specs += [pl.BlockSpec((tm, a[0].shape[1]), lambda i: (jnp.minimum(i, ctx_tiles - 1), 0)),
                         pl.BlockSpec((tm, a[1].shape[1]), lambda i: (jnp.maximum(i - ctx_tiles, 0), 0))]
            flat += list(a)
        else:
            in_specs.append(pl.BlockSpec((tm, a.shape[1]), lambda i: (i, 0)))
            flat.append(a)
    in_specs += [pl.BlockSpec(w.shape, lambda i: (0, 0)) for w in ws]
    return pl.pallas_call(
        functools.partial(_outproj_kernel, split, ctx_tiles),
        grid=(rows // tm,),
        in_specs=in_specs,
        out_specs=pl.BlockSpec((tm, d), lambda i: (i, 0)),
        out_shape=jax.ShapeDtypeStruct((rows, d), F32),
        compiler_params=_params(("arbitrary",)),
        name="out_proj",
    )(x, mods, *flat, *ws)


def _ffn_kernel(geom, nf, x_ref, xp_ref, xn_ref, g_ref, sh_ref, sc_ref, gate_ref,
                wg_ref, wv_ref, cwg_ref, cwv_ref, cbg_ref, cbv_ref, wd_ref, o_ref, h_ref, acc_ref, *u_refs):
    i, j = pl.program_id(0), pl.program_id(1)
    tm = x_ref.shape[0]
    bufs = (u_refs[0:2], u_refs[2:4])

    def up(buf):
        h = h_ref[...]
        buf[0][...] = jnp.dot(h, wg_ref[...].astype(BF16), preferred_element_type=F32)
        buf[1][...] = jnp.dot(h, wv_ref[...].astype(BF16), preferred_element_type=F32)

    def down(buf):
        rc = FFN_ROW_CHUNK
        row0 = i * tm
        is_ctx = row0 < geom.ctx_rows
        base = jnp.where(is_ctx, row0, row0 - geom.ctx_rows)
        lseq = jnp.where(is_ctx, geom.s_ctx, geom.s_lat)
        rowi = lax.broadcasted_iota(jnp.int32, (rc, LANES), 0)
        wd = wd_ref[...].astype(BF16)
        for r in range(0, tm, rc):
            at_start = (rowi == 0) & (((base + r) & (lseq - 1)) == 0)
            at_end = (rowi == rc - 1) & (((base + r + rc) & (lseq - 1)) == 0)

            def conv(u_ref, cw_ref, cb_ref):
                cols = []
                for c in range(0, u_ref.shape[1], LANES):
                    cs = slice(c, c + LANES)
                    up_ = jnp.where(at_start, 0.0, u_ref[HALO + r - 1:HALO + r - 1 + rc, cs])
                    un = jnp.where(at_end, 0.0, u_ref[HALO + r + 1:HALO + r + 1 + rc, cs])
                    cols.append(up_ * cw_ref[0:1, cs] + u_ref[HALO + r:HALO + r + rc, cs] * cw_ref[1:2, cs]
                                + un * cw_ref[2:3, cs] + cb_ref[:, cs])
                return jnp.concatenate(cols, axis=1)

            a = _silu(conv(buf[0], cwg_ref, cbg_ref)) * conv(buf[1], cwv_ref, cbv_ref)
            acc_ref[r:r + rc, :] += jnp.dot(a.astype(BF16), wd, preferred_element_type=F32)

    @pl.when(j == 0)
    def _():
        def hn(x):
            return (_rms(x, g_ref[...]) * (1.0 + sc_ref[...]) + sh_ref[...]).astype(BF16)
        h_ref[0:HALO, :] = hn(xp_ref[...])
        h_ref[HALO:HALO + tm, :] = hn(x_ref[...])
        h_ref[HALO + tm:, :] = hn(xn_ref[...])
        acc_ref[...] = jnp.zeros_like(acc_ref)
        up(bufs[0])

    for parity in range(2):
        @pl.when((j > 0) & (j < nf) & (j % 2 == parity))
        def _():
            up(bufs[parity])
            down(bufs[1 - parity])

    @pl.when(j == nf)
    def _():
        down(bufs[(nf - 1) % 2])
        o_ref[...] = x_ref[...] + gate_ref[...] * acc_ref[...]


def _ffn_call(x, mods, layer, norm_g, w_up, conv_w, conv_b, w_down, geom):
    rows, d = x.shape
    dff = w_down.shape[1]
    tm = geom.row_tile(ROW_TILE)
    tf = FF_TILE
    nf = dff // tf
    grp = geom.group(tm)
    nhalo = rows // HALO
    cur = lambda j: jnp.minimum(j, nf - 1)
    prv = lambda j: jnp.maximum(j - 1, 0)
    return pl.pallas_call(
        functools.partial(_ffn_kernel, geom, nf),
        grid=(rows // tm, nf + 1),
        in_specs=[pl.BlockSpec((tm, d), lambda i, j: (i, 0)),
                  pl.BlockSpec((HALO, d), lambda i, j: (jnp.maximum(i * (tm // HALO) - 1, 0), 0)),
                  pl.BlockSpec((HALO, d), lambda i, j: (jnp.minimum((i + 1) * (tm // HALO), nhalo - 1), 0)),
                  pl.BlockSpec((None, 1, d), lambda i, j: (2 * layer + 1, 0, 0)),
                  _mod_spec(d, layer, 3, grp), _mod_spec(d, layer, 4, grp), _mod_spec(d, layer, 5, grp),
                  pl.BlockSpec((None, d, tf), lambda i, j: (layer, 0, cur(j))),
                  pl.BlockSpec((None, d, tf), lambda i, j: (layer, 0, nf + cur(j))),
                  pl.BlockSpec((None, 3, tf), lambda i, j: (layer, 0, prv(j))),
                  pl.BlockSpec((None, 3, tf), lambda i, j: (layer, 0, nf + prv(j))),
                  pl.BlockSpec((None, 1, tf), lambda i, j: (layer, 0, prv(j))),
                  pl.BlockSpec((None, 1, tf), lambda i, j: (layer, 0, nf + prv(j))),
                  pl.BlockSpec((None, tf, d), lambda i, j: (layer, prv(j), 0))],
        out_specs=pl.BlockSpec((tm, d), lambda i, j: (i, 0)),
        out_shape=jax.ShapeDtypeStruct((rows, d), F32),
        scratch_shapes=[pltpu.VMEM((tm + 2 * HALO, d), BF16), pltpu.VMEM((tm, d), F32)]
        + [pltpu.VMEM((tm + 2 * HALO, tf), F32) for _ in range(4)],
        compiler_params=_params(("arbitrary", "arbitrary")),
        name="conv_ffn",
    )(x, x, x, norm_g, mods, mods, mods, w_up, w_up, conv_w, conv_w, conv_b, conv_b, w_down)


def _final_kernel(x_ref, g_ref, o_ref):
    o_ref[...] = _rms(x_ref[...], g_ref[...])


def _final_call(x, g, row0, nrows):
    d = x.shape[1]
    tm = OUT_ROW_TILE
    while nrows % tm or row0 % tm:
        tm //= 2
    return pl.pallas_call(
        _final_kernel,
        grid=(nrows // tm,),
        in_specs=[pl.BlockSpec((tm, d), lambda i: (row0 // tm + i, 0)), pl.BlockSpec((1, d), lambda i: (0, 0))],
        out_specs=pl.BlockSpec((tm, d), lambda i: (i, 0)),
        out_shape=jax.ShapeDtypeStruct((nrows, d), F32),
        compiler_params=_params(("arbitrary",)),
        name="final_norm",
    )(x, g.reshape(1, d))


def _ssd_kernel(fwd, geom, heads, *refs):
    (xs_ref, xsp_ref, xsn_ref, bc_ref, bcp_ref, bcn_ref, dt_ref, cwx_ref, cbx_ref, cwb_ref, cbb_ref,
     dtb_ref, alog_ref, pe_ref, bd_ref, h0_ref) = refs[:16]
    if fwd:
        z_ref, yb_ref, dx_ref, gn_ref, out_ref, st_ref, s_ref, yacc_ref = refs[16:]
    else:
        out_ref, st_ref, s_ref = refs[16:]
    q = xs_ref.shape[0]
    seq, cpos, nc, c, _ = geom.chunk_pos(pl.program_id(0), q, fwd)
    first, last = c == 0, c == nc - 1
    half = LANES // 2

    @pl.when(cpos == 0)
    def _():
        s_ref[...] = jnp.where(seq < geom.n_ctx, 0.0, h0_ref[...])

    xs = _conv3_silu(xs_ref[...], xsp_ref[...], xsn_ref[...], cwx_ref[...], cbx_ref[...], first, last)
    bc = _conv3_silu(bc_ref[...], bcp_ref[...], bcn_ref[...], cwb_ref[...], cbb_ref[...], first, last)
    bm, cm = bc[:, :LANES], bc[:, LANES:]
    dt = _softplus(dt_ref[...] + dtb_ref[...])
    la = dt * (-jnp.exp(alog_ref[...]))
    row = lax.broadcasted_iota(jnp.int32, (q, q), 0)
    col = lax.broadcasted_iota(jnp.int32, (q, q), 1)
    tril, triu = row >= col, row <= col
    cs_p = _mask_sum(tril.astype(F32).astype(BF16), la)
    cs_s = _mask_sum(triu.astype(F32).astype(BF16), la)
    cs = cs_p if fwd else cs_s
    edge = q - 1 if fwd else 0
    ecs_x = _expand(jnp.exp(cs), pe_ref[...])
    wd_x = _expand(dt * jnp.exp(cs[edge:edge + 1, :] - cs), pe_ref[...])
    s_old = s_ref[...]
    y_off = _dot(cm, s_old) * ecs_x
    s_new = (s_old * ecs_x[edge:edge + 1, :] + _dot_tn(bm, xs * wd_x)) * bd_ref[...]
    s_ref[...] = s_new

    @pl.when(cpos == nc - 1)
    def _():
        st_ref[...] = s_new

    if not fwd:
        out_ref[...] = y_off
        return

    log2e = 1.4426950408889634
    c2_p, c2_s = cs_p * log2e, cs_s * log2e
    ldt = jnp.log2(dt)
    rt_p, rt_s = (c2_p - ldt).T, (c2_s - ldt).T
    neg = -jnp.inf
    lane = lax.broadcasted_iota(jnp.int32, (q, LANES), 1)
    lo, hi = lane < half, lane >= half
    gmat = (_dot_nt(jnp.where(lo, cm, 0.0), bm), _dot_nt(jnp.where(hi, cm, 0.0), bm))
    for p in range(heads // 2):
        xp = xs[:, p * LANES:(p + 1) * LANES]
        acc = None
        for hh in range(2):
            h = 2 * p + hh
            hb = heads + h
            lf = jnp.exp2(jnp.where(tril, c2_p[:, h:h + 1] - rt_p[h:h + 1, :], neg))
            lb = jnp.exp2(jnp.where(triu, c2_s[:, hb:hb + 1] - rt_s[hb:hb + 1, :], neg))
            w = gmat[h // (heads // 2)] * (lf + lb)
            part = _dot(w, jnp.where(lo if hh == 0 else hi, xp, 0.0))
            acc = part if acc is None else acc + part
        yacc_ref[:, p * LANES:(p + 1) * LANES] = acc
    y = yacc_ref[...] + y_off + yb_ref[...] + dx_ref[...] * xs
    yz = y * _silu(z_ref[...])
    out_ref[...] = _rms(yz, gn_ref[...]).astype(BF16)


def _ssd_call(fwd, proj, yb, prm, h0, geom, heads, width):
    rows = proj.shape[0]
    q = geom.scan_chunk()
    nhalo = rows // HALO
    bc_w = 2 * LANES
    xs_blk, z_blk, bc_blk, dt_blk = 0, 1, 5 * width // bc_w, (5 * width + bc_w) // LANES

    def seq(t):
        return geom.chunk_pos(t, q, fwd)[0]

    def rowblk(t):
        return geom.chunk_pos(t, q, fwd)[4]

    def prev(t):
        return jnp.maximum(rowblk(t) * (q // HALO) - 1, 0)

    def nxt(t):
        return jnp.minimum((rowblk(t) + 1) * (q // HALO), nhalo - 1)

    const = lambda t: (0, 0)
    in_specs = [pl.BlockSpec((q, width), lambda t: (rowblk(t), xs_blk)),
                pl.BlockSpec((HALO, width), lambda t: (prev(t), xs_blk)),
                pl.BlockSpec((HALO, width), lambda t: (nxt(t), xs_blk)),
                pl.BlockSpec((q, bc_w), lambda t: (rowblk(t), bc_blk)),
                pl.BlockSpec((HALO, bc_w), lambda t: (prev(t), bc_blk)),
                pl.BlockSpec((HALO, bc_w), lambda t: (nxt(t), bc_blk)),
                pl.BlockSpec((q, LANES), lambda t: (rowblk(t), dt_blk)),
                pl.BlockSpec((3, width), const), pl.BlockSpec((1, width), const),
                pl.BlockSpec((3, bc_w), const), pl.BlockSpec((1, bc_w), const),
                pl.BlockSpec((1, LANES), const), pl.BlockSpec((1, LANES), const),
                pl.BlockSpec((LANES, width), const), pl.BlockSpec((LANES, width), const),
                pl.BlockSpec((None, LANES, width), lambda t: (jnp.maximum(seq(t) - geom.n_ctx, 0), 0, 0))]
    args = [proj, proj, proj, proj, proj, proj, proj, prm["cwx"], prm["cbx"], prm["cwb"], prm["cbb"],
            prm["dtb"], prm["alog"], prm["pf"] if fwd else prm["pb"], prm["bd"], h0]
    scratch = [pltpu.VMEM((LANES, width), F32)]
    if fwd:
        in_specs += [pl.BlockSpec((q, width), lambda t: (rowblk(t), z_blk)),
                     pl.BlockSpec((q, width), lambda t: (rowblk(t), 0)),
                     pl.BlockSpec((1, width), const), pl.BlockSpec((1, width), const)]
        args += [proj, yb, prm["dx"], prm["gn"]]
        scratch.append(pltpu.VMEM((q, width), F32))
        out_dtype = BF16
    else:
        out_dtype = F32
    return pl.pallas_call(
        functools.partial(_ssd_kernel, fwd, geom, heads),
        grid=(rows // q,),
        in_specs=in_specs,
        out_specs=[pl.BlockSpec((q, width), lambda t: (rowblk(t), 0)),
                   pl.BlockSpec((None, LANES, width), lambda t: (seq(t), 0, 0))],
        out_shape=[jax.ShapeDtypeStruct((rows, width), out_dtype),
                   jax.ShapeDtypeStruct((geom.n_ctx + geom.n_lat, LANES, width), F32)],
        scratch_shapes=scratch,
        compiler_params=_params(("arbitrary",)),
        name="ssd_fwd" if fwd else "ssd_bwd",
    )(*args)


def _rope(x, cos, sin_signed):
    lane = lax.broadcasted_iota(jnp.int32, x.shape, 1)
    partner = jnp.where((lane & 31) < 16, pltpu.roll(x, LANES - 16, 1), pltpu.roll(x, 16, 1))
    return x * cos + partner * sin_signed


def _attn_kernel(latent, lam_init, *refs):
    if latent:
        (q_ref, k_ref, v_ref, kc_ref, vc_ref, cq_ref, sq_ref, ck_ref, sk_ref, lp_ref, g_ref,
         o_ref, kr_ref, vr_ref) = refs
    else:
        q_ref, k_ref, v_ref, lp_ref, g_ref, o_ref, kr_ref, vr_ref = refs

    @pl.when(pl.program_id(2) == 0)
    def _():
        k = k_ref[...]
        if latent:
            k = _rope(k, ck_ref[...], sk_ref[...])
        kr_ref[...] = k.astype(BF16)
        vr_ref[...] = v_ref[...].astype(BF16)

    q = q_ref[...]
    if latent:
        q = _rope(q, cq_ref[...], sq_ref[...])
    q = q * (float(LANES // 2) ** -0.5)
    lane = lax.broadcasted_iota(jnp.int32, q.shape, 1)
    qs = (jnp.where(lane < LANES // 2, q, 0.0).astype(BF16), jnp.where(lane >= LANES // 2, q, 0.0).astype(BF16))
    keys = [kr_ref[...]] + ([kc_ref[...].astype(BF16)] if latent else [])
    vals = [vr_ref[...]] + ([vc_ref[...].astype(BF16)] if latent else [])
    lp = lp_ref[...]
    lam = (jnp.exp(jnp.sum(lp[0:1] * lp[1:2], axis=1, keepdims=True))
           - jnp.exp(jnp.sum(lp[2:3] * lp[3:4], axis=1, keepdims=True)) + lam_init)
    probs, coef = [], []
    for m in range(2):
        s = [lax.dot_general(qs[m], kk, (((1,), (1,)), ((), ())), preferred_element_type=F32) for kk in keys]
        mx = functools.reduce(jnp.maximum, [jnp.max(t, axis=1, keepdims=True) for t in s])
        p = [jnp.exp(t - mx) for t in s]
        den = functools.reduce(jnp.add, [jnp.sum(t, axis=1, keepdims=True) for t in p])
        probs.append(p)
        coef.append(1.0 / den if m == 0 else lam / den)
    o = None
    for seg in range(len(keys)):
        pd = (probs[0][seg] * coef[0] - probs[1][seg] * coef[1]).astype(BF16)
        part = jnp.dot(pd, vals[seg], preferred_element_type=F32)
        o = part if o is None else o + part
    o_ref[...] = (_rms(o, g_ref[...]) * (1.0 - lam_init)).astype(BF16)


def _attn_ctx_kernel(lam_init, heads, q_ref, k_ref, v_ref, lp_ref, g_ref, o_ref, ko_ref, vo_ref):
    ko_ref[...] = k_ref[...]
    vo_ref[...] = v_ref[...]
    rows = q_ref.shape[0]
    lane = lax.broadcasted_iota(jnp.int32, (rows, LANES), 1)
    lo = lane < LANES // 2
    scale = float(LANES // 2) ** -0.5
    lp = lp_ref[...]
    lam = (jnp.exp(jnp.sum(lp[0:1] * lp[1:2], axis=1, keepdims=True))
           - jnp.exp(jnp.sum(lp[2:3] * lp[3:4], axis=1, keepdims=True)) + lam_init)
    hs = range(heads)
    sls = [slice(h * LANES, (h + 1) * LANES) for h in hs]
    q = [q_ref[:, sl] * scale for sl in sls]
    k = [k_ref[:, sl].astype(BF16) for sl in sls]
    s0 = [_dot_nt(jnp.where(lo, q[h], 0.0), k[h]) for h in hs]
    s1 = [_dot_nt(jnp.where(lo, 0.0, q[h]), k[h]) for h in hs]
    pd = []
    for h in hs:
        p0 = jnp.exp(s0[h] - jnp.max(s0[h], axis=1, keepdims=True))
        p1 = jnp.exp(s1[h] - jnp.max(s1[h], axis=1, keepdims=True))
        c0 = 1.0 / jnp.sum(p0, axis=1, keepdims=True)
        c1 = lam / jnp.sum(p1, axis=1, keepdims=True)
        pd.append(p0 * c0 - p1 * c1)
    o = [_dot(pd[h], v_ref[:, sls[h]]) for h in hs]
    for h in hs:
        o_ref[:, sls[h]] = (_rms(o[h], g_ref[...]) * (1.0 - lam_init)).astype(BF16)


def _attn_ctx_call(lam_init, proj, lam_p, norm_g, n_seq, seq_len, heads, width):
    spec = lambda blk: pl.BlockSpec((seq_len, width), lambda b: (b, blk))
    return pl.pallas_call(
        functools.partial(_attn_ctx_kernel, lam_init, heads),
        grid=(n_seq,),
        in_specs=[spec(2), spec(3), spec(4), pl.BlockSpec(lam_p.shape, lambda b: (0, 0)),
                  pl.BlockSpec((1, LANES), lambda b: (0, 0))],
        out_specs=[pl.BlockSpec((seq_len, width), lambda b: (b, 0))] * 3,
        out_shape=[jax.ShapeDtypeStruct((n_seq * seq_len, width), BF16),
                   jax.ShapeDtypeStruct((n_seq * seq_len, width), F32),
                   jax.ShapeDtypeStruct((n_seq * seq_len, width), F32)],
        compiler_params=_params(("arbitrary",)),
        name="diff_attn_ctx",
    )(proj, proj, proj, lam_p, norm_g.reshape(1, LANES))


def _attn_call(latent, lam_init, proj, cache_k, cache_v, tables, lam_p, norm_g, row0, n_seq, seq_len, heads, width):
    tq = min(Q_TILE, seq_len)
    nq = seq_len // tq
    hpw = width // LANES
    q_blk, k_blk, v_blk = 2 * hpw, 3 * hpw, 4 * hpw
    qb, sb = row0 // tq, row0 // seq_len
    in_specs = [pl.BlockSpec((tq, LANES), lambda b, h, i: (qb + b * nq + i, q_blk + h)),
                pl.BlockSpec((seq_len, LANES), lambda b, h, i: (sb + b, k_blk + h)),
                pl.BlockSpec((seq_len, LANES), lambda b, h, i: (sb + b, v_blk + h))]
    args = [proj, proj, proj]
    if latent:
        past = cache_k.shape[1]
        in_specs += [pl.BlockSpec((None, past, LANES), lambda b, h, i: (b, 0, h)),
                     pl.BlockSpec((None, past, LANES), lambda b, h, i: (b, 0, h)),
                     pl.BlockSpec((tq, LANES), lambda b, h, i: (i, 0)),
                     pl.BlockSpec((tq, LANES), lambda b, h, i: (i, 0)),
                     pl.BlockSpec((seq_len, LANES), lambda b, h, i: (0, 0)),
                     pl.BlockSpec((seq_len, LANES), lambda b, h, i: (0, 0))]
        args += [cache_k, cache_v, tables[0], tables[1], tables[0], tables[1]]
    in_specs += [pl.BlockSpec(lam_p.shape, lambda b, h, i: (0, 0)), pl.BlockSpec((1, LANES), lambda b, h, i: (0, 0))]
    args += [lam_p, norm_g.reshape(1, LANES)]
    return pl.pallas_call(
        functools.partial(_attn_kernel, latent, lam_init),
        grid=(n_seq, heads, nq),
        in_specs=in_specs,
        out_specs=pl.BlockSpec((tq, LANES), lambda b, h, i: (b * nq + i, h)),
        out_shape=jax.ShapeDtypeStruct((n_seq * seq_len, width), BF16),
        scratch_shapes=[pltpu.VMEM((seq_len, LANES), BF16), pltpu.VMEM((seq_len, LANES), BF16)],
        compiler_params=_params(("arbitrary", "arbitrary", "arbitrary")),
        name="diff_attn_lat" if latent else "diff_attn_ctx",
    )(*args)


def _rope_tables(n_tokens, grid_w):
    n_freq = LANES // 8
    pos = jnp.arange(n_tokens)
    r = (pos // grid_w).astype(F32)
    cpos = (pos % grid_w).astype(F32)
    inv = ROPE_BASE ** (-jnp.arange(n_freq, dtype=F32) / n_freq)
    ang_r, ang_c = r[:, None] * inv, cpos[:, None] * inv
    cos32 = lambda a: jnp.concatenate([jnp.cos(a), jnp.cos(a)], axis=1)
    sin32 = lambda a: jnp.concatenate([-jnp.sin(a), jnp.sin(a)], axis=1)
    cos64 = jnp.concatenate([cos32(ang_r), cos32(ang_c)], axis=1)
    sin64 = jnp.concatenate([sin32(ang_r), sin32(ang_c)], axis=1)
    return jnp.concatenate([cos64, cos64], axis=1), jnp.concatenate([sin64, sin64], axis=1)


def _gdn_masks(rb, sb, fwd):
    shift = GDN_CHUNK.bit_length() - 1

    def tri(n):
        row = jnp.arange(n)[:, None]
        col = jnp.arange(n)[None, :]
        blk = (row >> shift) == (col >> shift)
        return row, col, blk, blk & ((row >= col) if fwd else (row <= col))

    row, col, blk, incl = tri(sb)
    strict = blk & ((row > col) if fwd else (row < col))
    ms = [incl, strict, row == col, (row >> 1) == (col >> 1)]
    for lvl in range(1, shift):
        ms.append(((row >> (lvl + 1)) == (col >> (lvl + 1))) & ((row >> lvl) != (col >> lvl)))
    _, _, blk_rb, incl_rb = tri(rb)
    return jnp.stack(ms).astype(F32), jnp.stack([incl_rb, blk_rb]).astype(BF16)


def _gdn_kernel(fwd, geom, heads, hps, *refs):
    (q_ref, qp_ref, qn_ref, k_ref, kp_ref, kn_ref, v_ref, vp_ref, vn_ref, ab_ref,
     cwq_ref, cbq_ref, cwk_ref, cbk_ref, cwv_ref, cbv_ref, alog_ref, dtb_ref, mf_ref, mb_ref, s0_ref) = refs[:21]
    if fwd:
        z_ref, ob_ref, gn_ref, out_ref, st_ref, s_ref = refs[21:]
    else:
        out_ref, st_ref, s_ref = refs[21:]
    hg = pl.program_id(0)
    rb = q_ref.shape[0]
    seq, cpos, nb, c, _ = geom.chunk_pos(pl.program_id(1), rb, fwd)
    first, last = c == 0, c == nb - 1
    dk = q_ref.shape[1] // hps
    ch = GDN_CHUNK
    nsub = rb // ch
    n_lvl = ch.bit_length() - 1

    @pl.when(cpos == 0)
    def _():
        s_ref[...] = jnp.where(seq < geom.n_ctx, 0.0, s0_ref[...])

    ab = ab_ref[...]
    d0 = 0 if fwd else 2 * heads
    g_all = -jnp.exp(alog_ref[...]) * _softplus(ab + dtb_ref[...])
    b_all = 1.0 / (1.0 + jnp.exp(-ab))
    gc_all = _mask_sum(mb_ref[0], g_all)
    gt_all = _mask_sum(mb_ref[1], g_all)
    gct_all = gc_all.T
    lane = lax.broadcasted_iota(jnp.int32, ab.shape, 1)
    sub = lax.broadcasted_iota(jnp.int32, gct_all.shape, 0)
    incl_f, strict_f, eye, pair = mf_ref[0], mf_ref[1], mf_ref[2], mf_ref[3]
    sb = mf_ref.shape[1]
    cps = sb // ch

    def pick(a, l):
        return jnp.sum(jnp.where(lane == l, a, 0.0), axis=1, keepdims=True)

    hs = range(hps)
    sls = [slice(hh * dk, (hh + 1) * dk) for hh in hs]
    cs = [(hh, blk) for hh in hs for blk in range(rb // sb)]
    nch = range(len(cs))
    kn, qn, vb, gc, gtot, beta, dec = ([] for _ in range(7))
    for hh in hs:
        hd = hg * hps + hh
        sl = sls[hh]

        def conv(x_ref, xp_ref, xn_ref, cw_ref, cb_ref):
            return _conv3_silu(x_ref[:, sl], xp_ref[:, sl], xn_ref[:, sl], cw_ref[:, sl], cb_ref[:, sl], first, last)

        qc = conv(q_ref, qp_ref, qn_ref, cwq_ref, cbq_ref)
        kc = conv(k_ref, kp_ref, kn_ref, cwk_ref, cbk_ref)
        vh = conv(v_ref, vp_ref, vn_ref, cwv_ref, cbv_ref)
        qh = qc * lax.rsqrt(jnp.sum(qc * qc, axis=1, keepdims=True) + EPS) * (float(dk) ** -0.5)
        kh = kc * lax.rsqrt(jnp.sum(kc * kc, axis=1, keepdims=True) + EPS)
        gch, gth = pick(gc_all, d0 + hd), pick(gt_all, d0 + hd)
        bh = pick(b_all, d0 + heads + hd)
        g_row = jnp.sum(jnp.where(sub == d0 + hd, gct_all, 0.0), axis=0, keepdims=True)
        for blk in range(rb // sb):
            r = slice(blk * sb, (blk + 1) * sb)
            qn.append(qh[r]); kn.append(kh[r]); gc.append(gch[r]); gtot.append(gth[r]); beta.append(bh[r])
            vb.append(vh[r] * bh[r])
            dec.append(jnp.exp(jnp.minimum(gch[r] - g_row[:, r], 0.0)) * incl_f)
    kk = [_dot_nt(kn[c], kn[c]) for c in nch]
    m = [(kk[c] * beta[c]) * (dec[c] * strict_f) for c in nch]
    t = [eye - m[c] * pair for c in nch]
    for lvl in range(1, n_lvl):
        off = mf_ref[3 + lvl]
        tm = [_dot(t[c], m[c] * off) for c in nch]
        tmt = [_dot(tm[c], t[c]) for c in nch]
        t = [t[c] - tmt[c] for c in nch]
    eg = [jnp.exp(gc[c]) for c in nch]
    sol = [_dot(t[c], jnp.concatenate([vb[c], kn[c] * (beta[c] * eg[c])], axis=1)) for c in nch]
    qk = [_dot_nt(qn[c], kn[c]) for c in nch]
    qg = [qn[c] * eg[c] for c in nch]
    kg = [kn[c] * jnp.exp(gtot[c] - gc[c]) for c in nch]
    egl = [jnp.exp(gtot[c]) for c in nch]
    s = [s_ref[hh] for hh in hs]
    vnew = [[None] * cps for _ in nch]
    qs = [[None] * cps for _ in nch]
    for i in (range(nsub) if fwd else reversed(range(nsub))):
        blk, li = divmod(i, cps)
        r = slice(li * ch, (li + 1) * ch)
        idx = [hh * (rb // sb) + blk for hh in hs]
        ws = [_dot(jnp.concatenate([sol[c][r, dk:], qg[c][r]], axis=0), s[hh]) for hh, c in zip(hs, idx)]
        for hh, c in zip(hs, idx):
            vnew[c][li] = sol[c][r, :dk] - ws[hh][:ch]
            qs[c][li] = ws[hh][ch:]
        upd = [_dot_tn(kg[c][r], vnew[c][li]) for c in idx]
        s = [s[hh] * egl[c][li * ch:li * ch + 1, :] + upd[hh] for hh, c in zip(hs, idx)]
    intra = [_dot(qk[c] * dec[c], jnp.concatenate(vnew[c], axis=0)) for c in nch]
    for c, (hh, blk) in enumerate(cs):
        sl = sls[hh]
        r = slice(blk * sb, (blk + 1) * sb)
        o = jnp.concatenate(qs[c], axis=0) + intra[c]
        if fwd:
            o = o + ob_ref[r, sl]
            out_ref[r, sl] = (_rms(o, gn_ref[...]) * _silu(z_ref[r, sl])).astype(BF16)
        else:
            out_ref[r, sl] = o
    for hh in hs:
        s_ref[hh] = s[hh]

    @pl.when(cpos == nb - 1)
    def _():
        st_ref[...] = s_ref[...]


def _gdn_call(fwd, proj, ob, prm, s0, geom, heads, width):
    rows = proj.shape[0]
    rb = geom.scan_chunk()
    nhalo = rows // HALO
    dk = width // heads
    hps = GDN_HEADS_PER_STEP
    hw = hps * dk
    ngrp = heads // hps
    ab_blk = 4 * width // LANES
    mf, mb = _gdn_masks(rb, min(GDN_SYSTEM_ROWS, rb), fwd)

    def seq(t):
        return geom.chunk_pos(t, rb, fwd)[0]

    def rowblk(t):
        return geom.chunk_pos(t, rb, fwd)[4]

    def prev(t):
        return jnp.maximum(rowblk(t) * (rb // HALO) - 1, 0)

    def nxt(t):
        return jnp.minimum((rowblk(t) + 1) * (rb // HALO), nhalo - 1)

    in_specs, args = [], []
    for sec in range(3):
        in_specs += [pl.BlockSpec((rb, hw), lambda h, t, sec=sec: (rowblk(t), sec * ngrp + h)),
                     pl.BlockSpec((HALO, hw), lambda h, t, sec=sec: (prev(t), sec * ngrp + h)),
                     pl.BlockSpec((HALO, hw), lambda h, t, sec=sec: (nxt(t), sec * ngrp + h))]
        args += [proj, proj, proj]
    in_specs.append(pl.BlockSpec((rb, LANES), lambda h, t: (rowblk(t), ab_blk)))
    args.append(proj)
    for sec in range(3):
        in_specs += [pl.BlockSpec((3, hw), lambda h, t, sec=sec: (0, sec * ngrp + h)),
                     pl.BlockSpec((1, hw), lambda h, t, sec=sec: (0, sec * ngrp + h))]
        args += [prm["cw"], prm["cb"]]
    in_specs += [pl.BlockSpec((1, LANES), lambda h, t: (0, 0)), pl.BlockSpec((1, LANES), lambda h, t: (0, 0)),
                 pl.BlockSpec(mf.shape, lambda h, t: (0, 0, 0)), pl.BlockSpec(mb.shape, lambda h, t: (0, 0, 0)),
                 pl.BlockSpec((None, hps, dk, dk), lambda h, t: (jnp.maximum(seq(t) - geom.n_ctx, 0), h, 0, 0))]
    args += [prm["alog"], prm["dtb"], mf, mb, s0]
    if fwd:
        in_specs += [pl.BlockSpec((rb, hw), lambda h, t: (rowblk(t), 3 * ngrp + h)),
                     pl.BlockSpec((rb, hw), lambda h, t: (rowblk(t), h)),
                     pl.BlockSpec((1, dk), lambda h, t: (0, 0))]
        args += [proj, ob, prm["gn"]]
    return pl.pallas_call(
        functools.partial(_gdn_kernel, fwd, geom, heads, hps),
        grid=(ngrp, rows // rb),
        in_specs=in_specs,
        out_specs=[pl.BlockSpec((rb, hw), lambda h, t: (rowblk(t), h)),
                   pl.BlockSpec((None, hps, dk, dk), lambda h, t: (seq(t), h, 0, 0))],
        out_shape=[jax.ShapeDtypeStruct((rows, width), BF16 if fwd else F32),
                   jax.ShapeDtypeStruct((geom.n_ctx + geom.n_lat, heads, dk, dk), F32)],
        scratch_shapes=[pltpu.VMEM((hps, dk, dk), F32)],
        compiler_params=_params(("arbitrary", "arbitrary")),
        name="gdn_fwd" if fwd else "gdn_bwd",
    )(*args)


def _pad_cols(w, n):
    return jnp.pad(w, ((0, 0), (0, n - w.shape[1])))


def _lane_row(v):
    v = v.reshape(1, -1).astype(F32)
    return _pad_cols(v, LANES)


def _even_layer(x, mods, layer, e, geom, norm_g, w_in, conv_w, conv_b, dt_bias, a_log, d_skip, ssd_g,
                lam_p, da_g, w_out, cache_k, cache_v, state_ssd, tables):
    d = x.shape[1]
    heads, p_dim, n_state = state_ssd.shape[3], state_ssd.shape[4], state_ssd.shape[5]
    width = heads * p_dim
    da_heads = cache_k.shape[3]
    assert width == d and da_heads * LANES == d and 2 * n_state == LANES and 2 * p_dim == LANES
    bc_w = 2 * LANES
    n_dt = 2 * heads
    o_z, o_xs, o_bc, o_dt = 0, width, 2 * width, 2 * width + bc_w
    o_q = o_dt + n_dt
    cols = [w_in[:, o_xs:o_xs + width], w_in[:, o_z:o_z + width], w_in[:, o_q:o_q + 3 * d],
            w_in[:, o_bc:o_bc + bc_w], w_in[:, o_dt:o_dt + n_dt]]
    n_used = 5 * width + bc_w + n_dt
    n_pad = -(-n_used // IN_COL_TILE) * IN_COL_TILE
    w_perm = _pad_cols(jnp.concatenate(cols, axis=1), n_pad).astype(BF16)
    proj = _inproj_call(x, mods, layer, norm_g, w_perm, geom)

    hl = jnp.arange(LANES)[:, None]
    cl = jnp.arange(width)[None, :]
    prm = {
        "cwx": conv_w[:, :width], "cbx": conv_b[None, :width],
        "cwb": conv_w[:, width:], "cbb": conv_b[None, width:],
        "dtb": _lane_row(dt_bias), "alog": _lane_row(a_log),
        "pf": (hl == cl // p_dim).astype(BF16), "pb": (hl == heads + cl // p_dim).astype(BF16),
        "bd": (hl // n_state == cl // (width // 2)).astype(F32),
        "dx": jnp.repeat(d_skip, p_dim)[None, :], "gn": ssd_g[None, :],
    }

    def state_in(st):
        t = st.transpose(0, 3, 1, 2).reshape(st.shape[0], n_state, width)
        return jnp.concatenate([t, t], axis=1) * prm["bd"]

    def state_out(st):
        per_h = st.reshape(st.shape[0], 2, n_state, heads, p_dim)
        sel = jnp.concatenate([per_h[:, 0, :, :heads // 2], per_h[:, 1, :, heads // 2:]], axis=2)
        return sel.transpose(0, 2, 3, 1)

    h0f, h0b = state_in(state_ssd[:, e, 0].astype(F32)), state_in(state_ssd[:, e, 1].astype(F32))
    yb, st_b = _ssd_call(False, proj, None, prm, h0b, geom, heads, width)
    y, st_f = _ssd_call(True, proj, yb, prm, h0f, geom, heads, width)
    new_states = jnp.stack([state_out(st_f[:geom.n_ctx]), state_out(st_b[:geom.n_ctx])], axis=1)
    lam_init = 0.8 - 0.6 * math.exp(-0.3 * layer)
    o_ctx, k_ctx, v_ctx = _attn_ctx_call(lam_init, proj, lam_p, da_g, geom.n_ctx, geom.s_ctx, da_heads, d)
    o_lat = _attn_call(True, lam_init, proj, cache_k[:, e].reshape(geom.n_lat, -1, d),
                       cache_v[:, e].reshape(geom.n_lat, -1, d), tables, lam_p, da_g, geom.ctx_rows,
                       geom.n_lat, geom.s_lat, da_heads, d)
    w_o = w_out.astype(BF16)
    x = _outproj_call(x, mods, layer, [y, (o_ctx, o_lat)], [w_o[:width], w_o[width:]], geom)
    k_new = k_ctx.reshape(geom.n_ctx, geom.s_ctx, da_heads, LANES)
    v_new = v_ctx.reshape(geom.n_ctx, geom.s_ctx, da_heads, LANES)
    return x, k_new, v_new, new_states


def _odd_layer(x, mods, layer, o, geom, norm_g, w_in, conv_w, conv_b, a_log, dt_bias, g_norm, w_out, state_gdn):
    d = x.shape[1]
    heads, dk, dv = state_gdn.shape[3], state_gdn.shape[4], state_gdn.shape[5]
    width = heads * dv
    assert width == d and dk == LANES and dv == LANES and 4 * heads <= LANES
    n_used = 4 * width + 4 * heads
    n_pad = -(-n_used // IN_COL_TILE) * IN_COL_TILE
    w_perm = _pad_cols(w_in, n_pad).astype(BF16)
    proj = _inproj_call(x, mods, layer, norm_g, w_perm, geom)

    def ab_row(v):
        r = jnp.zeros((2, 2 * heads), F32).at[:, :heads].set(v.astype(F32))
        return _lane_row(r)

    prm = {"cw": conv_w, "cb": conv_b[None, :], "alog": ab_row(a_log), "dtb": ab_row(dt_bias), "gn": g_norm[None, :]}
    s0f, s0b = state_gdn[:, o, 0].astype(F32), state_gdn[:, o, 1].astype(F32)
    ob, st_b = _gdn_call(False, proj, None, prm, s0b, geom, heads, width)
    og, st_f = _gdn_call(True, proj, ob, prm, s0f, geom, heads, width)
    new_states = jnp.stack([st_f[:geom.n_ctx], st_b[:geom.n_ctx]], axis=1)
    x = _outproj_call(x, mods, layer, [og], [w_out.astype(BF16)], geom)
    return x, new_states


def kernel(x_prompt, x_sample, cache_attn_k, cache_attn_v, state_ssd, state_gdn, c, c_ctx, ada_w, ada_b, norm_g, ev_w_in, ev_conv_w, ev_conv_b, ssd_dt_bias, ssd_a_log, ssd_d, ssd_norm_g, da_lambda, da_norm_g, ev_w_out, od_w_in, od_conv_w, od_conv_b, gdn_a_log, gdn_dt_bias, gdn_norm_g, od_w_out, ffn_w_up, ffn_conv_w, ffn_conv_b, ffn_w_down, final_norm_g):
    n_ctx, s_ctx, d = x_prompt.shape
    n_lat, s_lat, _ = x_sample.shape
    depth = ada_w.shape[0]
    assert n_lat + 1 <= MOD_ROWS and s_ctx & (s_ctx - 1) == 0 and s_lat & (s_lat - 1) == 0
    geom = _Geom(n_ctx, s_ctx, n_lat, s_lat)
    dtype = x_prompt.dtype
    x = jnp.concatenate([x_prompt.reshape(-1, d), x_sample.reshape(-1, d)], axis=0).astype(F32)

    cvec = jnp.zeros((MOD_ROWS, d), F32).at[0].set(c_ctx).at[1:1 + n_lat].set(c)
    mods = _mod_call(cvec, ada_w, ada_b)
    mods = mods.reshape(depth, MOD_ROWS, N_MOD, d).transpose(0, 2, 1, 3).reshape(depth * N_MOD * MOD_ROWS, 1, d)

    grid_w = 64
    tables = _rope_tables(s_lat, grid_w)
    ks_out, vs_out, ssd_out, gdn_out = [], [], [], []
    for l in range(depth):
        if l % 2 == 0:
            e = l // 2
            x, k_new, v_new, st = _even_layer(
                x, mods, l, e, geom, norm_g[l, 0], ev_w_in[e], ev_conv_w[e], ev_conv_b[e], ssd_dt_bias[e],
                ssd_a_log[e], ssd_d[e], ssd_norm_g[e], da_lambda[e], da_norm_g[e], ev_w_out[e],
                cache_attn_k, cache_attn_v, state_ssd, tables)
            ks_out.append(k_new)
            vs_out.append(v_new)
            ssd_out.append(st)
        else:
            o = l // 2
            x, st = _odd_layer(x, mods, l, o, geom, norm_g[l, 0], od_w_in[o], od_conv_w[o], od_conv_b[o],
                               gdn_a_log[o], gdn_dt_bias[o], gdn_norm_g[o], od_w_out[o], state_gdn)
            gdn_out.append(st)
        x = _ffn_call(x, mods, l, norm_g.reshape(2 * depth, 1, d), ffn_w_up, ffn_conv_w,
                      ffn_conv_b.reshape(depth, 1, -1), ffn_w_down, geom)
    y_prompt = _final_call(x, final_norm_g, 0, geom.ctx_rows).reshape(n_ctx, s_ctx, d)
    y_sample = _final_call(x, final_norm_g, geom.ctx_rows, n_lat * s_lat).reshape(n_lat, s_lat, d)
    return (y_prompt.astype(dtype), y_sample.astype(dtype),
            jnp.stack(ks_out, axis=1).astype(dtype), jnp.stack(vs_out, axis=1).astype(dtype),
            jnp.stack(ssd_out, axis=1).astype(dtype), jnp.stack(gdn_out, axis=1).astype(dtype))
```

```python
import functools
import math

import jax
import jax.numpy as jnp
from jax import lax
from jax.experimental import pallas as pl
from jax.experimental.pallas import tpu as pltpu

F32 = jnp.float32
BF16 = jnp.bfloat16
EPS = 1e-6
ROPE_BASE = 10000.0
LOG2E = 1.4426950408889634

LANES = 128
HALO = 8
VMEM_LIMIT_BYTES = 56 * 1024 * 1024

N_MOD = 6
MOD_ROWS = 8
SCAN_CHUNK = 256
GDN_CHUNK = 64
GDN_HEADS_PER_STEP = 8
GDN_SYSTEM_ROWS = 128
Q_TILE = 256
ROW_TILE = 1024
OUT_ROW_TILE = 512
FF_TILE = 256
FFN_ROW_CHUNK = 256
IN_COL_TILE = 512


def _silu(x):
    return x / (1.0 + jnp.exp(-x))


def _softplus(x):
    return jnp.maximum(x, 0.0) + jnp.log1p(jnp.exp(-jnp.abs(x)))


def _dot(a, b):
    return jnp.dot(a.astype(BF16), b.astype(BF16), preferred_element_type=F32)


def _dot_nt(a, b):
    return lax.dot_general(a.astype(BF16), b.astype(BF16), (((1,), (1,)), ((), ())),
                           preferred_element_type=F32)


def _dot_tn(a, b):
    return lax.dot_general(a.astype(BF16), b.astype(BF16), (((0,), (0,)), ((), ())),
                           preferred_element_type=F32)


def _split3(a):
    hi = a.astype(BF16)
    r1 = a - hi.astype(F32)
    mid = r1.astype(BF16)
    return hi, mid, (r1 - mid.astype(F32)).astype(BF16)


def _expand(a, p, pieces=3):
    d = functools.partial(jnp.dot, preferred_element_type=F32)
    hi, mid, lo = _split3(a)
    out = d(hi, p) + d(mid, p)
    return out + d(lo, p) if pieces == 3 else out


def _mask_sum(m, a):
    d = functools.partial(jnp.dot, preferred_element_type=F32)
    hi, mid, lo = _split3(a)
    return d(m, hi) + d(m, mid) + d(m, lo)


def _rms(x, g):
    ms = jnp.mean(x * x, axis=-1, keepdims=True)
    return x * lax.rsqrt(ms + EPS) * g


def _params(sem):
    return pltpu.CompilerParams(dimension_semantics=sem, vmem_limit_bytes=VMEM_LIMIT_BYTES)


def _conv3_silu(x, xp, xn, w, b, first, last):
    rows = x.shape[0]
    row = lax.broadcasted_iota(jnp.int32, x.shape, 0)
    prev_row = jnp.where(first, 0.0, xp[HALO - 1:HALO, :])
    next_row = jnp.where(last, 0.0, xn[0:1, :])
    x_prev = jnp.where(row == 0, prev_row, pltpu.roll(x, 1, 0))
    x_next = jnp.where(row == rows - 1, next_row, pltpu.roll(x, rows - 1, 0))
    y = x_prev * w[0:1, :] + x * w[1:2, :] + x_next * w[2:3, :] + b
    return _silu(y)


def _mod_kernel(c_ref, w_ref, b_ref, o_ref):
    s = _silu(c_ref[...])
    o_ref[...] = _dot(s, w_ref[...]) + b_ref[...]


def _mod_call(cvec, ada_w, ada_b):
    depth, d, n = ada_w.shape
    tn = d
    return pl.pallas_call(
        _mod_kernel,
        grid=(depth, n // tn),
        in_specs=[pl.BlockSpec((MOD_ROWS, d), lambda l, j: (0, 0)),
                  pl.BlockSpec((None, d, tn), lambda l, j: (l, 0, j)),
                  pl.BlockSpec((None, 1, tn), lambda l, j: (l, 0, j))],
        out_specs=pl.BlockSpec((None, MOD_ROWS, tn), lambda l, j: (l, 0, j)),
        out_shape=jax.ShapeDtypeStruct((depth, MOD_ROWS, n), F32),
        compiler_params=_params(("arbitrary", "arbitrary")),
        name="ada_mod",
    )(cvec, ada_w, ada_b.reshape(depth, 1, n))


class _Geom:
    def __init__(self, n_ctx, s_ctx, n_lat, s_lat):
        self.n_ctx, self.s_ctx, self.n_lat, self.s_lat = n_ctx, s_ctx, n_lat, s_lat
        self.ctx_rows = n_ctx * s_ctx
        self.rows = self.ctx_rows + n_lat * s_lat

    def row_tile(self, want):
        t = want
        while self.ctx_rows % t or self.s_lat % t:
            t //= 2
        return t

    def scan_chunk(self):
        return min(SCAN_CHUNK, self.s_ctx, self.s_lat)

    def chunk_pos(self, t, q, fwd):
        ctx_blocks = self.ctx_rows // q
        nc_ctx, nc_lat = self.s_ctx // q, self.s_lat // q
        is_ctx = t < ctx_blocks
        tl = jnp.maximum(t - ctx_blocks, 0)
        seq = jnp.where(is_ctx, t >> (nc_ctx.bit_length() - 1), self.n_ctx + (tl >> (nc_lat.bit_length() - 1)))
        cpos = jnp.where(is_ctx, t & (nc_ctx - 1), tl & (nc_lat - 1))
        nc = jnp.where(is_ctx, nc_ctx, nc_lat)
        c = cpos if fwd else nc - 1 - cpos
        return seq, cpos, nc, c, t - cpos + c

    def group(self, tile):
        def fn(i):
            r = i * tile
            return jnp.where(r < self.ctx_rows, 0,
                             1 + (jnp.maximum(r - self.ctx_rows, 0) >> (self.s_lat.bit_length() - 1)))
        return fn


def _mod_spec(d, layer, which, group_fn):
    return pl.BlockSpec((None, 1, d), lambda i, *_: ((layer * N_MOD + which) * MOD_ROWS + group_fn(i), 0, 0))


def _inproj_kernel(x_ref, g_ref, sh_ref, sc_ref, w_ref, o_ref, h_ref):
    @pl.when(pl.program_id(1) == 0)
    def _():
        h_ref[...] = (_rms(x_ref[...], g_ref[...]) * (1.0 + sc_ref[...]) + sh_ref[...]).astype(BF16)

    o_ref[...] = jnp.dot(h_ref[...], w_ref[...], preferred_element_type=F32)


def _inproj_call(x, mods, layer, norm_g, w, geom):
    rows, d = x.shape
    n = w.shape[1]
    tm = geom.row_tile(ROW_TILE)
    tn = IN_COL_TILE
    grp = geom.group(tm)
    return pl.pallas_call(
        _inproj_kernel,
        grid=(rows // tm, n // tn),
        in_specs=[pl.BlockSpec((tm, d), lambda i, j: (i, 0)),
                  pl.BlockSpec((1, d), lambda i, j: (0, 0)),
                  _mod_spec(d, layer, 0, grp),
                  _mod_spec(d, layer, 1, grp),
                  pl.BlockSpec((d, tn), lambda i, j: (0, j))],
        out_specs=pl.BlockSpec((tm, tn), lambda i, j: (i, j)),
        out_shape=jax.ShapeDtypeStruct((rows, n), F32),
        scratch_shapes=[pltpu.VMEM((tm, d), BF16)],
        compiler_params=_params(("arbitrary", "arbitrary")),
        name="in_proj",
    )(x, norm_g.reshape(1, d), mods, mods, w)


def _outproj_kernel(split, ctx_tiles, x_ref, gate_ref, *refs):
    n_a = sum(2 if s else 1 for s in split)
    a_refs, w_refs, o_ref = list(refs[:n_a]), refs[n_a:n_a + len(split)], refs[n_a + len(split)]
    is_ctx = pl.program_id(0) < ctx_tiles
    acc = None
    for s, w_ref in zip(split, w_refs):
        a = a_refs.pop(0)[...]
        if s:
            a = jnp.where(is_ctx, a, a_refs.pop(0)[...])
        part = jnp.dot(a, w_ref[...], preferred_element_type=F32)
        acc = part if acc is None else acc + part
    o_ref[...] = x_ref[...] + gate_ref[...] * acc


def _outproj_call(x, mods, layer, acts, ws, geom):
    rows, d = x.shape
    tm = geom.row_tile(OUT_ROW_TILE)
    grp = geom.group(tm)
    ctx_tiles = geom.ctx_rows // tm
    split = tuple(isinstance(a, tuple) for a in acts)
    in_specs = [pl.BlockSpec((tm, d), lambda i: (i, 0)), _mod_spec(d, layer, 2, grp)]
    flat = []
    for a in acts:
        if isinstance(a, tuple):
            in_specs += [pl.BlockSpec((tm, a[0].shape[1]), lambda i: (jnp.minimum(i, ctx_tiles - 1), 0)),
                         pl.BlockSpec((tm, a[1].shape[1]), lambda i: (jnp.maximum(i - ctx_tiles, 0), 0))]
            flat += list(a)
        else:
            in_specs.append(pl.BlockSpec((tm, a.shape[1]), lambda i: (i, 0)))
            flat.append(a)
    in_specs += [pl.BlockSpec(w.shape, lambda i: (0, 0)) for w in ws]
    return pl.pallas_call(
        functools.partial(_outproj_kernel, split, ctx_tiles),
        grid=(rows // tm,),
        in_specs=in_specs,
        out_specs=pl.BlockSpec((tm, d), lambda i: (i, 0)),
        out_shape=jax.ShapeDtypeStruct((rows, d), F32),
        compiler_params=_params(("arbitrary",)),
        name="out_proj",
    )(x, mods, *flat, *ws)


def _ffn_kernel(geom, nf, x_ref, xp_ref, xn_ref, g_ref, sh_ref, sc_ref, gate_ref,
                wg_ref, wv_ref, cwg_ref, cwv_ref, cbg_ref, cbv_ref, wd_ref, o_ref, h_ref, acc_ref, *u_refs):
    i, j = pl.program_id(0), pl.program_id(1)
    tm = x_ref.shape[0]
    bufs = (u_refs[0:2], u_refs[2:4])

    def up(buf):
        h = h_ref[...]
        buf[0][...] = jnp.dot(h, wg_ref[...].astype(BF16), preferred_element_type=F32)
        buf[1][...] = jnp.dot(h, wv_ref[...].astype(BF16), preferred_element_type=F32)

    def down(buf):
        rc = FFN_ROW_CHUNK
        row0 = i * tm
        is_ctx = row0 < geom.ctx_rows
        base = jnp.where(is_ctx, row0, row0 - geom.ctx_rows)
        lseq = jnp.where(is_ctx, geom.s_ctx, geom.s_lat)
        rowi = lax.broadcasted_iota(jnp.int32, (rc, LANES), 0)
        wd = wd_ref[...].astype(BF16)
        for r in range(0, tm, rc):
            at_start = (rowi == 0) & (((base + r) & (lseq - 1)) == 0)
            at_end = (rowi == rc - 1) & (((base + r + rc) & (lseq - 1)) == 0)

            def conv(u_ref, cw_ref, cb_ref):
                cols = []
                for c in range(0, u_ref.shape[1], LANES):
                    cs = slice(c, c + LANES)
                    up_ = jnp.where(at_start, 0.0, u_ref[HALO + r - 1:HALO + r - 1 + rc, cs])
                    un = jnp.where(at_end, 0.0, u_ref[HALO + r + 1:HALO + r + 1 + rc, cs])
                    cols.append(up_ * cw_ref[0:1, cs] + u_ref[HALO + r:HALO + r + rc, cs] * cw_ref[1:2, cs]
                                + un * cw_ref[2:3, cs] + cb_ref[:, cs])
                return jnp.concatenate(cols, axis=1)

            a = _silu(conv(buf[0], cwg_ref, cbg_ref)) * conv(buf[1], cwv_ref, cbv_ref)
            acc_ref[r:r + rc, :] += jnp.dot(a.astype(BF16), wd, preferred_element_type=F32)

    @pl.when(j == 0)
    def _():
        def hn(x):
            return (_rms(x, g_ref[...]) * (1.0 + sc_ref[...]) + sh_ref[...]).astype(BF16)
        h_ref[0:HALO, :] = hn(xp_ref[...])
        h_ref[HALO:HALO + tm, :] = hn(x_ref[...])
        h_ref[HALO + tm:, :] = hn(xn_ref[...])
        acc_ref[...] = jnp.zeros_like(acc_ref)
        up(bufs[0])

    for parity in range(2):
        @pl.when((j > 0) & (j < nf) & (j % 2 == parity))
        def _():
            up(bufs[parity])
            down(bufs[1 - parity])

    @pl.when(j == nf)
    def _():
        down(bufs[(nf - 1) % 2])
        o_ref[...] = x_ref[...] + gate_ref[...] * acc_ref[...]


def _ffn_call(x, mods, layer, norm_g, w_up, conv_w, conv_b, w_down, geom):
    rows, d = x.shape
    dff = w_down.shape[1]
    tm = geom.row_tile(ROW_TILE)
    tf = FF_TILE
    nf = dff // tf
    grp = geom.group(tm)
    nhalo = rows // HALO
    cur = lambda j: jnp.minimum(j, nf - 1)
    prv = lambda j: jnp.maximum(j - 1, 0)
    return pl.pallas_call(
        functools.partial(_ffn_kernel, geom, nf),
        grid=(rows // tm, nf + 1),
        in_specs=[pl.BlockSpec((tm, d), lambda i, j: (i, 0)),
                  pl.BlockSpec((HALO, d), lambda i, j: (jnp.maximum(i * (tm // HALO) - 1, 0), 0)),
                  pl.BlockSpec((HALO, d), lambda i, j: (jnp.minimum((i + 1) * (tm // HALO), nhalo - 1), 0)),
                  pl.BlockSpec((None, 1, d), lambda i, j: (2 * layer + 1, 0, 0)),
                  _mod_spec(d, layer, 3, grp), _mod_spec(d, layer, 4, grp), _mod_spec(d, layer, 5, grp),
                  pl.BlockSpec((None, d, tf), lambda i, j: (layer, 0, cur(j))),
                  pl.BlockSpec((None, d, tf), lambda i, j: (layer, 0, nf + cur(j))),
                  pl.BlockSpec((None, 3, tf), lambda i, j: (layer, 0, prv(j))),
                  pl.BlockSpec((None, 3, tf), lambda i, j: (layer, 0, nf + prv(j))),
                  pl.BlockSpec((None, 1, tf), lambda i, j: (layer, 0, prv(j))),
                  pl.BlockSpec((None, 1, tf), lambda i, j: (layer, 0, nf + prv(j))),
                  pl.BlockSpec((None, tf, d), lambda i, j: (layer, prv(j), 0))],
        out_specs=pl.BlockSpec((tm, d), lambda i, j: (i, 0)),
        out_shape=jax.ShapeDtypeStruct((rows, d), F32),
        scratch_shapes=[pltpu.VMEM((tm + 2 * HALO, d), BF16), pltpu.VMEM((tm, d), F32)]
        + [pltpu.VMEM((tm + 2 * HALO, tf), F32) for _ in range(4)],
        compiler_params=_params(("arbitrary", "arbitrary")),
        name="conv_ffn",
    )(x, x, x, norm_g, mods, mods, mods, w_up, w_up, conv_w, conv_w, conv_b, conv_b, w_down)


def _final_kernel(x_ref, g_ref, o_ref):
    o_ref[...] = _rms(x_ref[...], g_ref[...])


def _final_call(x, g, row0, nrows):
    d = x.shape[1]
    tm = OUT_ROW_TILE
    while nrows % tm or row0 % tm:
        tm //= 2
    return pl.pallas_call(
        _final_kernel,
        grid=(nrows // tm,),
        in_specs=[pl.BlockSpec((tm, d), lambda i: (row0 // tm + i, 0)), pl.BlockSpec((1, d), lambda i: (0, 0))],
        out_specs=pl.BlockSpec((tm, d), lambda i: (i, 0)),
        out_shape=jax.ShapeDtypeStruct((nrows, d), F32),
        compiler_params=_params(("arbitrary",)),
        name="final_norm",
    )(x, g.reshape(1, d))


def _ssd_kernel(fwd, geom, heads, *refs):
    (xs_ref, xsp_ref, xsn_ref, bc_ref, bcp_ref, bcn_ref, dt_ref, cwx_ref, cbx_ref, cwb_ref, cbb_ref,
     dtb_ref, alog_ref, pe_ref, bd_ref, h0_ref) = refs[:16]
    if fwd:
        z_ref, yb_ref, dx_ref, gn_ref, out_ref, st_ref, s_ref, yacc_ref = refs[16:]
    else:
        out_ref, st_ref, s_ref = refs[16:]
    q = xs_ref.shape[0]
    seq, cpos, nc, c, _ = geom.chunk_pos(pl.program_id(0), q, fwd)
    first, last = c == 0, c == nc - 1
    half = LANES // 2

    @pl.when(cpos == 0)
    def _():
        s_ref[...] = jnp.where(seq < geom.n_ctx, 0.0, h0_ref[...])

    xs = _conv3_silu(xs_ref[...], xsp_ref[...], xsn_ref[...], cwx_ref[...], cbx_ref[...], first, last)
    bc = _conv3_silu(bc_ref[...], bcp_ref[...], bcn_ref[...], cwb_ref[...], cbb_ref[...], first, last)
    bm, cm = bc[:, :LANES], bc[:, LANES:]
    dt = _softplus(dt_ref[...] + dtb_ref[...])
    la = dt * (-jnp.exp(alog_ref[...]))
    row = lax.broadcasted_iota(jnp.int32, (q, q), 0)
    col = lax.broadcasted_iota(jnp.int32, (q, q), 1)
    tril, triu = row >= col, row <= col
    cs_p = _mask_sum(tril.astype(F32).astype(BF16), la)
    cs_s = _mask_sum(triu.astype(F32).astype(BF16), la)
    cs = cs_p if fwd else cs_s
    edge = q - 1 if fwd else 0
    ecs = jnp.exp(cs)
    ecs_x = _expand(ecs, pe_ref[...], pieces=2)
    wd_x = _expand(dt * jnp.exp(cs[edge:edge + 1, :] - cs), pe_ref[...], pieces=2)
    carry = _expand(jnp.broadcast_to(ecs[edge:edge + 1, :], (HALO, LANES)), pe_ref[...])[0:1, :]
    s_old = s_ref[...]
    y_off = _dot(cm, s_old) * ecs_x
    s_new = (s_old * carry + _dot_tn(bm, xs * wd_x)) * bd_ref[...]
    s_ref[...] = s_new

    @pl.when(cpos == nc - 1)
    def _():
        s_t = s_new.T
        p_dim = s_t.shape[0] // heads
        for h in range(heads):
            grp = h // (heads // 2)
            st_ref[h] = s_t[h * p_dim:(h + 1) * p_dim, grp * half:(grp + 1) * half]

    if not fwd:
        out_ref[...] = y_off
        return

    c2_p, c2_s = cs_p * LOG2E, cs_s * LOG2E
    ldt = jnp.log2(dt)
    rt_p, rt_s = (c2_p - ldt).T, (c2_s - ldt).T
    neg = -jnp.inf
    lane = lax.broadcasted_iota(jnp.int32, (q, LANES), 1)
    lo, hi = lane < half, lane >= half
    gmat = (_dot_nt(jnp.where(lo, cm, 0.0), bm), _dot_nt(jnp.where(hi, cm, 0.0), bm))
    for p in range(heads // 2):
        xp = xs[:, p * LANES:(p + 1) * LANES]
        acc = None
        for hh in range(2):
            h = 2 * p + hh
            hb = heads + h
            lf = jnp.exp2(jnp.where(tril, c2_p[:, h:h + 1] - rt_p[h:h + 1, :], neg))
            lb = jnp.exp2(jnp.where(triu, c2_s[:, hb:hb + 1] - rt_s[hb:hb + 1, :], neg))
            w = gmat[h // (heads // 2)] * (lf + lb)
            part = _dot(w, jnp.where(lo if hh == 0 else hi, xp, 0.0))
            acc = part if acc is None else acc + part
        yacc_ref[:, p * LANES:(p + 1) * LANES] = acc
    y = yacc_ref[...] + y_off + yb_ref[...] + dx_ref[...] * xs
    yz = y * _silu(z_ref[...])
    out_ref[...] = _rms(yz, gn_ref[...]).astype(BF16)


def _ssd_call(fwd, proj, yb, prm, h0, geom, heads, width):
    rows = proj.shape[0]
    q = geom.scan_chunk()
    nhalo = rows // HALO
    bc_w = 2 * LANES
    xs_blk, z_blk, bc_blk, dt_blk = 0, 1, 5 * width // bc_w, (5 * width + bc_w) // LANES

    def seq(t):
        return geom.chunk_pos(t, q, fwd)[0]

    def rowblk(t):
        return geom.chunk_pos(t, q, fwd)[4]

    def prev(t):
        return jnp.maximum(rowblk(t) * (q // HALO) - 1, 0)

    def nxt(t):
        return jnp.minimum((rowblk(t) + 1) * (q // HALO), nhalo - 1)

    const = lambda t: (0, 0)
    in_specs = [pl.BlockSpec((q, width), lambda t: (rowblk(t), xs_blk)),
                pl.BlockSpec((HALO, width), lambda t: (prev(t), xs_blk)),
                pl.BlockSpec((HALO, width), lambda t: (nxt(t), xs_blk)),
                pl.BlockSpec((q, bc_w), lambda t: (rowblk(t), bc_blk)),
                pl.BlockSpec((HALO, bc_w), lambda t: (prev(t), bc_blk)),
                pl.BlockSpec((HALO, bc_w), lambda t: (nxt(t), bc_blk)),
                pl.BlockSpec((q, LANES), lambda t: (rowblk(t), dt_blk)),
                pl.BlockSpec((3, width), const), pl.BlockSpec((1, width), const),
                pl.BlockSpec((3, bc_w), const), pl.BlockSpec((1, bc_w), const),
                pl.BlockSpec((1, LANES), const), pl.BlockSpec((1, LANES), const),
                pl.BlockSpec((LANES, width), const), pl.BlockSpec((LANES, width), const),
                pl.BlockSpec((None, LANES, width), lambda t: (jnp.maximum(seq(t) - geom.n_ctx, 0), 0, 0))]
    args = [proj, proj, proj, proj, proj, proj, proj, prm["cwx"], prm["cbx"], prm["cwb"], prm["cbb"],
            prm["dtb"], prm["alog"], prm["pf"] if fwd else prm["pb"], prm["bd"], h0]
    scratch = [pltpu.VMEM((LANES, width), F32)]
    if fwd:
        in_specs += [pl.BlockSpec((q, width), lambda t: (rowblk(t), z_blk)),
                     pl.BlockSpec((q, width), lambda t: (rowblk(t), 0)),
                     pl.BlockSpec((1, width), const), pl.BlockSpec((1, width), const)]
        args += [proj, yb, prm["dx"], prm["gn"]]
        scratch.append(pltpu.VMEM((q, width), F32))
        out_dtype = BF16
    else:
        out_dtype = F32
    return pl.pallas_call(
        functools.partial(_ssd_kernel, fwd, geom, heads),
        grid=(rows // q,),
        in_specs=in_specs,
        out_specs=[pl.BlockSpec((q, width), lambda t: (rowblk(t), 0)),
                   pl.BlockSpec((None, heads, width // heads, LANES // 2), lambda t: (seq(t), 0, 0, 0))],
        out_shape=[jax.ShapeDtypeStruct((rows, width), out_dtype),
                   jax.ShapeDtypeStruct((geom.n_ctx + geom.n_lat, heads, width // heads, LANES // 2), F32)],
        scratch_shapes=scratch,
        compiler_params=_params(("arbitrary",)),
        name="ssd_fwd" if fwd else "ssd_bwd",
    )(*args)


def _rope(x, cos, sin_signed):
    lane = lax.broadcasted_iota(jnp.int32, x.shape, 1)
    partner = jnp.where((lane & 31) < 16, pltpu.roll(x, LANES - 16, 1), pltpu.roll(x, 16, 1))
    return x * cos + partner * sin_signed


def _attn_kernel(latent, lam_init, *refs):
    if latent:
        (q_ref, k_ref, v_ref, kc_ref, vc_ref, cq_ref, sq_ref, ck_ref, sk_ref, lp_ref, g_ref,
         o_ref, kr_ref, vr_ref) = refs
    else:
        q_ref, k_ref, v_ref, lp_ref, g_ref, o_ref, kr_ref, vr_ref = refs

    @pl.when(pl.program_id(2) == 0)
    def _():
        k = k_ref[...]
        if latent:
            k = _rope(k, ck_ref[...], sk_ref[...])
        kr_ref[...] = k.astype(BF16)
        vr_ref[...] = v_ref[...].astype(BF16)

    q = q_ref[...]
    if latent:
        q = _rope(q, cq_ref[...], sq_ref[...])
    q = q * (float(LANES // 2) ** -0.5 * LOG2E)
    lane = lax.broadcasted_iota(jnp.int32, q.shape, 1)
    qs = (jnp.where(lane < LANES // 2, q, 0.0).astype(BF16), jnp.where(lane >= LANES // 2, q, 0.0).astype(BF16))
    keys = [kr_ref[...]] + ([kc_ref[...].astype(BF16)] if latent else [])
    vals = [vr_ref[...]] + ([vc_ref[...].astype(BF16)] if latent else [])
    lp = lp_ref[...]
    lam = (jnp.exp(jnp.sum(lp[0:1] * lp[1:2], axis=1, keepdims=True))
           - jnp.exp(jnp.sum(lp[2:3] * lp[3:4], axis=1, keepdims=True)) + lam_init)
    probs, coef = [], []
    for m in range(2):
        s = [lax.dot_general(qs[m], kk, (((1,), (1,)), ((), ())), preferred_element_type=F32) for kk in keys]
        mx = functools.reduce(jnp.maximum, [jnp.max(t, axis=1, keepdims=True) for t in s])
        p = [jnp.exp2(t - mx) for t in s]
        den = functools.reduce(jnp.add, [jnp.sum(t, axis=1, keepdims=True) for t in p])
        probs.append(p)
        coef.append(1.0 / den if m == 0 else lam / den)
    o = None
    for seg in range(len(keys)):
        pd = (probs[0][seg] * coef[0] - probs[1][seg] * coef[1]).astype(BF16)
        part = jnp.dot(pd, vals[seg], preferred_element_type=F32)
        o = part if o is None else o + part
    o_ref[...] = (_rms(o, g_ref[...]) * (1.0 - lam_init)).astype(BF16)


def _attn_ctx_kernel(lam_init, heads, q_ref, k_ref, v_ref, lp_ref, g_ref, o_ref, ko_ref, vo_ref):
    ko_ref[...] = k_ref[...]
    vo_ref[...] = v_ref[...]
    rows = q_ref.shape[0]
    lane = lax.broadcasted_iota(jnp.int32, (rows, LANES), 1)
    lo = lane < LANES // 2
    scale = float(LANES // 2) ** -0.5 * LOG2E
    lp = lp_ref[...]
    lam = (jnp.exp(jnp.sum(lp[0:1] * lp[1:2], axis=1, keepdims=True))
           - jnp.exp(jnp.sum(lp[2:3] * lp[3:4], axis=1, keepdims=True)) + lam_init)
    hs = range(heads)
    sls = [slice(h * LANES, (h + 1) * LANES) for h in hs]
    q = [q_ref[:, sl] * scale for sl in sls]
    k = [k_ref[:, sl].astype(BF16) for sl in sls]
    s0 = [_dot_nt(jnp.where(lo, q[h], 0.0), k[h]) for h in hs]
    s1 = [_dot_nt(jnp.where(lo, 0.0, q[h]), k[h]) for h in hs]
    pd = []
    for h in hs:
        p0 = jnp.exp2(s0[h] - jnp.max(s0[h], axis=1, keepdims=True))
        p1 = jnp.exp2(s1[h] - jnp.max(s1[h], axis=1, keepdims=True))
        c0 = 1.0 / jnp.sum(p0, axis=1, keepdims=True)
        c1 = lam / jnp.sum(p1, axis=1, keepdims=True)
        pd.append(p0 * c0 - p1 * c1)
    o = [_dot(pd[h], v_ref[:, sls[h]]) for h in hs]
    for h in hs:
        o_ref[:, sls[h]] = (_rms(o[h], g_ref[...]) * (1.0 - lam_init)).astype(BF16)


def _attn_ctx_call(lam_init, proj, lam_p, norm_g, n_seq, seq_len, heads, width):
    spec = lambda blk: pl.BlockSpec((seq_len, width), lambda b: (b, blk))
    return pl.pallas_call(
        functools.partial(_attn_ctx_kernel, lam_init, heads),
        grid=(n_seq,),
        in_specs=[spec(2), spec(3), spec(4), pl.BlockSpec(lam_p.shape, lambda b: (0, 0)),
                  pl.BlockSpec((1, LANES), lambda b: (0, 0))],
        out_specs=[pl.BlockSpec((seq_len, width), lambda b: (b, 0))] * 3,
        out_shape=[jax.ShapeDtypeStruct((n_seq * seq_len, width), BF16),
                   jax.ShapeDtypeStruct((n_seq * seq_len, width), F32),
                   jax.ShapeDtypeStruct((n_seq * seq_len, width), F32)],
        compiler_params=_params(("arbitrary",)),
        name="diff_attn_ctx",
    )(proj, proj, proj, lam_p, norm_g.reshape(1, LANES))


def _attn_call(latent, lam_init, proj, cache_k, cache_v, tables, lam_p, norm_g, row0, n_seq, seq_len, heads, width):
    tq = min(Q_TILE, seq_len)
    nq = seq_len // tq
    hpw = width // LANES
    q_blk, k_blk, v_blk = 2 * hpw, 3 * hpw, 4 * hpw
    qb, sb = row0 // tq, row0 // seq_len
    in_specs = [pl.BlockSpec((tq, LANES), lambda b, h, i: (qb + b * nq + i, q_blk + h)),
                pl.BlockSpec((seq_len, LANES), lambda b, h, i: (sb + b, k_blk + h)),
                pl.BlockSpec((seq_len, LANES), lambda b, h, i: (sb + b, v_blk + h))]
    args = [proj, proj, proj]
    if latent:
        past = cache_k.shape[1]
        in_specs += [pl.BlockSpec((None, past, LANES), lambda b, h, i: (b, 0, h)),
                     pl.BlockSpec((None, past, LANES), lambda b, h, i: (b, 0, h)),
                     pl.BlockSpec((tq, LANES), lambda b, h, i: (i, 0)),
                     pl.BlockSpec((tq, LANES), lambda b, h, i: (i, 0)),
                     pl.BlockSpec((seq_len, LANES), lambda b, h, i: (0, 0)),
                     pl.BlockSpec((seq_len, LANES), lambda b, h, i: (0, 0))]
        args += [cache_k, cache_v, tables[0], tables[1], tables[0], tables[1]]
    in_specs += [pl.BlockSpec(lam_p.shape, lambda b, h, i: (0, 0)), pl.BlockSpec((1, LANES), lambda b, h, i: (0, 0))]
    args += [lam_p, norm_g.reshape(1, LANES)]
    return pl.pallas_call(
        functools.partial(_attn_kernel, latent, lam_init),
        grid=(n_seq, heads, nq),
        in_specs=in_specs,
        out_specs=pl.BlockSpec((tq, LANES), lambda b, h, i: (b * nq + i, h)),
        out_shape=jax.ShapeDtypeStruct((n_seq * seq_len, width), BF16),
        scratch_shapes=[pltpu.VMEM((seq_len, LANES), BF16), pltpu.VMEM((seq_len, LANES), BF16)],
        compiler_params=_params(("arbitrary", "arbitrary", "arbitrary")),
        name="diff_attn_lat" if latent else "diff_attn_ctx",
    )(*args)


def _rope_tables(n_tokens, grid_w):
    n_freq = LANES // 8
    pos = jnp.arange(n_tokens)
    r = (pos // grid_w).astype(F32)
    cpos = (pos % grid_w).astype(F32)
    inv = ROPE_BASE ** (-jnp.arange(n_freq, dtype=F32) / n_freq)
    ang_r, ang_c = r[:, None] * inv, cpos[:, None] * inv
    cos32 = lambda a: jnp.concatenate([jnp.cos(a), jnp.cos(a)], axis=1)
    sin32 = lambda a: jnp.concatenate([-jnp.sin(a), jnp.sin(a)], axis=1)
    cos64 = jnp.concatenate([cos32(ang_r), cos32(ang_c)], axis=1)
    sin64 = jnp.concatenate([sin32(ang_r), sin32(ang_c)], axis=1)
    return jnp.concatenate([cos64, cos64], axis=1), jnp.concatenate([sin64, sin64], axis=1)


def _gdn_masks(rb, sb, fwd):
    shift = GDN_CHUNK.bit_length() - 1

    def tri(n):
        row = jnp.arange(n)[:, None]
        col = jnp.arange(n)[None, :]
        blk = (row >> shift) == (col >> shift)
        return row, col, blk, blk & ((row >= col) if fwd else (row <= col))

    row, col, blk, incl = tri(sb)
    strict = blk & ((row > col) if fwd else (row < col))
    ms = [incl, strict, row == col, (row >> 1) == (col >> 1)]
    for lvl in range(1, shift):
        ms.append(((row >> (lvl + 1)) == (col >> (lvl + 1))) & ((row >> lvl) != (col >> lvl)))
    _, _, blk_rb, incl_rb = tri(rb)
    return jnp.stack(ms).astype(F32), jnp.stack([incl_rb, blk_rb]).astype(BF16)


def _gdn_kernel(fwd, geom, heads, hps, *refs):
    (q_ref, qp_ref, qn_ref, k_ref, kp_ref, kn_ref, v_ref, vp_ref, vn_ref, ab_ref,
     cwq_ref, cbq_ref, cwk_ref, cbk_ref, cwv_ref, cbv_ref, alog_ref, dtb_ref, mf_ref, mb_ref, s0_ref) = refs[:21]
    if fwd:
        z_ref, ob_ref, gn_ref, out_ref, st_ref, s_ref = refs[21:]
    else:
        out_ref, st_ref, s_ref = refs[21:]
    hg = pl.program_id(0)
    rb = q_ref.shape[0]
    seq, cpos, nb, c, _ = geom.chunk_pos(pl.program_id(1), rb, fwd)
    first, last = c == 0, c == nb - 1
    dk = q_ref.shape[1] // hps
    ch = GDN_CHUNK
    nsub = rb // ch
    n_lvl = ch.bit_length() - 1

    @pl.when(cpos == 0)
    def _():
        s_ref[...] = jnp.where(seq < geom.n_ctx, 0.0, s0_ref[...])

    ab = ab_ref[...]
    d0 = 0 if fwd else 2 * heads
    g_all = -jnp.exp(alog_ref[...]) * _softplus(ab + dtb_ref[...])
    b_all = 1.0 / (1.0 + jnp.exp(-ab))
    gc_all = _mask_sum(mb_ref[0], g_all)
    gt_all = _mask_sum(mb_ref[1], g_all)
    gct_all = gc_all.T
    lane = lax.broadcasted_iota(jnp.int32, ab.shape, 1)
    sub = lax.broadcasted_iota(jnp.int32, gct_all.shape, 0)
    incl_f, strict_f, eye, pair = mf_ref[0], mf_ref[1], mf_ref[2], mf_ref[3]
    sb = mf_ref.shape[1]
    cps = sb // ch

    def pick(a, l):
        return jnp.sum(jnp.where(lane == l, a, 0.0), axis=1, keepdims=True)

    hs = range(hps)
    sls = [slice(hh * dk, (hh + 1) * dk) for hh in hs]
    cs = [(hh, blk) for hh in hs for blk in range(rb // sb)]
    nch = range(len(cs))
    kn, qn, vb, gc, gtot, beta, dec = ([] for _ in range(7))
    for hh in hs:
        hd = hg * hps + hh
        sl = sls[hh]

        def conv(x_ref, xp_ref, xn_ref, cw_ref, cb_ref):
            return _conv3_silu(x_ref[:, sl], xp_ref[:, sl], xn_ref[:, sl], cw_ref[:, sl], cb_ref[:, sl], first, last)

        qc = conv(q_ref, qp_ref, qn_ref, cwq_ref, cbq_ref)
        kc = conv(k_ref, kp_ref, kn_ref, cwk_ref, cbk_ref)
        vh = conv(v_ref, vp_ref, vn_ref, cwv_ref, cbv_ref)
        qh = qc * lax.rsqrt(jnp.sum(qc * qc, axis=1, keepdims=True) + EPS) * (float(dk) ** -0.5)
        kh = kc * lax.rsqrt(jnp.sum(kc * kc, axis=1, keepdims=True) + EPS)
        gch, gth = pick(gc_all, d0 + hd), pick(gt_all, d0 + hd)
        bh = pick(b_all, d0 + heads + hd)
        g_row = jnp.sum(jnp.where(sub == d0 + hd, gct_all, 0.0), axis=0, keepdims=True)
        for blk in range(rb // sb):
            r = slice(blk * sb, (blk + 1) * sb)
            qn.append(qh[r]); kn.append(kh[r]); gc.append(gch[r]); gtot.append(gth[r]); beta.append(bh[r])
            vb.append(vh[r] * bh[r])
            dec.append(jnp.exp(jnp.minimum(gch[r] - g_row[:, r], 0.0)) * incl_f)
    kk = [_dot_nt(kn[c], kn[c]) for c in nch]
    m = [(kk[c] * beta[c]) * (dec[c] * strict_f) for c in nch]
    t = [eye - m[c] * pair for c in nch]
    for lvl in range(1, n_lvl):
        off = mf_ref[3 + lvl]
        tm = [_dot(t[c], m[c] * off) for c in nch]
        tmt = [_dot(tm[c], t[c]) for c in nch]
        t = [t[c] - tmt[c] for c in nch]
    eg = [jnp.exp(gc[c]) for c in nch]
    sol = [_dot(t[c], jnp.concatenate([vb[c], kn[c] * (beta[c] * eg[c])], axis=1)) for c in nch]
    qk = [_dot_nt(qn[c], kn[c]) for c in nch]
    qg = [qn[c] * eg[c] for c in nch]
    kg = [kn[c] * jnp.exp(gtot[c] - gc[c]) for c in nch]
    egl = [jnp.exp(gtot[c]) for c in nch]
    s = [s_ref[hh] for hh in hs]
    vnew = [[None] * cps for _ in nch]
    qs = [[None] * cps for _ in nch]
    for i in (range(nsub) if fwd else reversed(range(nsub))):
        blk, li = divmod(i, cps)
        r = slice(li * ch, (li + 1) * ch)
        idx = [hh * (rb // sb) + blk for hh in hs]
        ws = [_dot(jnp.concatenate([sol[c][r, dk:], qg[c][r]], axis=0), s[hh]) for hh, c in zip(hs, idx)]
        for hh, c in zip(hs, idx):
            vnew[c][li] = sol[c][r, :dk] - ws[hh][:ch]
            qs[c][li] = ws[hh][ch:]
        upd = [_dot_tn(kg[c][r], vnew[c][li]) for c in idx]
        s = [s[hh] * egl[c][li * ch:li * ch + 1, :] + upd[hh] for hh, c in zip(hs, idx)]
    intra = [_dot(qk[c] * dec[c], jnp.concatenate(vnew[c], axis=0)) for c in nch]
    for c, (hh, blk) in enumerate(cs):
        sl = sls[hh]
        r = slice(blk * sb, (blk + 1) * sb)
        o = jnp.concatenate(qs[c], axis=0) + intra[c]
        if fwd:
            o = o + ob_ref[r, sl]
            out_ref[r, sl] = (_rms(o, gn_ref[...]) * _silu(z_ref[r, sl])).astype(BF16)
        else:
            out_ref[r, sl] = o
    for hh in hs:
        s_ref[hh] = s[hh]

    @pl.when(cpos == nb - 1)
    def _():
        st_ref[...] = s_ref[...]


def _gdn_call(fwd, proj, ob, prm, s0, geom, heads, width):
    rows = proj.shape[0]
    rb = geom.scan_chunk()
    nhalo = rows // HALO
    dk = width // heads
    hps = GDN_HEADS_PER_STEP
    hw = hps * dk
    ngrp = heads // hps
    ab_blk = 4 * width // LANES
    mf, mb = _gdn_masks(rb, min(GDN_SYSTEM_ROWS, rb), fwd)

    def seq(t):
        return geom.chunk_pos(t, rb, fwd)[0]

    def rowblk(t):
        return geom.chunk_pos(t, rb, fwd)[4]

    def prev(t):
        return jnp.maximum(rowblk(t) * (rb // HALO) - 1, 0)

    def nxt(t):
        return jnp.minimum((rowblk(t) + 1) * (rb // HALO), nhalo - 1)

    in_specs, args = [], []
    for sec in range(3):
        in_specs += [pl.BlockSpec((rb, hw), lambda h, t, sec=sec: (rowblk(t), sec * ngrp + h)),
                     pl.BlockSpec((HALO, hw), lambda h, t, sec=sec: (prev(t), sec * ngrp + h)),
                     pl.BlockSpec((HALO, hw), lambda h, t, sec=sec: (nxt(t), sec * ngrp + h))]
        args += [proj, proj, proj]
    in_specs.append(pl.BlockSpec((rb, LANES), lambda h, t: (rowblk(t), ab_blk)))
    args.append(proj)
    for sec in range(3):
        in_specs += [pl.BlockSpec((3, hw), lambda h, t, sec=sec: (0, sec * ngrp + h)),
                     pl.BlockSpec((1, hw), lambda h, t, sec=sec: (0, sec * ngrp + h))]
        args += [prm["cw"], prm["cb"]]
    in_specs += [pl.BlockSpec((1, LANES), lambda h, t: (0, 0)), pl.BlockSpec((1, LANES), lambda h, t: (0, 0)),
                 pl.BlockSpec(mf.shape, lambda h, t: (0, 0, 0)), pl.BlockSpec(mb.shape, lambda h, t: (0, 0, 0)),
                 pl.BlockSpec((None, hps, dk, dk), lambda h, t: (jnp.maximum(seq(t) - geom.n_ctx, 0), h, 0, 0))]
    args += [prm["alog"], prm["dtb"], mf, mb, s0]
    if fwd:
        in_specs += [pl.BlockSpec((rb, hw), lambda h, t: (rowblk(t), 3 * ngrp + h)),
                     pl.BlockSpec((rb, hw), lambda h, t: (rowblk(t), h)),
                     pl.BlockSpec((1, dk), lambda h, t: (0, 0))]
        args += [proj, ob, prm["gn"]]
    return pl.pallas_call(
        functools.partial(_gdn_kernel, fwd, geom, heads, hps),
        grid=(ngrp, rows // rb),
        in_specs=in_specs,
        out_specs=[pl.BlockSpec((rb, hw), lambda h, t: (rowblk(t), h)),
                   pl.BlockSpec((None, hps, dk, dk), lambda h, t: (seq(t), h, 0, 0))],
        out_shape=[jax.ShapeDtypeStruct((rows, width), BF16 if fwd else F32),
                   jax.ShapeDtypeStruct((geom.n_ctx + geom.n_lat, heads, dk, dk), F32)],
        scratch_shapes=[pltpu.VMEM((hps, dk, dk), F32)],
        compiler_params=_params(("arbitrary", "arbitrary")),
        name="gdn_fwd" if fwd else "gdn_bwd",
    )(*args)


def _pad_cols(w, n):
    return jnp.pad(w, ((0, 0), (0, n - w.shape[1])))


def _lane_row(v):
    v = v.reshape(1, -1).astype(F32)
    return _pad_cols(v, LANES)


def _even_layer(x, mods, layer, e, geom, norm_g, w_in, conv_w, conv_b, dt_bias, a_log, d_skip, ssd_g,
                lam_p, da_g, w_out, cache_k, cache_v, state_ssd, tables):
    d = x.shape[1]
    heads, p_dim, n_state = state_ssd.shape[3], state_ssd.shape[4], state_ssd.shape[5]
    width = heads * p_dim
    da_heads = cache_k.shape[3]
    assert width == d and da_heads * LANES == d and 2 * n_state == LANES and 2 * p_dim == LANES
    bc_w = 2 * LANES
    n_dt = 2 * heads
    o_z, o_xs, o_bc, o_dt = 0, width, 2 * width, 2 * width + bc_w
    o_q = o_dt + n_dt
    cols = [w_in[:, o_xs:o_xs + width], w_in[:, o_z:o_z + width], w_in[:, o_q:o_q + 3 * d],
            w_in[:, o_bc:o_bc + bc_w], w_in[:, o_dt:o_dt + n_dt]]
    n_used = 5 * width + bc_w + n_dt
    n_pad = -(-n_used // IN_COL_TILE) * IN_COL_TILE
    w_perm = _pad_cols(jnp.concatenate(cols, axis=1), n_pad).astype(BF16)
    proj = _inproj_call(x, mods, layer, norm_g, w_perm, geom)

    hl = jnp.arange(LANES)[:, None]
    cl = jnp.arange(width)[None, :]
    prm = {
        "cwx": conv_w[:, :width], "cbx": conv_b[None, :width],
        "cwb": conv_w[:, width:], "cbb": conv_b[None, width:],
        "dtb": _lane_row(dt_bias), "alog": _lane_row(a_log),
        "pf": (hl == cl // p_dim).astype(BF16), "pb": (hl == heads + cl // p_dim).astype(BF16),
        "bd": (hl // n_state == cl // (width // 2)).astype(F32),
        "dx": jnp.repeat(d_skip, p_dim)[None, :], "gn": ssd_g[None, :],
    }

    def state_in(st):
        t = st.transpose(0, 3, 1, 2).reshape(st.shape[0], n_state, width)
        return jnp.concatenate([t, t], axis=1) * prm["bd"]

    h0f, h0b = state_in(state_ssd[:, e, 0].astype(F32)), state_in(state_ssd[:, e, 1].astype(F32))
    yb, st_b = _ssd_call(False, proj, None, prm, h0b, geom, heads, width)
    y, st_f = _ssd_call(True, proj, yb, prm, h0f, geom, heads, width)
    new_states = jnp.stack([st_f[:geom.n_ctx], st_b[:geom.n_ctx]], axis=1)
    lam_init = 0.8 - 0.6 * math.exp(-0.3 * layer)
    o_ctx, k_ctx, v_ctx = _attn_ctx_call(lam_init, proj, lam_p, da_g, geom.n_ctx, geom.s_ctx, da_heads, d)
    o_lat = _attn_call(True, lam_init, proj, cache_k[:, e].reshape(geom.n_lat, -1, d),
                       cache_v[:, e].reshape(geom.n_lat, -1, d), tables, lam_p, da_g, geom.ctx_rows,
                       geom.n_lat, geom.s_lat, da_heads, d)
    w_o = w_out.astype(BF16)
    x = _outproj_call(x, mods, layer, [y, (o_ctx, o_lat)], [w_o[:width], w_o[width:]], geom)
    k_new = k_ctx.reshape(geom.n_ctx, geom.s_ctx, da_heads, LANES)
    v_new = v_ctx.reshape(geom.n_ctx, geom.s_ctx, da_heads, LANES)
    return x, k_new, v_new, new_states


def _odd_layer(x, mods, layer, o, geom, norm_g, w_in, conv_w, conv_b, a_log, dt_bias, g_norm, w_out, state_gdn):
    d = x.shape[1]
    heads, dk, dv = state_gdn.shape[3], state_gdn.shape[4], state_gdn.shape[5]
    width = heads * dv
    assert width == d and dk == LANES and dv == LANES and 4 * heads <= LANES
    n_used = 4 * width + 4 * heads
    n_pad = -(-n_used // IN_COL_TILE) * IN_COL_TILE
    w_perm = _pad_cols(w_in, n_pad).astype(BF16)
    proj = _inproj_call(x, mods, layer, norm_g, w_perm, geom)

    def ab_row(v):
        r = jnp.zeros((2, 2 * heads), F32).at[:, :heads].set(v.astype(F32))
        return _lane_row(r)

    prm = {"cw": conv_w, "cb": conv_b[None, :], "alog": ab_row(a_log), "dtb": ab_row(dt_bias), "gn": g_norm[None, :]}
    s0f, s0b = state_gdn[:, o, 0].astype(F32), state_gdn[:, o, 1].astype(F32)
    ob, st_b = _gdn_call(False, proj, None, prm, s0b, geom, heads, width)
    og, st_f = _gdn_call(True, proj, ob, prm, s0f, geom, heads, width)
    new_states = jnp.stack([st_f[:geom.n_ctx], st_b[:geom.n_ctx]], axis=1)
    x = _outproj_call(x, mods, layer, [og], [w_out.astype(BF16)], geom)
    return x, new_states


def kernel(x_prompt, x_sample, cache_attn_k, cache_attn_v, state_ssd, state_gdn, c, c_ctx, ada_w, ada_b, norm_g, ev_w_in, ev_conv_w, ev_conv_b, ssd_dt_bias, ssd_a_log, ssd_d, ssd_norm_g, da_lambda, da_norm_g, ev_w_out, od_w_in, od_conv_w, od_conv_b, gdn_a_log, gdn_dt_bias, gdn_norm_g, od_w_out, ffn_w_up, ffn_conv_w, ffn_conv_b, ffn_w_down, final_norm_g):
    n_ctx, s_ctx, d = x_prompt.shape
    n_lat, s_lat, _ = x_sample.shape
    depth = ada_w.shape[0]
    assert n_lat + 1 <= MOD_ROWS and s_ctx & (s_ctx - 1) == 0 and s_lat & (s_lat - 1) == 0
    geom = _Geom(n_ctx, s_ctx, n_lat, s_lat)
    dtype = x_prompt.dtype
    x = jnp.concatenate([x_prompt.reshape(-1, d), x_sample.reshape(-1, d)], axis=0).astype(F32)

    cvec = jnp.zeros((MOD_ROWS, d), F32).at[0].set(c_ctx).at[1:1 + n_lat].set(c)
    mods = _mod_call(cvec, ada_w, ada_b)
    mods = mods.reshape(depth, MOD_ROWS, N_MOD, d).transpose(0, 2, 1, 3).reshape(depth * N_MOD * MOD_ROWS, 1, d)

    grid_w = 64
    tables = _rope_tables(s_lat, grid_w)
    ks_out, vs_out, ssd_out, gdn_out = [], [], [], []
    for l in range(depth):
        if l % 2 == 0:
            e = l // 2
            x, k_new, v_new, st = _even_layer(
                x, mods, l, e, geom, norm_g[l, 0], ev_w_in[e], ev_conv_w[e], ev_conv_b[e], ssd_dt_bias[e],
                ssd_a_log[e], ssd_d[e], ssd_norm_g[e], da_lambda[e], da_norm_g[e], ev_w_out[e],
                cache_attn_k, cache_attn_v, state_ssd, tables)
            ks_out.append(k_new)
            vs_out.append(v_new)
            ssd_out.append(st)
        else:
            o = l // 2
            x, st = _odd_layer(x, mods, l, o, geom, norm_g[l, 0], od_w_in[o], od_conv_w[o], od_conv_b[o],
                               gdn_a_log[o], gdn_dt_bias[o], gdn_norm_g[o], od_w_out[o], state_gdn)
            gdn_out.append(st)
        x = _ffn_call(x, mods, l, norm_g.reshape(2 * depth, 1, d), ffn_w_up, ffn_conv_w,
                      ffn_conv_b.reshape(depth, 1, -1), ffn_w_down, geom)
    y_prompt = _final_call(x, final_norm_g, 0, geom.ctx_rows).reshape(n_ctx, s_ctx, d)
    y_sample = _final_call(x, final_norm_g, geom.ctx_rows, n_lat * s_lat).reshape(n_lat, s_lat, d)
    return (y_prompt.astype(dtype), y_sample.astype(dtype),
            jnp.stack(ks_out, axis=1).astype(dtype), jnp.stack(vs_out, axis=1).astype(dtype),
            jnp.stack(ssd_out, axis=1).astype(dtype), jnp.stack(gdn_out, axis=1).astype(dtype))
```

```python
import functools
import math

import jax
import jax.numpy as jnp
from jax import lax
from jax.experimental import pallas as pl
from jax.experimental.pallas import tpu as pltpu

F32 = jnp.float32
BF16 = jnp.bfloat16
EPS = 1e-6
ROPE_BASE = 10000.0
LOG2E = 1.4426950408889634

LANES = 128
HALO = 8
VMEM_LIMIT_BYTES = 56 * 1024 * 1024

N_MOD = 6
MOD_ROWS = 8
SCAN_CHUNK = 256
GDN_CHUNK = 64
GDN_HEADS_PER_STEP = 8
GDN_SYSTEM_ROWS = 128
Q_TILE = 256
ATTN_HEADS_PER_STEP = 2
ROW_TILE = 1024
OUT_ROW_TILE = 512
FF_TILE = 256
FFN_ROW_CHUNK = 256
IN_COL_TILE = 512


def _silu(x):
    return x / (1.0 + jnp.exp(-x))


def _softplus(x):
    return jnp.maximum(x, 0.0) + jnp.log1p(jnp.exp(-jnp.abs(x)))


def _dot(a, b):
    return jnp.dot(a.astype(BF16), b.astype(BF16), preferred_element_type=F32)


def _dot_nt(a, b):
    return lax.dot_general(a.astype(BF16), b.astype(BF16), (((1,), (1,)), ((), ())),
                           preferred_element_type=F32)


def _dot_tn(a, b):
    return lax.dot_general(a.astype(BF16), b.astype(BF16), (((0,), (0,)), ((), ())),
                           preferred_element_type=F32)


def _split3(a):
    hi = a.astype(BF16)
    r1 = a - hi.astype(F32)
    mid = r1.astype(BF16)
    return hi, mid, (r1 - mid.astype(F32)).astype(BF16)


def _expand(a, p, pieces=3):
    d = functools.partial(jnp.dot, preferred_element_type=F32)
    hi, mid, lo = _split3(a)
    out = d(hi, p) + d(mid, p)
    return out + d(lo, p) if pieces == 3 else out


def _mask_sum(m, a):
    d = functools.partial(jnp.dot, preferred_element_type=F32)
    hi, mid, lo = _split3(a)
    return d(m, hi) + d(m, mid) + d(m, lo)


def _rms(x, g):
    ms = jnp.mean(x * x, axis=-1, keepdims=True)
    return x * lax.rsqrt(ms + EPS) * g


def _params(sem):
    return pltpu.CompilerParams(dimension_semantics=sem, vmem_limit_bytes=VMEM_LIMIT_BYTES)


def _conv3_silu(x, xp, xn, w, b, first, last):
    rows = x.shape[0]
    row = lax.broadcasted_iota(jnp.int32, x.shape, 0)
    prev_row = jnp.where(first, 0.0, xp[HALO - 1:HALO, :])
    next_row = jnp.where(last, 0.0, xn[0:1, :])
    x_prev = jnp.where(row == 0, prev_row, pltpu.roll(x, 1, 0))
    x_next = jnp.where(row == rows - 1, next_row, pltpu.roll(x, rows - 1, 0))
    y = x_prev * w[0:1, :] + x * w[1:2, :] + x_next * w[2:3, :] + b
    return _silu(y)


def _mod_kernel(c_ref, w_ref, b_ref, o_ref):
    s = _silu(c_ref[...])
    o_ref[...] = _dot(s, w_ref[...]) + b_ref[...]


def _mod_call(cvec, ada_w, ada_b):
    depth, d, n = ada_w.shape
    tn = d
    return pl.pallas_call(
        _mod_kernel,
        grid=(depth, n // tn),
        in_specs=[pl.BlockSpec((MOD_ROWS, d), lambda l, j: (0, 0)),
                  pl.BlockSpec((None, d, tn), lambda l, j: (l, 0, j)),
                  pl.BlockSpec((None, 1, tn), lambda l, j: (l, 0, j))],
        out_specs=pl.BlockSpec((None, MOD_ROWS, tn), lambda l, j: (l, 0, j)),
        out_shape=jax.ShapeDtypeStruct((depth, MOD_ROWS, n), F32),
        compiler_params=_params(("arbitrary", "arbitrary")),
        name="ada_mod",
    )(cvec, ada_w, ada_b.reshape(depth, 1, n))


class _Geom:
    def __init__(self, n_ctx, s_ctx, n_lat, s_lat):
        self.n_ctx, self.s_ctx, self.n_lat, self.s_lat = n_ctx, s_ctx, n_lat, s_lat
        self.ctx_rows = n_ctx * s_ctx
        self.rows = self.ctx_rows + n_lat * s_lat

    def row_tile(self, want):
        t = want
        while self.ctx_rows % t or self.s_lat % t:
            t //= 2
        return t

    def scan_chunk(self):
        return min(SCAN_CHUNK, self.s_ctx, self.s_lat)

    def chunk_pos(self, t, q, fwd):
        ctx_blocks = self.ctx_rows // q
        nc_ctx, nc_lat = self.s_ctx // q, self.s_lat // q
        is_ctx = t < ctx_blocks
        tl = jnp.maximum(t - ctx_blocks, 0)
        seq = jnp.where(is_ctx, t >> (nc_ctx.bit_length() - 1), self.n_ctx + (tl >> (nc_lat.bit_length() - 1)))
        cpos = jnp.where(is_ctx, t & (nc_ctx - 1), tl & (nc_lat - 1))
        nc = jnp.where(is_ctx, nc_ctx, nc_lat)
        c = cpos if fwd else nc - 1 - cpos
        return seq, cpos, nc, c, t - cpos + c

    def group(self, tile):
        def fn(i):
            r = i * tile
            return jnp.where(r < self.ctx_rows, 0,
                             1 + (jnp.maximum(r - self.ctx_rows, 0) >> (self.s_lat.bit_length() - 1)))
        return fn


def _mod_spec(d, layer, which, group_fn):
    return pl.BlockSpec((None, 1, d), lambda i, *_: ((layer * N_MOD + which) * MOD_ROWS + group_fn(i), 0, 0))


def _inproj_kernel(x_ref, g_ref, sh_ref, sc_ref, w_ref, o_ref, h_ref):
    @pl.when(pl.program_id(1) == 0)
    def _():
        h_ref[...] = (_rms(x_ref[...], g_ref[...]) * (1.0 + sc_ref[...]) + sh_ref[...]).astype(BF16)

    o_ref[...] = jnp.dot(h_ref[...], w_ref[...], preferred_element_type=F32)


def _inproj_call(x, mods, layer, norm_g, w, geom):
    rows, d = x.shape
    n = w.shape[1]
    tm = geom.row_tile(ROW_TILE)
    tn = IN_COL_TILE
    grp = geom.group(tm)
    return pl.pallas_call(
        _inproj_kernel,
        grid=(rows // tm, n // tn),
        in_specs=[pl.BlockSpec((tm, d), lambda i, j: (i, 0)),
                  pl.BlockSpec((1, d), lambda i, j: (0, 0)),
                  _mod_spec(d, layer, 0, grp),
                  _mod_spec(d, layer, 1, grp),
                  pl.BlockSpec((d, tn), lambda i, j: (0, j))],
        out_specs=pl.BlockSpec((tm, tn), lambda i, j: (i, j)),
        out_shape=jax.ShapeDtypeStruct((rows, n), F32),
        scratch_shapes=[pltpu.VMEM((tm, d), BF16)],
        compiler_params=_params(("arbitrary", "arbitrary")),
        name="in_proj",
    )(x, norm_g.reshape(1, d), mods, mods, w)


def _outproj_kernel(split, ctx_tiles, x_ref, gate_ref, *refs):
    n_a = sum(2 if s else 1 for s in split)
    a_refs, w_refs, o_ref = list(refs[:n_a]), refs[n_a:n_a + len(split)], refs[n_a + len(split)]
    is_ctx = pl.program_id(0) < ctx_tiles
    acc = None
    for s, w_ref in zip(split, w_refs):
        a = a_refs.pop(0)[...]
        if s:
            a = jnp.where(is_ctx, a, a_refs.pop(0)[...])
        part = jnp.dot(a, w_ref[...], preferred_element_type=F32)
        acc = part if acc is None else acc + part
    o_ref[...] = x_ref[...] + gate_ref[...] * acc


def _outproj_call(x, mods, layer, acts, ws, geom):
    rows, d = x.shape
    tm = geom.row_tile(OUT_ROW_TILE)
    grp = geom.group(tm)
    ctx_tiles = geom.ctx_rows // tm
    split = tuple(isinstance(a, tuple) for a in acts)
    in_specs = [pl.BlockSpec((tm, d), lambda i: (i, 0)), _mod_spec(d, layer, 2, grp)]
    flat = []
    for a in acts:
        if isinstance(a, tuple):
            in_specs += [pl.BlockSpec((tm, a[0].shape[1]), lambda i: (jnp.minimum(i, ctx_tiles - 1), 0)),
                         pl.BlockSpec((tm, a[1].shape[1]), lambda i: (jnp.maximum(i - ctx_tiles, 0), 0))]
            flat += list(a)
        else:
            in_specs.append(pl.BlockSpec((tm, a.shape[1]), lambda i: (i, 0)))
            flat.append(a)
    in_specs += [pl.BlockSpec(w.shape, lambda i: (0, 0)) for w in ws]
    return pl.pallas_call(
        functools.partial(_outproj_kernel, split, ctx_tiles),
        grid=(rows // tm,),
        in_specs=in_specs,
        out_specs=pl.BlockSpec((tm, d), lambda i: (i, 0)),
        out_shape=jax.ShapeDtypeStruct((rows, d), F32),
        compiler_params=_params(("arbitrary",)),
        name="out_proj",
    )(x, mods, *flat, *ws)


def _ffn_kernel(geom, nf, x_ref, xp_ref, xn_ref, g_ref, sh_ref, sc_ref, gate_ref,
                wg_ref, wv_ref, cwg_ref, cwv_ref, cbg_ref, cbv_ref, wd_ref, o_ref, h_ref, acc_ref, *u_refs):
    i, j = pl.program_id(0), pl.program_id(1)
    tm = x_ref.shape[0]
    bufs = (u_refs[0:2], u_refs[2:4])

    def up(buf):
        h = h_ref[...]
        buf[0][...] = jnp.dot(h, wg_ref[...].astype(BF16), preferred_element_type=F32)
        buf[1][...] = jnp.dot(h, wv_ref[...].astype(BF16), preferred_element_type=F32)

    def down(buf):
        rc = FFN_ROW_CHUNK
        row0 = i * tm
        is_ctx = row0 < geom.ctx_rows
        base = jnp.where(is_ctx, row0, row0 - geom.ctx_rows)
        lseq = jnp.where(is_ctx, geom.s_ctx, geom.s_lat)
        rowi = lax.broadcasted_iota(jnp.int32, (rc, LANES), 0)
        wd = wd_ref[...].astype(BF16)
        for r in range(0, tm, rc):
            at_start = (rowi == 0) & (((base + r) & (lseq - 1)) == 0)
            at_end = (rowi == rc - 1) & (((base + r + rc) & (lseq - 1)) == 0)

            def conv(u_ref, cw_ref, cb_ref):
                cols = []
                for c in range(0, u_ref.shape[1], LANES):
                    cs = slice(c, c + LANES)
                    up_ = jnp.where(at_start, 0.0, u_ref[HALO + r - 1:HALO + r - 1 + rc, cs])
                    un = jnp.where(at_end, 0.0, u_ref[HALO + r + 1:HALO + r + 1 + rc, cs])
                    cols.append(up_ * cw_ref[0:1, cs] + u_ref[HALO + r:HALO + r + rc, cs] * cw_ref[1:2, cs]
                                + un * cw_ref[2:3, cs] + cb_ref[:, cs])
                return jnp.concatenate(cols, axis=1)

            a = _silu(conv(buf[0], cwg_ref, cbg_ref)) * conv(buf[1], cwv_ref, cbv_ref)
            acc_ref[r:r + rc, :] += jnp.dot(a.astype(BF16), wd, preferred_element_type=F32)

    @pl.when(j == 0)
    def _():
        def hn(x):
            return (_rms(x, g_ref[...]) * (1.0 + sc_ref[...]) + sh_ref[...]).astype(BF16)
        h_ref[0:HALO, :] = hn(xp_ref[...])
        h_ref[HALO:HALO + tm, :] = hn(x_ref[...])
        h_ref[HALO + tm:, :] = hn(xn_ref[...])
        acc_ref[...] = jnp.zeros_like(acc_ref)
        up(bufs[0])

    for parity in range(2):
        @pl.when((j > 0) & (j < nf) & (j % 2 == parity))
        def _():
            up(bufs[parity])
            down(bufs[1 - parity])

    @pl.when(j == nf)
    def _():
        down(bufs[(nf - 1) % 2])
        o_ref[...] = x_ref[...] + gate_ref[...] * acc_ref[...]


def _ffn_call(x, mods, layer, norm_g, w_up, conv_w, conv_b, w_down, geom):
    rows, d = x.shape
    dff = w_down.shape[1]
    tm = geom.row_tile(ROW_TILE)
    tf = FF_TILE
    nf = dff // tf
    grp = geom.group(tm)
    nhalo = rows // HALO
    cur = lambda j: jnp.minimum(j, nf - 1)
    prv = lambda j: jnp.maximum(j - 1, 0)
    return pl.pallas_call(
        functools.partial(_ffn_kernel, geom, nf),
        grid=(rows // tm, nf + 1),
        in_specs=[pl.BlockSpec((tm, d), lambda i, j: (i, 0)),
                  pl.BlockSpec((HALO, d), lambda i, j: (jnp.maximum(i * (tm // HALO) - 1, 0), 0)),
                  pl.BlockSpec((HALO, d), lambda i, j: (jnp.minimum((i + 1) * (tm // HALO), nhalo - 1), 0)),
                  pl.BlockSpec((None, 1, d), lambda i, j: (2 * layer + 1, 0, 0)),
                  _mod_spec(d, layer, 3, grp), _mod_spec(d, layer, 4, grp), _mod_spec(d, layer, 5, grp),
                  pl.BlockSpec((None, d, tf), lambda i, j: (layer, 0, cur(j))),
                  pl.BlockSpec((None, d, tf), lambda i, j: (layer, 0, nf + cur(j))),
                  pl.BlockSpec((None, 3, tf), lambda i, j: (layer, 0, prv(j))),
                  pl.BlockSpec((None, 3, tf), lambda i, j: (layer, 0, nf + prv(j))),
                  pl.BlockSpec((None, 1, tf), lambda i, j: (layer, 0, prv(j))),
                  pl.BlockSpec((None, 1, tf), lambda i, j: (layer, 0, nf + prv(j))),
                  pl.BlockSpec((None, tf, d), lambda i, j: (layer, prv(j), 0))],
        out_specs=pl.BlockSpec((tm, d), lambda i, j: (i, 0)),
        out_shape=jax.ShapeDtypeStruct((rows, d), F32),
        scratch_shapes=[pltpu.VMEM((tm + 2 * HALO, d), BF16), pltpu.VMEM((tm, d), F32)]
        + [pltpu.VMEM((tm + 2 * HALO, tf), F32) for _ in range(4)],
        compiler_params=_params(("arbitrary", "arbitrary")),
        name="conv_ffn",
    )(x, x, x, norm_g, mods, mods, mods, w_up, w_up, conv_w, conv_w, conv_b, conv_b, w_down)


def _final_kernel(x_ref, g_ref, o_ref):
    o_ref[...] = _rms(x_ref[...], g_ref[...])


def _final_call(x, g, row0, nrows):
    d = x.shape[1]
    tm = OUT_ROW_TILE
    while nrows % tm or row0 % tm:
        tm //= 2
    return pl.pallas_call(
        _final_kernel,
        grid=(nrows // tm,),
        in_specs=[pl.BlockSpec((tm, d), lambda i: (row0 // tm + i, 0)), pl.BlockSpec((1, d), lambda i: (0, 0))],
        out_specs=pl.BlockSpec((tm, d), lambda i: (i, 0)),
        out_shape=jax.ShapeDtypeStruct((nrows, d), F32),
        compiler_params=_params(("arbitrary",)),
        name="final_norm",
    )(x, g.reshape(1, d))


def _ssd_kernel(fwd, geom, heads, *refs):
    (xs_ref, xsp_ref, xsn_ref, bc_ref, bcp_ref, bcn_ref, dt_ref, cwx_ref, cbx_ref, cwb_ref, cbb_ref,
     dtb_ref, alog_ref, pe_ref, bd_ref, h0_ref) = refs[:16]
    if fwd:
        z_ref, yb_ref, dx_ref, gn_ref, out_ref, st_ref, s_ref, yacc_ref = refs[16:]
    else:
        out_ref, st_ref, s_ref = refs[16:]
    q = xs_ref.shape[0]
    seq, cpos, nc, c, _ = geom.chunk_pos(pl.program_id(0), q, fwd)
    first, last = c == 0, c == nc - 1
    half = LANES // 2

    @pl.when(cpos == 0)
    def _():
        s_ref[...] = jnp.where(seq < geom.n_ctx, 0.0, h0_ref[...])

    xs = _conv3_silu(xs_ref[...], xsp_ref[...], xsn_ref[...], cwx_ref[...], cbx_ref[...], first, last)
    bc = _conv3_silu(bc_ref[...], bcp_ref[...], bcn_ref[...], cwb_ref[...], cbb_ref[...], first, last)
    bm, cm = bc[:, :LANES], bc[:, LANES:]
    dt = _softplus(dt_ref[...] + dtb_ref[...])
    la = dt * (-jnp.exp(alog_ref[...]))
    row = lax.broadcasted_iota(jnp.int32, (q, q), 0)
    col = lax.broadcasted_iota(jnp.int32, (q, q), 1)
    tril, triu = row >= col, row <= col
    cs_p = _mask_sum(tril.astype(F32).astype(BF16), la)
    cs_s = _mask_sum(triu.astype(F32).astype(BF16), la)
    cs = cs_p if fwd else cs_s
    edge = q - 1 if fwd else 0
    ecs = jnp.exp(cs)
    ecs_x = _expand(ecs, pe_ref[...], pieces=2)
    wd_x = _expand(dt * jnp.exp(cs[edge:edge + 1, :] - cs), pe_ref[...], pieces=2)
    carry = _expand(jnp.broadcast_to(ecs[edge:edge + 1, :], (HALO, LANES)), pe_ref[...])[0:1, :]
    s_old = s_ref[...]
    y_off = _dot(cm, s_old) * ecs_x
    s_new = (s_old * carry + _dot_tn(bm, xs * wd_x)) * bd_ref[...]
    s_ref[...] = s_new

    @pl.when(cpos == nc - 1)
    def _():
        s_t = s_new.T
        p_dim = s_t.shape[0] // heads
        for h in range(heads):
            grp = h // (heads // 2)
            st_ref[h] = s_t[h * p_dim:(h + 1) * p_dim, grp * half:(grp + 1) * half]

    if not fwd:
        out_ref[...] = y_off
        return

    c2_p, c2_s = cs_p * LOG2E, cs_s * LOG2E
    ldt = jnp.log2(dt)
    rt_p, rt_s = (c2_p - ldt).T, (c2_s - ldt).T
    neg = -jnp.inf
    lane = lax.broadcasted_iota(jnp.int32, (q, LANES), 1)
    lo, hi = lane < half, lane >= half
    gmat = (_dot_nt(jnp.where(lo, cm, 0.0), bm), _dot_nt(jnp.where(hi, cm, 0.0), bm))
    for p in range(heads // 2):
        xp = xs[:, p * LANES:(p + 1) * LANES]
        acc = None
        for hh in range(2):
            h = 2 * p + hh
            hb = heads + h
            lf = jnp.exp2(jnp.where(tril, c2_p[:, h:h + 1] - rt_p[h:h + 1, :], neg))
            lb = jnp.exp2(jnp.where(triu, c2_s[:, hb:hb + 1] - rt_s[hb:hb + 1, :], neg))
            w = gmat[h // (heads // 2)] * (lf + lb)
            part = _dot(w, jnp.where(lo if hh == 0 else hi, xp, 0.0))
            acc = part if acc is None else acc + part
        yacc_ref[:, p * LANES:(p + 1) * LANES] = acc
    y = yacc_ref[...] + y_off + yb_ref[...] + dx_ref[...] * xs
    yz = y * _silu(z_ref[...])
    out_ref[...] = _rms(yz, gn_ref[...]).astype(BF16)


def _ssd_call(fwd, proj, yb, prm, h0, geom, heads, width):
    rows = proj.shape[0]
    q = geom.scan_chunk()
    nhalo = rows // HALO
    bc_w = 2 * LANES
    xs_blk, z_blk, bc_blk, dt_blk = 0, 1, 5 * width // bc_w, (5 * width + bc_w) // LANES

    def seq(t):
        return geom.chunk_pos(t, q, fwd)[0]

    def rowblk(t):
        return geom.chunk_pos(t, q, fwd)[4]

    def prev(t):
        return jnp.maximum(rowblk(t) * (q // HALO) - 1, 0)

    def nxt(t):
        return jnp.minimum((rowblk(t) + 1) * (q // HALO), nhalo - 1)

    const = lambda t: (0, 0)
    in_specs = [pl.BlockSpec((q, width), lambda t: (rowblk(t), xs_blk)),
                pl.BlockSpec((HALO, width), lambda t: (prev(t), xs_blk)),
                pl.BlockSpec((HALO, width), lambda t: (nxt(t), xs_blk)),
                pl.BlockSpec((q, bc_w), lambda t: (rowblk(t), bc_blk)),
                pl.BlockSpec((HALO, bc_w), lambda t: (prev(t), bc_blk)),
                pl.BlockSpec((HALO, bc_w), lambda t: (nxt(t), bc_blk)),
                pl.BlockSpec((q, LANES), lambda t: (rowblk(t), dt_blk)),
                pl.BlockSpec((3, width), const), pl.BlockSpec((1, width), const),
                pl.BlockSpec((3, bc_w), const), pl.BlockSpec((1, bc_w), const),
                pl.BlockSpec((1, LANES), const), pl.BlockSpec((1, LANES), const),
                pl.BlockSpec((LANES, width), const), pl.BlockSpec((LANES, width), const),
                pl.BlockSpec((None, LANES, width), lambda t: (jnp.maximum(seq(t) - geom.n_ctx, 0), 0, 0))]
    args = [proj, proj, proj, proj, proj, proj, proj, prm["cwx"], prm["cbx"], prm["cwb"], prm["cbb"],
            prm["dtb"], prm["alog"], prm["pf"] if fwd else prm["pb"], prm["bd"], h0]
    scratch = [pltpu.VMEM((LANES, width), F32)]
    if fwd:
        in_specs += [pl.BlockSpec((q, width), lambda t: (rowblk(t), z_blk)),
                     pl.BlockSpec((q, width), lambda t: (rowblk(t), 0)),
                     pl.BlockSpec((1, width), const), pl.BlockSpec((1, width), const)]
        args += [proj, yb, prm["dx"], prm["gn"]]
        scratch.append(pltpu.VMEM((q, width), F32))
        out_dtype = BF16
    else:
        out_dtype = F32
    return pl.pallas_call(
        functools.partial(_ssd_kernel, fwd, geom, heads),
        grid=(rows // q,),
        in_specs=in_specs,
        out_specs=[pl.BlockSpec((q, width), lambda t: (rowblk(t), 0)),
                   pl.BlockSpec((None, heads, width // heads, LANES // 2), lambda t: (seq(t), 0, 0, 0))],
        out_shape=[jax.ShapeDtypeStruct((rows, width), out_dtype),
                   jax.ShapeDtypeStruct((geom.n_ctx + geom.n_lat, heads, width // heads, LANES // 2), F32)],
        scratch_shapes=scratch,
        compiler_params=_params(("arbitrary",)),
        name="ssd_fwd" if fwd else "ssd_bwd",
    )(*args)


def _rope(x, cos, sin_signed):
    lane = lax.broadcasted_iota(jnp.int32, x.shape, 1)
    partner = jnp.where((lane & 31) < 16, pltpu.roll(x, LANES - 16, 1), pltpu.roll(x, 16, 1))
    return x * cos + partner * sin_signed


def _attn_kernel(lam_init, hps, q_ref, k_ref, v_ref, kc_ref, vc_ref, cq_ref, sq_ref, ck_ref, sk_ref, lp_ref, g_ref,
                 o_ref, kr_ref, vr_ref):
    hs = range(hps)
    sls = [slice(h * LANES, (h + 1) * LANES) for h in hs]

    @pl.when(pl.program_id(2) == 0)
    def _():
        for sl in sls:
            kr_ref[:, sl] = _rope(k_ref[:, sl], ck_ref[...], sk_ref[...]).astype(BF16)
        vr_ref[...] = v_ref[...].astype(BF16)

    scale = float(LANES // 2) ** -0.5 * LOG2E
    lane = lax.broadcasted_iota(jnp.int32, (q_ref.shape[0], LANES), 1)
    lo = lane < LANES // 2
    lp = lp_ref[...]
    lam = (jnp.exp(jnp.sum(lp[0:1] * lp[1:2], axis=1, keepdims=True))
           - jnp.exp(jnp.sum(lp[2:3] * lp[3:4], axis=1, keepdims=True)) + lam_init)
    q = [_rope(q_ref[:, sl], cq_ref[...], sq_ref[...]) * scale for sl in sls]
    qm = [(jnp.where(lo, q[h], 0.0).astype(BF16), jnp.where(lo, 0.0, q[h]).astype(BF16)) for h in hs]
    nt = (((1,), (1,)), ((), ()))
    s_own = [[lax.dot_general(qm[h][m], kr_ref[:, sls[h]], nt, preferred_element_type=F32) for m in range(2)]
             for h in hs]
    s_ctx = [[lax.dot_general(qm[h][m], kc_ref[:, sls[h]].astype(BF16), nt, preferred_element_type=F32)
              for m in range(2)] for h in hs]
    pd_own, pd_ctx = [], []
    for h in hs:
        po, pc, coef = [], [], []
        for m in range(2):
            mx = jnp.maximum(jnp.max(s_own[h][m], axis=1, keepdims=True), jnp.max(s_ctx[h][m], axis=1, keepdims=True))
            po.append(jnp.exp2(s_own[h][m] - mx))
            pc.append(jnp.exp2(s_ctx[h][m] - mx))
            den = jnp.sum(po[m], axis=1, keepdims=True) + jnp.sum(pc[m], axis=1, keepdims=True)
            coef.append(1.0 / den if m == 0 else lam / den)
        pd_own.append((po[0] * coef[0] - po[1] * coef[1]).astype(BF16))
        pd_ctx.append((pc[0] * coef[0] - pc[1] * coef[1]).astype(BF16))
    o = [jnp.dot(pd_own[h], vr_ref[:, sls[h]], preferred_element_type=F32)
         + jnp.dot(pd_ctx[h], vc_ref[:, sls[h]].astype(BF16), preferred_element_type=F32) for h in hs]
    for h in hs:
        o_ref[:, sls[h]] = (_rms(o[h], g_ref[...]) * (1.0 - lam_init)).astype(BF16)


def _attn_ctx_kernel(lam_init, heads, q_ref, k_ref, v_ref, lp_ref, g_ref, o_ref, ko_ref, vo_ref):
    for h in range(heads):
        ko_ref[:, h, :] = k_ref[:, h * LANES:(h + 1) * LANES]
        vo_ref[:, h, :] = v_ref[:, h * LANES:(h + 1) * LANES]
    rows = q_ref.shape[0]
    lane = lax.broadcasted_iota(jnp.int32, (rows, LANES), 1)
    lo = lane < LANES // 2
    scale = float(LANES // 2) ** -0.5 * LOG2E
    lp = lp_ref[...]
    lam = (jnp.exp(jnp.sum(lp[0:1] * lp[1:2], axis=1, keepdims=True))
           - jnp.exp(jnp.sum(lp[2:3] * lp[3:4], axis=1, keepdims=True)) + lam_init)
    hs = range(heads)
    sls = [slice(h * LANES, (h + 1) * LANES) for h in hs]
    q = [q_ref[:, sl] * scale for sl in sls]
    k = [k_ref[:, sl].astype(BF16) for sl in sls]
    s0 = [_dot_nt(jnp.where(lo, q[h], 0.0), k[h]) for h in hs]
    s1 = [_dot_nt(jnp.where(lo, 0.0, q[h]), k[h]) for h in hs]
    pd = []
    for h in hs:
        p0 = jnp.exp2(s0[h] - jnp.max(s0[h], axis=1, keepdims=True))
        p1 = jnp.exp2(s1[h] - jnp.max(s1[h], axis=1, keepdims=True))
        c0 = 1.0 / jnp.sum(p0, axis=1, keepdims=True)
        c1 = lam / jnp.sum(p1, axis=1, keepdims=True)
        pd.append(p0 * c0 - p1 * c1)
    o = [_dot(pd[h], v_ref[:, sls[h]]) for h in hs]
    for h in hs:
        o_ref[:, sls[h]] = (_rms(o[h], g_ref[...]) * (1.0 - lam_init)).astype(BF16)


def _attn_ctx_call(lam_init, proj, lam_p, norm_g, n_seq, seq_len, heads, width):
    spec = lambda blk: pl.BlockSpec((seq_len, width), lambda b: (b, blk))
    return pl.pallas_call(
        functools.partial(_attn_ctx_kernel, lam_init, heads),
        grid=(n_seq,),
        in_specs=[spec(2), spec(3), spec(4), pl.BlockSpec(lam_p.shape, lambda b: (0, 0)),
                  pl.BlockSpec((1, LANES), lambda b: (0, 0))],
        out_specs=[pl.BlockSpec((seq_len, width), lambda b: (b, 0)),
                   pl.BlockSpec((None, seq_len, heads, LANES), lambda b: (b, 0, 0, 0)),
                   pl.BlockSpec((None, seq_len, heads, LANES), lambda b: (b, 0, 0, 0))],
        out_shape=[jax.ShapeDtypeStruct((n_seq * seq_len, width), BF16),
                   jax.ShapeDtypeStruct((n_seq, seq_len, heads, LANES), F32),
                   jax.ShapeDtypeStruct((n_seq, seq_len, heads, LANES), F32)],
        compiler_params=_params(("arbitrary",)),
        name="diff_attn_ctx",
    )(proj, proj, proj, lam_p, norm_g.reshape(1, LANES))


def _attn_call(lam_init, proj, cache_k, cache_v, tables, lam_p, norm_g, row0, n_seq, seq_len, heads, width):
    tq = min(Q_TILE, seq_len)
    nq = seq_len // tq
    hps = ATTN_HEADS_PER_STEP
    hw = hps * LANES
    ngrp = heads // hps
    q_blk, k_blk, v_blk = 2 * ngrp, 3 * ngrp, 4 * ngrp
    qb, sb = row0 // tq, row0 // seq_len
    past = cache_k.shape[1]
    const = lambda b, h, i: (0, 0)
    in_specs = [pl.BlockSpec((tq, hw), lambda b, h, i: (qb + b * nq + i, q_blk + h)),
                pl.BlockSpec((seq_len, hw), lambda b, h, i: (sb + b, k_blk + h)),
                pl.BlockSpec((seq_len, hw), lambda b, h, i: (sb + b, v_blk + h)),
                pl.BlockSpec((None, past, hw), lambda b, h, i: (b, 0, h)),
                pl.BlockSpec((None, past, hw), lambda b, h, i: (b, 0, h)),
                pl.BlockSpec((tq, LANES), lambda b, h, i: (i, 0)),
                pl.BlockSpec((tq, LANES), lambda b, h, i: (i, 0)),
                pl.BlockSpec((seq_len, LANES), const), pl.BlockSpec((seq_len, LANES), const),
                pl.BlockSpec(lam_p.shape, const), pl.BlockSpec((1, LANES), const)]
    return pl.pallas_call(
        functools.partial(_attn_kernel, lam_init, hps),
        grid=(n_seq, ngrp, nq),
        in_specs=in_specs,
        out_specs=pl.BlockSpec((tq, hw), lambda b, h, i: (b * nq + i, h)),
        out_shape=jax.ShapeDtypeStruct((n_seq * seq_len, width), BF16),
        scratch_shapes=[pltpu.VMEM((seq_len, hw), BF16), pltpu.VMEM((seq_len, hw), BF16)],
        compiler_params=_params(("arbitrary", "arbitrary", "arbitrary")),
        name="diff_attn_lat",
    )(proj, proj, proj, cache_k, cache_v, tables[0], tables[1], tables[0], tables[1], lam_p,
      norm_g.reshape(1, LANES))


def _rope_tables(n_tokens, grid_w):
    n_freq = LANES // 8
    pos = jnp.arange(n_tokens)
    r = (pos // grid_w).astype(F32)
    cpos = (pos % grid_w).astype(F32)
    inv = ROPE_BASE ** (-jnp.arange(n_freq, dtype=F32) / n_freq)
    ang_r, ang_c = r[:, None] * inv, cpos[:, None] * inv
    cos32 = lambda a: jnp.concatenate([jnp.cos(a), jnp.cos(a)], axis=1)
    sin32 = lambda a: jnp.concatenate([-jnp.sin(a), jnp.sin(a)], axis=1)
    cos64 = jnp.concatenate([cos32(ang_r), cos32(ang_c)], axis=1)
    sin64 = jnp.concatenate([sin32(ang_r), sin32(ang_c)], axis=1)
    return jnp.concatenate([cos64, cos64], axis=1), jnp.concatenate([sin64, sin64], axis=1)


def _gdn_masks(rb, sb, fwd):
    shift = GDN_CHUNK.bit_length() - 1

    def tri(n):
        row = jnp.arange(n)[:, None]
        col = jnp.arange(n)[None, :]
        blk = (row >> shift) == (col >> shift)
        return row, col, blk, blk & ((row >= col) if fwd else (row <= col))

    row, col, blk, incl = tri(sb)
    strict = blk & ((row > col) if fwd else (row < col))
    ms = [incl, strict, row == col, (row >> 1) == (col >> 1)]
    for lvl in range(1, shift):
        ms.append(((row >> (lvl + 1)) == (col >> (lvl + 1))) & ((row >> lvl) != (col >> lvl)))
    _, _, blk_rb, incl_rb = tri(rb)
    return jnp.stack(ms).astype(F32), jnp.stack([incl_rb, blk_rb]).astype(BF16)


def _gdn_kernel(fwd, geom, heads, hps, *refs):
    (q_ref, qp_ref, qn_ref, k_ref, kp_ref, kn_ref, v_ref, vp_ref, vn_ref, ab_ref,
     cwq_ref, cbq_ref, cwk_ref, cbk_ref, cwv_ref, cbv_ref, alog_ref, dtb_ref, mf_ref, mb_ref, s0_ref) = refs[:21]
    if fwd:
        z_ref, ob_ref, gn_ref, out_ref, st_ref, s_ref = refs[21:]
    else:
        out_ref, st_ref, s_ref = refs[21:]
    hg = pl.program_id(0)
    rb = q_ref.shape[0]
    seq, cpos, nb, c, _ = geom.chunk_pos(pl.program_id(1), rb, fwd)
    first, last = c == 0, c == nb - 1
    dk = q_ref.shape[1] // hps
    ch = GDN_CHUNK
    nsub = rb // ch
    n_lvl = ch.bit_length() - 1

    @pl.when(cpos == 0)
    def _():
        s_ref[...] = jnp.where(seq < geom.n_ctx, 0.0, s0_ref[...])

    ab = ab_ref[...]
    d0 = 0 if fwd else 2 * heads
    g_all = -jnp.exp(alog_ref[...]) * _softplus(ab + dtb_ref[...])
    b_all = 1.0 / (1.0 + jnp.exp(-ab))
    gc_all = _mask_sum(mb_ref[0], g_all)
    gt_all = _mask_sum(mb_ref[1], g_all)
    gct_all = gc_all.T
    lane = lax.broadcasted_iota(jnp.int32, ab.shape, 1)
    sub = lax.broadcasted_iota(jnp.int32, gct_all.shape, 0)
    incl_f, strict_f, eye, pair = mf_ref[0], mf_ref[1], mf_ref[2], mf_ref[3]
    sb = mf_ref.shape[1]
    cps = sb // ch

    def pick(a, l):
        return jnp.sum(jnp.where(lane == l, a, 0.0), axis=1, keepdims=True)

    hs = range(hps)
    sls = [slice(hh * dk, (hh + 1) * dk) for hh in hs]
    cs = [(hh, blk) for hh in hs for blk in range(rb // sb)]
    nch = range(len(cs))
    kn, qn, vb, gc, gtot, beta, dec = ([] for _ in range(7))
    for hh in hs:
        hd = hg * hps + hh
        sl = sls[hh]

        def conv(x_ref, xp_ref, xn_ref, cw_ref, cb_ref):
            return _conv3_silu(x_ref[:, sl], xp_ref[:, sl], xn_ref[:, sl], cw_ref[:, sl], cb_ref[:, sl], first, last)

        qc = conv(q_ref, qp_ref, qn_ref, cwq_ref, cbq_ref)
        kc = conv(k_ref, kp_ref, kn_ref, cwk_ref, cbk_ref)
        vh = conv(v_ref, vp_ref, vn_ref, cwv_ref, cbv_ref)
        qh = qc * lax.rsqrt(jnp.sum(qc * qc, axis=1, keepdims=True) + EPS) * (float(dk) ** -0.5)
        kh = kc * lax.rsqrt(jnp.sum(kc * kc, axis=1, keepdims=True) + EPS)
        gch, gth = pick(gc_all, d0 + hd), pick(gt_all, d0 + hd)
        bh = pick(b_all, d0 + heads + hd)
        g_row = jnp.sum(jnp.where(sub == d0 + hd, gct_all, 0.0), axis=0, keepdims=True)
        for blk in range(rb // sb):
            r = slice(blk * sb, (blk + 1) * sb)
            qn.append(qh[r]); kn.append(kh[r]); gc.append(gch[r]); gtot.append(gth[r]); beta.append(bh[r])
            vb.append(vh[r] * bh[r])
            dec.append(jnp.exp(jnp.minimum(gch[r] - g_row[:, r], 0.0)) * incl_f)
    kk = [_dot_nt(kn[c], kn[c]) for c in nch]
    m = [(kk[c] * beta[c]) * (dec[c] * strict_f) for c in nch]
    t = [eye - m[c] * pair for c in nch]
    for lvl in range(1, n_lvl):
        off = mf_ref[3 + lvl]
        tm = [_dot(t[c], m[c] * off) for c in nch]
        tmt = [_dot(tm[c], t[c]) for c in nch]
        t = [t[c] - tmt[c] for c in nch]
    eg = [jnp.exp(gc[c]) for c in nch]
    sol = [_dot(t[c], jnp.concatenate([vb[c], kn[c] * (beta[c] * eg[c])], axis=1)) for c in nch]
    qk = [_dot_nt(qn[c], kn[c]) for c in nch]
    qg = [qn[c] * eg[c] for c in nch]
    kg = [kn[c] * jnp.exp(gtot[c] - gc[c]) for c in nch]
    egl = [jnp.exp(gtot[c]) for c in nch]
    s = [s_ref[hh] for hh in hs]
    vnew = [[None] * cps for _ in nch]
    qs = [[None] * cps for _ in nch]
    for i in (range(nsub) if fwd else reversed(range(nsub))):
        blk, li = divmod(i, cps)
        r = slice(li * ch, (li + 1) * ch)
        idx = [hh * (rb // sb) + blk for hh in hs]
        ws = [_dot(jnp.concatenate([sol[c][r, dk:], qg[c][r]], axis=0), s[hh]) for hh, c in zip(hs, idx)]
        for hh, c in zip(hs, idx):
            vnew[c][li] = sol[c][r, :dk] - ws[hh][:ch]
            qs[c][li] = ws[hh][ch:]
        upd = [_dot_tn(kg[c][r], vnew[c][li]) for c in idx]
        s = [s[hh] * egl[c][li * ch:li * ch + 1, :] + upd[hh] for hh, c in zip(hs, idx)]
    intra = [_dot(qk[c] * dec[c], jnp.concatenate(vnew[c], axis=0)) for c in nch]
    for c, (hh, blk) in enumerate(cs):
        sl = sls[hh]
        r = slice(blk * sb, (blk + 1) * sb)
        o = jnp.concatenate(qs[c], axis=0) + intra[c]
        if fwd:
            o = o + ob_ref[r, sl]
            out_ref[r, sl] = (_rms(o, gn_ref[...]) * _silu(z_ref[r, sl])).astype(BF16)
        else:
            out_ref[r, sl] = o
    for hh in hs:
        s_ref[hh] = s[hh]

    @pl.when(cpos == nb - 1)
    def _():
        st_ref[...] = s_ref[...]


def _gdn_call(fwd, proj, ob, prm, s0, geom, heads, width):
    rows = proj.shape[0]
    rb = geom.scan_chunk()
    nhalo = rows // HALO
    dk = width // heads
    hps = GDN_HEADS_PER_STEP
    hw = hps * dk
    ngrp = heads // hps
    ab_blk = 4 * width // LANES
    mf, mb = _gdn_masks(rb, min(GDN_SYSTEM_ROWS, rb), fwd)

    def seq(t):
        return geom.chunk_pos(t, rb, fwd)[0]

    def rowblk(t):
        return geom.chunk_pos(t, rb, fwd)[4]

    def prev(t):
        return jnp.maximum(rowblk(t) * (rb // HALO) - 1, 0)

    def nxt(t):
        return jnp.minimum((rowblk(t) + 1) * (rb // HALO), nhalo - 1)

    in_specs, args = [], []
    for sec in range(3):
        in_specs += [pl.BlockSpec((rb, hw), lambda h, t, sec=sec: (rowblk(t), sec * ngrp + h)),
                     pl.BlockSpec((HALO, hw), lambda h, t, sec=sec: (prev(t), sec * ngrp + h)),
                     pl.BlockSpec((HALO, hw), lambda h, t, sec=sec: (nxt(t), sec * ngrp + h))]
        args += [proj, proj, proj]
    in_specs.append(pl.BlockSpec((rb, LANES), lambda h, t: (rowblk(t), ab_blk)))
    args.append(proj)
    for sec in range(3):
        in_specs += [pl.BlockSpec((3, hw), lambda h, t, sec=sec: (0, sec * ngrp + h)),
                     pl.BlockSpec((1, hw), lambda h, t, sec=sec: (0, sec * ngrp + h))]
        args += [prm["cw"], prm["cb"]]
    in_specs += [pl.BlockSpec((1, LANES), lambda h, t: (0, 0)), pl.BlockSpec((1, LANES), lambda h, t: (0, 0)),
                 pl.BlockSpec(mf.shape, lambda h, t: (0, 0, 0)), pl.BlockSpec(mb.shape, lambda h, t: (0, 0, 0)),
                 pl.BlockSpec((None, hps, dk, dk), lambda h, t: (jnp.maximum(seq(t) - geom.n_ctx, 0), h, 0, 0))]
    args += [prm["alog"], prm["dtb"], mf, mb, s0]
    if fwd:
        in_specs += [pl.BlockSpec((rb, hw), lambda h, t: (rowblk(t), 3 * ngrp + h)),
                     pl.BlockSpec((rb, hw), lambda h, t: (rowblk(t), h)),
                     pl.BlockSpec((1, dk), lambda h, t: (0, 0))]
        args += [proj, ob, prm["gn"]]
    return pl.pallas_call(
        functools.partial(_gdn_kernel, fwd, geom, heads, hps),
        grid=(ngrp, rows // rb),
        in_specs=in_specs,
        out_specs=[pl.BlockSpec((rb, hw), lambda h, t: (rowblk(t), h)),
                   pl.BlockSpec((None, hps, dk, dk), lambda h, t: (seq(t), h, 0, 0))],
        out_shape=[jax.ShapeDtypeStruct((rows, width), BF16 if fwd else F32),
                   jax.ShapeDtypeStruct((geom.n_ctx + geom.n_lat, heads, dk, dk), F32)],
        scratch_shapes=[pltpu.VMEM((hps, dk, dk), F32)],
        compiler_params=_params(("arbitrary", "arbitrary")),
        name="gdn_fwd" if fwd else "gdn_bwd",
    )(*args)


def _pack_states_kernel(n_in, *refs):
    o_ref = refs[n_in]
    for i in range(n_in):
        o_ref[i] = refs[i][...]


def _pack_states_call(states, n_keep):
    h, a, b = states[0].shape[1:]
    n = len(states)
    return pl.pallas_call(
        functools.partial(_pack_states_kernel, n),
        grid=(n_keep,),
        in_specs=[pl.BlockSpec((None, h, a, b), lambda i: (i, 0, 0, 0))] * n,
        out_specs=pl.BlockSpec((None, n, h, a, b), lambda i: (i, 0, 0, 0, 0)),
        out_shape=jax.ShapeDtypeStruct((n_keep, n, h, a, b), F32),
        compiler_params=_params(("arbitrary",)),
        name="pack_states",
    )(*states)


def _pad_cols(w, n):
    return jnp.pad(w, ((0, 0), (0, n - w.shape[1])))


def _lane_row(v):
    v = v.reshape(1, -1).astype(F32)
    return _pad_cols(v, LANES)


def _even_layer(x, mods, layer, e, geom, norm_g, w_in, conv_w, conv_b, dt_bias, a_log, d_skip, ssd_g,
                lam_p, da_g, w_out, cache_k, cache_v, state_ssd, tables):
    d = x.shape[1]
    heads, p_dim, n_state = state_ssd.shape[3], state_ssd.shape[4], state_ssd.shape[5]
    width = heads * p_dim
    da_heads = cache_k.shape[3]
    assert width == d and da_heads * LANES == d and 2 * n_state == LANES and 2 * p_dim == LANES
    bc_w = 2 * LANES
    n_dt = 2 * heads
    o_z, o_xs, o_bc, o_dt = 0, width, 2 * width, 2 * width + bc_w
    o_q = o_dt + n_dt
    cols = [w_in[:, o_xs:o_xs + width], w_in[:, o_z:o_z + width], w_in[:, o_q:o_q + 3 * d],
            w_in[:, o_bc:o_bc + bc_w], w_in[:, o_dt:o_dt + n_dt]]
    n_used = 5 * width + bc_w + n_dt
    n_pad = -(-n_used // IN_COL_TILE) * IN_COL_TILE
    w_perm = _pad_cols(jnp.concatenate(cols, axis=1), n_pad).astype(BF16)
    proj = _inproj_call(x, mods, layer, norm_g, w_perm, geom)

    hl = jnp.arange(LANES)[:, None]
    cl = jnp.arange(width)[None, :]
    prm = {
        "cwx": conv_w[:, :width], "cbx": conv_b[None, :width],
        "cwb": conv_w[:, width:], "cbb": conv_b[None, width:],
        "dtb": _lane_row(dt_bias), "alog": _lane_row(a_log),
        "pf": (hl == cl // p_dim).astype(BF16), "pb": (hl == heads + cl // p_dim).astype(BF16),
        "bd": (hl // n_state == cl // (width // 2)).astype(F32),
        "dx": jnp.repeat(d_skip, p_dim)[None, :], "gn": ssd_g[None, :],
    }

    def state_in(st):
        t = st.transpose(0, 3, 1, 2).reshape(st.shape[0], n_state, width)
        return jnp.concatenate([t, t], axis=1) * prm["bd"]

    h0f, h0b = state_in(state_ssd[:, e, 0].astype(F32)), state_in(state_ssd[:, e, 1].astype(F32))
    yb, st_b = _ssd_call(False, proj, None, prm, h0b, geom, heads, width)
    y, st_f = _ssd_call(True, proj, yb, prm, h0f, geom, heads, width)
    new_states = [st_f, st_b]
    lam_init = 0.8 - 0.6 * math.exp(-0.3 * layer)
    o_ctx, k_ctx, v_ctx = _attn_ctx_call(lam_init, proj, lam_p, da_g, geom.n_ctx, geom.s_ctx, da_heads, d)
    o_lat = _attn_call(lam_init, proj, cache_k[:, e].reshape(geom.n_lat, -1, d),
                       cache_v[:, e].reshape(geom.n_lat, -1, d), tables, lam_p, da_g, geom.ctx_rows,
                       geom.n_lat, geom.s_lat, da_heads, d)
    w_o = w_out.astype(BF16)
    x = _outproj_call(x, mods, layer, [y, (o_ctx, o_lat)], [w_o[:width], w_o[width:]], geom)
    return x, k_ctx, v_ctx, new_states


def _odd_layer(x, mods, layer, o, geom, norm_g, w_in, conv_w, conv_b, a_log, dt_bias, g_norm, w_out, state_gdn):
    d = x.shape[1]
    heads, dk, dv = state_gdn.shape[3], state_gdn.shape[4], state_gdn.shape[5]
    width = heads * dv
    assert width == d and dk == LANES and dv == LANES and 4 * heads <= LANES
    n_used = 4 * width + 4 * heads
    n_pad = -(-n_used // IN_COL_TILE) * IN_COL_TILE
    w_perm = _pad_cols(w_in, n_pad).astype(BF16)
    proj = _inproj_call(x, mods, layer, norm_g, w_perm, geom)

    def ab_row(v):
        r = jnp.zeros((2, 2 * heads), F32).at[:, :heads].set(v.astype(F32))
        return _lane_row(r)

    prm = {"cw": conv_w, "cb": conv_b[None, :], "alog": ab_row(a_log), "dtb": ab_row(dt_bias), "gn": g_norm[None, :]}
    s0f, s0b = state_gdn[:, o, 0].astype(F32), state_gdn[:, o, 1].astype(F32)
    ob, st_b = _gdn_call(False, proj, None, prm, s0b, geom, heads, width)
    og, st_f = _gdn_call(True, proj, ob, prm, s0f, geom, heads, width)
    new_states = [st_f, st_b]
    x = _outproj_call(x, mods, layer, [og], [w_out.astype(BF16)], geom)
    return x, new_states


def kernel(x_prompt, x_sample, cache_attn_k, cache_attn_v, state_ssd, state_gdn, c, c_ctx, ada_w, ada_b, norm_g, ev_w_in, ev_conv_w, ev_conv_b, ssd_dt_bias, ssd_a_log, ssd_d, ssd_norm_g, da_lambda, da_norm_g, ev_w_out, od_w_in, od_conv_w, od_conv_b, gdn_a_log, gdn_dt_bias, gdn_norm_g, od_w_out, ffn_w_up, ffn_conv_w, ffn_conv_b, ffn_w_down, final_norm_g):
    n_ctx, s_ctx, d = x_prompt.shape
    n_lat, s_lat, _ = x_sample.shape
    depth = ada_w.shape[0]
    assert n_lat + 1 <= MOD_ROWS and s_ctx & (s_ctx - 1) == 0 and s_lat & (s_lat - 1) == 0
    geom = _Geom(n_ctx, s_ctx, n_lat, s_lat)
    dtype = x_prompt.dtype
    x = jnp.concatenate([x_prompt.reshape(-1, d), x_sample.reshape(-1, d)], axis=0).astype(F32)

    cvec = jnp.zeros((MOD_ROWS, d), F32).at[0].set(c_ctx).at[1:1 + n_lat].set(c)
    mods = _mod_call(cvec, ada_w, ada_b)
    mods = mods.reshape(depth, MOD_ROWS, N_MOD, d).transpose(0, 2, 1, 3).reshape(depth * N_MOD * MOD_ROWS, 1, d)

    grid_w = 64
    tables = _rope_tables(s_lat, grid_w)
    ks_out, vs_out, ssd_out, gdn_out = [], [], [], []
    for l in range(depth):
        if l % 2 == 0:
            e = l // 2
            x, k_new, v_new, st = _even_layer(
                x, mods, l, e, geom, norm_g[l, 0], ev_w_in[e], ev_conv_w[e], ev_conv_b[e], ssd_dt_bias[e],
                ssd_a_log[e], ssd_d[e], ssd_norm_g[e], da_lambda[e], da_norm_g[e], ev_w_out[e],
                cache_attn_k, cache_attn_v, state_ssd, tables)
            ks_out.append(k_new)
            vs_out.append(v_new)
            ssd_out += st
        else:
            o = l // 2
            x, st = _odd_layer(x, mods, l, o, geom, norm_g[l, 0], od_w_in[o], od_conv_w[o], od_conv_b[o],
                               gdn_a_log[o], gdn_dt_bias[o], gdn_norm_g[o], od_w_out[o], state_gdn)
            gdn_out += st
        x = _ffn_call(x, mods, l, norm_g.reshape(2 * depth, 1, d), ffn_w_up, ffn_conv_w,
                      ffn_conv_b.reshape(depth, 1, -1), ffn_w_down, geom)
    y_prompt = _final_call(x, final_norm_g, 0, geom.ctx_rows).reshape(n_ctx, s_ctx, d)
    y_sample = _final_call(x, final_norm_g, geom.ctx_rows, n_lat * s_lat).reshape(n_lat, s_lat, d)
    def states(parts):
        packed = _pack_states_call(parts, n_ctx)
        return packed.reshape((n_ctx, len(parts) // 2, 2) + packed.shape[2:])

    return (y_prompt.astype(dtype), y_sample.astype(dtype),
            jnp.stack(ks_out, axis=1).astype(dtype), jnp.stack(vs_out, axis=1).astype(dtype),
            states(ssd_out).astype(dtype), states(gdn_out).astype(dtype))
```

```python
import functools
import math

import jax
import jax.numpy as jnp
from jax import lax
from jax.experimental import pallas as pl
from jax.experimental.pallas import tpu as pltpu

F32 = jnp.float32
BF16 = jnp.bfloat16
EPS = 1e-6
ROPE_BASE = 10000.0
LOG2E = 1.4426950408889634

LANES = 128
HALO = 8
VMEM_LIMIT_BYTES = 56 * 1024 * 1024

N_MOD = 6
MOD_ROWS = 8
SCAN_CHUNK = 256
GDN_CHUNK = 64
GDN_HEADS_PER_STEP = 8
GDN_SYSTEM_ROWS = 128
Q_TILE = 256
ATTN_HEADS_PER_STEP = 2
ROW_TILE = 1024
OUT_ROW_TILE = 512
FFN_ROW_TILE = 1024
FF_TILE = 256
FFN_ROW_CHUNK = 256
IN_COL_TILE = 1408


def _silu(x):
    return x / (1.0 + jnp.exp(-x))


def _softplus(x):
    return jnp.maximum(x, 0.0) + jnp.log1p(jnp.exp(-jnp.abs(x)))


def _dot(a, b):
    return jnp.dot(a.astype(BF16), b.astype(BF16), preferred_element_type=F32)


def _dot_nt(a, b):
    return lax.dot_general(a.astype(BF16), b.astype(BF16), (((1,), (1,)), ((), ())),
                           preferred_element_type=F32)


def _dot_tn(a, b):
    return lax.dot_general(a.astype(BF16), b.astype(BF16), (((0,), (0,)), ((), ())),
                           preferred_element_type=F32)


def _split3(a):
    hi = a.astype(BF16)
    r1 = a - hi.astype(F32)
    mid = r1.astype(BF16)
    return hi, mid, (r1 - mid.astype(F32)).astype(BF16)


def _expand(a, p, pieces=3):
    d = functools.partial(jnp.dot, preferred_element_type=F32)
    hi, mid, lo = _split3(a)
    out = d(hi, p) + d(mid, p)
    return out + d(lo, p) if pieces == 3 else out


def _mask_sum(m, a):
    d = functools.partial(jnp.dot, preferred_element_type=F32)
    hi, mid, lo = _split3(a)
    return d(m, hi) + d(m, mid) + d(m, lo)


def _rms(x, g):
    ms = jnp.mean(x * x, axis=-1, keepdims=True)
    return x * lax.rsqrt(ms + EPS) * g


def _params(sem):
    return pltpu.CompilerParams(dimension_semantics=sem, vmem_limit_bytes=VMEM_LIMIT_BYTES)


def _conv3_silu(x, xp, xn, w, b, first, last):
    rows = x.shape[0]
    row = lax.broadcasted_iota(jnp.int32, x.shape, 0)
    prev_row = jnp.where(first, 0.0, xp[HALO - 1:HALO, :])
    next_row = jnp.where(last, 0.0, xn[0:1, :])
    x_prev = jnp.where(row == 0, prev_row, pltpu.roll(x, 1, 0))
    x_next = jnp.where(row == rows - 1, next_row, pltpu.roll(x, rows - 1, 0))
    y = x_prev * w[0:1, :] + x * w[1:2, :] + x_next * w[2:3, :] + b
    return _silu(y)


def _mod_kernel(c_ref, w_ref, b_ref, o_ref):
    s = _silu(c_ref[...])
    o_ref[...] = _dot(s, w_ref[...]) + b_ref[...]


def _mod_call(cvec, ada_w, ada_b):
    depth, d, n = ada_w.shape
    tn = d
    return pl.pallas_call(
        _mod_kernel,
        grid=(depth, n // tn),
        in_specs=[pl.BlockSpec((MOD_ROWS, d), lambda l, j: (0, 0)),
                  pl.BlockSpec((None, d, tn), lambda l, j: (l, 0, j)),
                  pl.BlockSpec((None, 1, tn), lambda l, j: (l, 0, j))],
        out_specs=pl.BlockSpec((None, MOD_ROWS, tn), lambda l, j: (l, 0, j)),
        out_shape=jax.ShapeDtypeStruct((depth, MOD_ROWS, n), F32),
        compiler_params=_params(("arbitrary", "arbitrary")),
        name="ada_mod",
    )(cvec, ada_w, ada_b.reshape(depth, 1, n))


class _Geom:
    def __init__(self, n_ctx, s_ctx, n_lat, s_lat):
        self.n_ctx, self.s_ctx, self.n_lat, self.s_lat = n_ctx, s_ctx, n_lat, s_lat
        self.ctx_rows = n_ctx * s_ctx
        self.rows = self.ctx_rows + n_lat * s_lat

    def row_tile(self, want):
        t = want
        while self.ctx_rows % t or self.s_lat % t:
            t //= 2
        return t

    def scan_chunk(self):
        return min(SCAN_CHUNK, self.s_ctx, self.s_lat)

    def chunk_pos(self, t, q, fwd):
        ctx_blocks = self.ctx_rows // q
        nc_ctx, nc_lat = self.s_ctx // q, self.s_lat // q
        is_ctx = t < ctx_blocks
        tl = jnp.maximum(t - ctx_blocks, 0)
        seq = jnp.where(is_ctx, t >> (nc_ctx.bit_length() - 1), self.n_ctx + (tl >> (nc_lat.bit_length() - 1)))
        cpos = jnp.where(is_ctx, t & (nc_ctx - 1), tl & (nc_lat - 1))
        nc = jnp.where(is_ctx, nc_ctx, nc_lat)
        c = cpos if fwd else nc - 1 - cpos
        return seq, cpos, nc, c, t - cpos + c

    def group(self, tile):
        def fn(i):
            r = i * tile
            return jnp.where(r < self.ctx_rows, 0,
                             1 + (jnp.maximum(r - self.ctx_rows, 0) >> (self.s_lat.bit_length() - 1)))
        return fn


def _mod_spec(d, layer, which, group_fn):
    return pl.BlockSpec((None, 1, d), lambda i, *_: ((layer * N_MOD + which) * MOD_ROWS + group_fn(i), 0, 0))


def _inproj_kernel(x_ref, g_ref, sh_ref, sc_ref, w_ref, o_ref, h_ref):
    @pl.when(pl.program_id(1) == 0)
    def _():
        h_ref[...] = (_rms(x_ref[...], g_ref[...]) * (1.0 + sc_ref[...]) + sh_ref[...]).astype(BF16)

    o_ref[...] = jnp.dot(h_ref[...], w_ref[...], preferred_element_type=F32)


def _inproj_call(x, mods, layer, norm_g, w, geom):
    rows, d = x.shape
    n = w.shape[1]
    tm = geom.row_tile(ROW_TILE)
    tn = IN_COL_TILE
    grp = geom.group(tm)
    return pl.pallas_call(
        _inproj_kernel,
        grid=(rows // tm, n // tn),
        in_specs=[pl.BlockSpec((tm, d), lambda i, j: (i, 0)),
                  pl.BlockSpec((1, d), lambda i, j: (0, 0)),
                  _mod_spec(d, layer, 0, grp),
                  _mod_spec(d, layer, 1, grp),
                  pl.BlockSpec((d, tn), lambda i, j: (0, j))],
        out_specs=pl.BlockSpec((tm, tn), lambda i, j: (i, j)),
        out_shape=jax.ShapeDtypeStruct((rows, n), F32),
        scratch_shapes=[pltpu.VMEM((tm, d), BF16)],
        compiler_params=_params(("arbitrary", "arbitrary")),
        name="in_proj",
    )(x, norm_g.reshape(1, d), mods, mods, w)


def _outproj_kernel(split, ctx_tiles, x_ref, gate_ref, *refs):
    n_a = sum(2 if s else 1 for s in split)
    a_refs, w_refs, o_ref = list(refs[:n_a]), refs[n_a:n_a + len(split)], refs[n_a + len(split)]
    is_ctx = pl.program_id(0) < ctx_tiles
    acc = None
    for s, w_ref in zip(split, w_refs):
        a = a_refs.pop(0)[...]
        if s:
            a = jnp.where(is_ctx, a, a_refs.pop(0)[...])
        part = jnp.dot(a, w_ref[...], preferred_element_type=F32)
        acc = part if acc is None else acc + part
    o_ref[...] = x_ref[...] + gate_ref[...] * acc


def _outproj_call(x, mods, layer, acts, ws, geom):
    rows, d = x.shape
    tm = geom.row_tile(OUT_ROW_TILE)
    grp = geom.group(tm)
    ctx_tiles = geom.ctx_rows // tm
    split = tuple(isinstance(a, tuple) for a in acts)
    in_specs = [pl.BlockSpec((tm, d), lambda i: (i, 0)), _mod_spec(d, layer, 2, grp)]
    flat = []
    for a in acts:
        if isinstance(a, tuple):
            in_specs += [pl.BlockSpec((tm, a[0].shape[1]), lambda i: (jnp.minimum(i, ctx_tiles - 1), 0)),
                         pl.BlockSpec((tm, a[1].shape[1]), lambda i: (jnp.maximum(i - ctx_tiles, 0), 0))]
            flat += list(a)
        else:
            in_specs.append(pl.BlockSpec((tm, a.shape[1]), lambda i: (i, 0)))
            flat.append(a)
    in_specs += [pl.BlockSpec(w.shape, lambda i: (0, 0)) for w in ws]
    return pl.pallas_call(
        functools.partial(_outproj_kernel, split, ctx_tiles),
        grid=(rows // tm,),
        in_specs=in_specs,
        out_specs=pl.BlockSpec((tm, d), lambda i: (i, 0)),
        out_shape=jax.ShapeDtypeStruct((rows, d), F32),
        compiler_params=_params(("arbitrary",)),
        name="out_proj",
    )(x, mods, *flat, *ws)


def _ffn_kernel(geom, nf, x_ref, xp_ref, xn_ref, g_ref, sh_ref, sc_ref, gate_ref,
                wg_ref, wv_ref, cwg_ref, cwv_ref, cbg_ref, cbv_ref, wd_ref, o_ref, h_ref, acc_ref, *u_refs):
    i, j = pl.program_id(0), pl.program_id(1)
    tm = x_ref.shape[0]
    bufs = (u_refs[0:2], u_refs[2:4])

    def up(buf):
        h = h_ref[...]
        tf = wg_ref.shape[1]
        w = jnp.concatenate([wg_ref[...].astype(BF16), wv_ref[...].astype(BF16)], axis=1)
        u = jnp.dot(h, w, preferred_element_type=F32)
        buf[0][...] = u[:, :tf]
        buf[1][...] = u[:, tf:]

    def down(buf):
        rc = FFN_ROW_CHUNK
        row0 = i * tm
        is_ctx = row0 < geom.ctx_rows
        base = jnp.where(is_ctx, row0, row0 - geom.ctx_rows)
        lseq = jnp.where(is_ctx, geom.s_ctx, geom.s_lat)
        rowi = lax.broadcasted_iota(jnp.int32, (rc, LANES), 0)
        wd = wd_ref[...].astype(BF16)
        for r in range(0, tm, rc):
            at_start = (rowi == 0) & (((base + r) & (lseq - 1)) == 0)
            at_end = (rowi == rc - 1) & (((base + r + rc) & (lseq - 1)) == 0)

            def conv(u_ref, cw_ref, cb_ref):
                cols = []
                for c in range(0, u_ref.shape[1], LANES):
                    cs = slice(c, c + LANES)
                    up_ = jnp.where(at_start, 0.0, u_ref[HALO + r - 1:HALO + r - 1 + rc, cs])
                    un = jnp.where(at_end, 0.0, u_ref[HALO + r + 1:HALO + r + 1 + rc, cs])
                    cols.append(up_ * cw_ref[0:1, cs] + u_ref[HALO + r:HALO + r + rc, cs] * cw_ref[1:2, cs]
                                + un * cw_ref[2:3, cs] + cb_ref[:, cs])
                return jnp.concatenate(cols, axis=1)

            a = _silu(conv(buf[0], cwg_ref, cbg_ref)) * conv(buf[1], cwv_ref, cbv_ref)
            acc_ref[r:r + rc, :] += jnp.dot(a.astype(BF16), wd, preferred_element_type=F32)

    @pl.when(j == 0)
    def _():
        def hn(x):
            return (_rms(x, g_ref[...]) * (1.0 + sc_ref[...]) + sh_ref[...]).astype(BF16)
        h_ref[0:HALO, :] = hn(xp_ref[...])
        h_ref[HALO:HALO + tm, :] = hn(x_ref[...])
        h_ref[HALO + tm:, :] = hn(xn_ref[...])
        acc_ref[...] = jnp.zeros_like(acc_ref)
        up(bufs[0])

    for parity in range(2):
        @pl.when((j > 0) & (j < nf) & (j % 2 == parity))
        def _():
            up(bufs[parity])
            down(bufs[1 - parity])

    @pl.when(j == nf)
    def _():
        down(bufs[(nf - 1) % 2])
        o_ref[...] = x_ref[...] + gate_ref[...] * acc_ref[...]


def _ffn_call(x, mods, layer, norm_g, w_up, conv_w, conv_b, w_down, geom):
    rows, d = x.shape
    dff = w_down.shape[1]
    tm = geom.row_tile(FFN_ROW_TILE)
    tf = FF_TILE
    nf = dff // tf
    grp = geom.group(tm)
    nhalo = rows // HALO
    cur = lambda j: jnp.minimum(j, nf - 1)
    prv = lambda j: jnp.maximum(j - 1, 0)
    return pl.pallas_call(
        functools.partial(_ffn_kernel, geom, nf),
        grid=(rows // tm, nf + 1),
        in_specs=[pl.BlockSpec((tm, d), lambda i, j: (i, 0)),
                  pl.BlockSpec((HALO, d), lambda i, j: (jnp.maximum(i * (tm // HALO) - 1, 0), 0)),
                  pl.BlockSpec((HALO, d), lambda i, j: (jnp.minimum((i + 1) * (tm // HALO), nhalo - 1), 0)),
                  pl.BlockSpec((None, 1, d), lambda i, j: (2 * layer + 1, 0, 0)),
                  _mod_spec(d, layer, 3, grp), _mod_spec(d, layer, 4, grp), _mod_spec(d, layer, 5, grp),
                  pl.BlockSpec((None, d, tf), lambda i, j: (layer, 0, cur(j))),
                  pl.BlockSpec((None, d, tf), lambda i, j: (layer, 0, nf + cur(j))),
                  pl.BlockSpec((None, 3, tf), lambda i, j: (layer, 0, prv(j))),
                  pl.BlockSpec((None, 3, tf), lambda i, j: (layer, 0, nf + prv(j))),
                  pl.BlockSpec((None, 1, tf), lambda i, j: (layer, 0, prv(j))),
                  pl.BlockSpec((None, 1, tf), lambda i, j: (layer, 0, nf + prv(j))),
                  pl.BlockSpec((None, tf, d), lambda i, j: (layer, prv(j), 0))],
        out_specs=pl.BlockSpec((tm, d), lambda i, j: (i, 0)),
        out_shape=jax.ShapeDtypeStruct((rows, d), F32),
        scratch_shapes=[pltpu.VMEM((tm + 2 * HALO, d), BF16), pltpu.VMEM((tm, d), F32)]
        + [pltpu.VMEM((tm + 2 * HALO, tf), F32) for _ in range(4)],
        compiler_params=_params(("arbitrary", "arbitrary")),
        name="conv_ffn",
    )(x, x, x, norm_g, mods, mods, mods, w_up, w_up, conv_w, conv_w, conv_b, conv_b, w_down)


def _final_kernel(x_ref, g_ref, o_ref):
    o_ref[...] = _rms(x_ref[...], g_ref[...])


def _final_call(x, g, row0, nrows):
    d = x.shape[1]
    tm = OUT_ROW_TILE
    while nrows % tm or row0 % tm:
        tm //= 2
    return pl.pallas_call(
        _final_kernel,
        grid=(nrows // tm,),
        in_specs=[pl.BlockSpec((tm, d), lambda i: (row0 // tm + i, 0)), pl.BlockSpec((1, d), lambda i: (0, 0))],
        out_specs=pl.BlockSpec((tm, d), lambda i: (i, 0)),
        out_shape=jax.ShapeDtypeStruct((nrows, d), F32),
        compiler_params=_params(("arbitrary",)),
        name="final_norm",
    )(x, g.reshape(1, d))


def _ssd_kernel(fwd, geom, heads, *refs):
    (xs_ref, xsp_ref, xsn_ref, bc_ref, bcp_ref, bcn_ref, dt_ref, cwx_ref, cbx_ref, cwb_ref, cbb_ref,
     dtb_ref, alog_ref, pe_ref, bd_ref, h0_ref) = refs[:16]
    if fwd:
        z_ref, yb_ref, dx_ref, gn_ref, out_ref, st_ref, s_ref, yacc_ref = refs[16:]
    else:
        out_ref, st_ref, s_ref = refs[16:]
    q = xs_ref.shape[0]
    seq, cpos, nc, c, _ = geom.chunk_pos(pl.program_id(0), q, fwd)
    first, last = c == 0, c == nc - 1
    half = LANES // 2

    @pl.when(cpos == 0)
    def _():
        s_ref[...] = jnp.where(seq < geom.n_ctx, 0.0, h0_ref[...])

    xs = _conv3_silu(xs_ref[...], xsp_ref[...], xsn_ref[...], cwx_ref[...], cbx_ref[...], first, last)
    bc = _conv3_silu(bc_ref[...], bcp_ref[...], bcn_ref[...], cwb_ref[...], cbb_ref[...], first, last)
    bm, cm = bc[:, :LANES], bc[:, LANES:]
    dt = _softplus(dt_ref[...] + dtb_ref[...])
    la = dt * (-jnp.exp(alog_ref[...]))
    row = lax.broadcasted_iota(jnp.int32, (q, q), 0)
    col = lax.broadcasted_iota(jnp.int32, (q, q), 1)
    tril, triu = row >= col, row <= col
    cs_p = _mask_sum(tril.astype(F32).astype(BF16), la)
    cs_s = _mask_sum(triu.astype(F32).astype(BF16), la)
    cs = cs_p if fwd else cs_s
    edge = q - 1 if fwd else 0
    ecs = jnp.exp(cs)
    ecs_x = _expand(ecs, pe_ref[...], pieces=2)
    wd_x = _expand(dt * jnp.exp(cs[edge:edge + 1, :] - cs), pe_ref[...], pieces=2)
    carry = _expand(jnp.broadcast_to(ecs[edge:edge + 1, :], (HALO, LANES)), pe_ref[...])[0:1, :]
    s_old = s_ref[...]
    y_off = _dot(cm, s_old) * ecs_x
    s_new = (s_old * carry + _dot_tn(bm, xs * wd_x)) * bd_ref[...]
    s_ref[...] = s_new

    @pl.when(cpos == nc - 1)
    def _():
        s_t = s_new.T
        p_dim = s_t.shape[0] // heads
        for h in range(heads):
            grp = h // (heads // 2)
            st_ref[h] = s_t[h * p_dim:(h + 1) * p_dim, grp * half:(grp + 1) * half]

    if not fwd:
        out_ref[...] = y_off
        return

    c2_p, c2_s = cs_p * LOG2E, cs_s * LOG2E
    ldt = jnp.log2(dt)
    rt_p, rt_s = (c2_p - ldt).T, (c2_s - ldt).T
    neg = -jnp.inf
    lane = lax.broadcasted_iota(jnp.int32, (q, LANES), 1)
    lo, hi = lane < half, lane >= half
    gmat = (_dot_nt(jnp.where(lo, cm, 0.0), bm), _dot_nt(jnp.where(hi, cm, 0.0), bm))
    for p in range(heads // 2):
        xp = xs[:, p * LANES:(p + 1) * LANES]
        acc = None
        for hh in range(2):
            h = 2 * p + hh
            hb = heads + h
            lf = jnp.exp2(jnp.where(tril, c2_p[:, h:h + 1] - rt_p[h:h + 1, :], neg))
            lb = jnp.exp2(jnp.where(triu, c2_s[:, hb:hb + 1] - rt_s[hb:hb + 1, :], neg))
            w = gmat[h // (heads // 2)] * (lf + lb)
            part = _dot(w, jnp.where(lo if hh == 0 else hi, xp, 0.0))
            acc = part if acc is None else acc + part
        yacc_ref[:, p * LANES:(p + 1) * LANES] = acc
    y = yacc_ref[...] + y_off + yb_ref[...] + dx_ref[...] * xs
    yz = y * _silu(z_ref[...])
    out_ref[...] = _rms(yz, gn_ref[...]).astype(BF16)


def _ssd_call(fwd, proj, yb, prm, h0, geom, heads, width):
    rows = proj.shape[0]
    q = geom.scan_chunk()
    nhalo = rows // HALO
    bc_w = 2 * LANES
    xs_blk, z_blk, bc_blk, dt_blk = 0, 1, 5 * width // bc_w, (5 * width + bc_w) // LANES

    def seq(t):
        return geom.chunk_pos(t, q, fwd)[0]

    def rowblk(t):
        return geom.chunk_pos(t, q, fwd)[4]

    def prev(t):
        return jnp.maximum(rowblk(t) * (q // HALO) - 1, 0)

    def nxt(t):
        return jnp.minimum((rowblk(t) + 1) * (q // HALO), nhalo - 1)

    const = lambda t: (0, 0)
    in_specs = [pl.BlockSpec((q, width), lambda t: (rowblk(t), xs_blk)),
                pl.BlockSpec((HALO, width), lambda t: (prev(t), xs_blk)),
                pl.BlockSpec((HALO, width), lambda t: (nxt(t), xs_blk)),
                pl.BlockSpec((q, bc_w), lambda t: (rowblk(t), bc_blk)),
                pl.BlockSpec((HALO, bc_w), lambda t: (prev(t), bc_blk)),
                pl.BlockSpec((HALO, bc_w), lambda t: (nxt(t), bc_blk)),
                pl.BlockSpec((q, LANES), lambda t: (rowblk(t), dt_blk)),
                pl.BlockSpec((3, width), const), pl.BlockSpec((1, width), const),
                pl.BlockSpec((3, bc_w), const), pl.BlockSpec((1, bc_w), const),
                pl.BlockSpec((1, LANES), const), pl.BlockSpec((1, LANES), const),
                pl.BlockSpec((LANES, width), const), pl.BlockSpec((LANES, width), const),
                pl.BlockSpec((None, LANES, width), lambda t: (jnp.maximum(seq(t) - geom.n_ctx, 0), 0, 0))]
    args = [proj, proj, proj, proj, proj, proj, proj, prm["cwx"], prm["cbx"], prm["cwb"], prm["cbb"],
            prm["dtb"], prm["alog"], prm["pf"] if fwd else prm["pb"], prm["bd"], h0]
    scratch = [pltpu.VMEM((LANES, width), F32)]
    if fwd:
        in_specs += [pl.BlockSpec((q, width), lambda t: (rowblk(t), z_blk)),
                     pl.BlockSpec((q, width), lambda t: (rowblk(t), 0)),
                     pl.BlockSpec((1, width), const), pl.BlockSpec((1, width), const)]
        args += [proj, yb, prm["dx"], prm["gn"]]
        scratch.append(pltpu.VMEM((q, width), F32))
        out_dtype = BF16
    else:
        out_dtype = F32
    return pl.pallas_call(
        functools.partial(_ssd_kernel, fwd, geom, heads),
        grid=(rows // q,),
        in_specs=in_specs,
        out_specs=[pl.BlockSpec((q, width), lambda t: (rowblk(t), 0)),
                   pl.BlockSpec((None, heads, width // heads, LANES // 2), lambda t: (seq(t), 0, 0, 0))],
        out_shape=[jax.ShapeDtypeStruct((rows, width), out_dtype),
                   jax.ShapeDtypeStruct((geom.n_ctx + geom.n_lat, heads, width // heads, LANES // 2), F32)],
        scratch_shapes=scratch,
        compiler_params=_params(("arbitrary",)),
        name="ssd_fwd" if fwd else "ssd_bwd",
    )(*args)


def _rope(x, cos, sin_signed):
    lane = lax.broadcasted_iota(jnp.int32, x.shape, 1)
    partner = jnp.where((lane & 31) < 16, pltpu.roll(x, LANES - 16, 1), pltpu.roll(x, 16, 1))
    return x * cos + partner * sin_signed


def _attn_kernel(lam_init, hps, q_ref, k_ref, v_ref, kc_ref, vc_ref, cq_ref, sq_ref, ck_ref, sk_ref, lp_ref, g_ref,
                 o_ref, kr_ref, vr_ref):
    hs = range(hps)
    sls = [slice(h * LANES, (h + 1) * LANES) for h in hs]

    @pl.when(pl.program_id(2) == 0)
    def _():
        for sl in sls:
            kr_ref[:, sl] = _rope(k_ref[:, sl], ck_ref[...], sk_ref[...]).astype(BF16)
        vr_ref[...] = v_ref[...].astype(BF16)

    scale = float(LANES // 2) ** -0.5 * LOG2E
    lane = lax.broadcasted_iota(jnp.int32, (q_ref.shape[0], LANES), 1)
    lo = lane < LANES // 2
    lp = lp_ref[...]
    lam = (jnp.exp(jnp.sum(lp[0:1] * lp[1:2], axis=1, keepdims=True))
           - jnp.exp(jnp.sum(lp[2:3] * lp[3:4], axis=1, keepdims=True)) + lam_init)
    q = [_rope(q_ref[:, sl], cq_ref[...], sq_ref[...]) * scale for sl in sls]
    qm = [(jnp.where(lo, q[h], 0.0).astype(BF16), jnp.where(lo, 0.0, q[h]).astype(BF16)) for h in hs]
    nt = (((1,), (1,)), ((), ()))
    s_own = [[lax.dot_general(qm[h][m], kr_ref[:, sls[h]], nt, preferred_element_type=F32) for m in range(2)]
             for h in hs]
    s_ctx = [[lax.dot_general(qm[h][m], kc_ref[:, sls[h]].astype(BF16), nt, preferred_element_type=F32)
              for m in range(2)] for h in hs]
    pd_own, pd_ctx = [], []
    for h in hs:
        po, pc, coef = [], [], []
        for m in range(2):
            mx = jnp.maximum(jnp.max(s_own[h][m], axis=1, keepdims=True), jnp.max(s_ctx[h][m], axis=1, keepdims=True))
            po.append(jnp.exp2(s_own[h][m] - mx))
            pc.append(jnp.exp2(s_ctx[h][m] - mx))
            den = jnp.sum(po[m], axis=1, keepdims=True) + jnp.sum(pc[m], axis=1, keepdims=True)
            coef.append(1.0 / den if m == 0 else lam / den)
        pd_own.append((po[0] * coef[0] - po[1] * coef[1]).astype(BF16))
        pd_ctx.append((pc[0] * coef[0] - pc[1] * coef[1]).astype(BF16))
    o = [jnp.dot(pd_own[h], vr_ref[:, sls[h]], preferred_element_type=F32)
         + jnp.dot(pd_ctx[h], vc_ref[:, sls[h]].astype(BF16), preferred_element_type=F32) for h in hs]
    for h in hs:
        o_ref[:, sls[h]] = (_rms(o[h], g_ref[...]) * (1.0 - lam_init)).astype(BF16)


def _attn_ctx_kernel(lam_init, heads, q_ref, k_ref, v_ref, lp_ref, g_ref, o_ref, ko_ref, vo_ref):
    for h in range(heads):
        ko_ref[:, h, :] = k_ref[:, h * LANES:(h + 1) * LANES]
        vo_ref[:, h, :] = v_ref[:, h * LANES:(h + 1) * LANES]
    rows = q_ref.shape[0]
    lane = lax.broadcasted_iota(jnp.int32, (rows, LANES), 1)
    lo = lane < LANES // 2
    scale = float(LANES // 2) ** -0.5 * LOG2E
    lp = lp_ref[...]
    lam = (jnp.exp(jnp.sum(lp[0:1] * lp[1:2], axis=1, keepdims=True))
           - jnp.exp(jnp.sum(lp[2:3] * lp[3:4], axis=1, keepdims=True)) + lam_init)
    hs = range(heads)
    sls = [slice(h * LANES, (h + 1) * LANES) for h in hs]
    q = [q_ref[:, sl] * scale for sl in sls]
    k = [k_ref[:, sl].astype(BF16) for sl in sls]
    s0 = [_dot_nt(jnp.where(lo, q[h], 0.0), k[h]) for h in hs]
    s1 = [_dot_nt(jnp.where(lo, 0.0, q[h]), k[h]) for h in hs]
    pd = []
    for h in hs:
        p0 = jnp.exp2(s0[h] - jnp.max(s0[h], axis=1, keepdims=True))
        p1 = jnp.exp2(s1[h] - jnp.max(s1[h], axis=1, keepdims=True))
        c0 = 1.0 / jnp.sum(p0, axis=1, keepdims=True)
        c1 = lam / jnp.sum(p1, axis=1, keepdims=True)
        pd.append(p0 * c0 - p1 * c1)
    o = [_dot(pd[h], v_ref[:, sls[h]]) for h in hs]
    for h in hs:
        o_ref[:, sls[h]] = (_rms(o[h], g_ref[...]) * (1.0 - lam_init)).astype(BF16)


def _attn_ctx_call(lam_init, proj, lam_p, norm_g, n_seq, seq_len, heads, width):
    spec = lambda blk: pl.BlockSpec((seq_len, width), lambda b: (b, blk))
    return pl.pallas_call(
        functools.partial(_attn_ctx_kernel, lam_init, heads),
        grid=(n_seq,),
        in_specs=[spec(2), spec(3), spec(4), pl.BlockSpec(lam_p.shape, lambda b: (0, 0)),
                  pl.BlockSpec((1, LANES), lambda b: (0, 0))],
        out_specs=[pl.BlockSpec((seq_len, width), lambda b: (b, 0)),
                   pl.BlockSpec((None, seq_len, heads, LANES), lambda b: (b, 0, 0, 0)),
                   pl.BlockSpec((None, seq_len, heads, LANES), lambda b: (b, 0, 0, 0))],
        out_shape=[jax.ShapeDtypeStruct((n_seq * seq_len, width), BF16),
                   jax.ShapeDtypeStruct((n_seq, seq_len, heads, LANES), F32),
                   jax.ShapeDtypeStruct((n_seq, seq_len, heads, LANES), F32)],
        compiler_params=_params(("arbitrary",)),
        name="diff_attn_ctx",
    )(proj, proj, proj, lam_p, norm_g.reshape(1, LANES))


def _attn_call(lam_init, proj, cache_k, cache_v, tables, lam_p, norm_g, row0, n_seq, seq_len, heads, width):
    tq = min(Q_TILE, seq_len)
    nq = seq_len // tq
    hps = ATTN_HEADS_PER_STEP
    hw = hps * LANES
    ngrp = heads // hps
    q_blk, k_blk, v_blk = 2 * ngrp, 3 * ngrp, 4 * ngrp
    qb, sb = row0 // tq, row0 // seq_len
    past = cache_k.shape[1]
    const = lambda b, h, i: (0, 0)
    in_specs = [pl.BlockSpec((tq, hw), lambda b, h, i: (qb + b * nq + i, q_blk + h)),
                pl.BlockSpec((seq_len, hw), lambda b, h, i: (sb + b, k_blk + h)),
                pl.BlockSpec((seq_len, hw), lambda b, h, i: (sb + b, v_blk + h)),
                pl.BlockSpec((None, past, hw), lambda b, h, i: (b, 0, h)),
                pl.BlockSpec((None, past, hw), lambda b, h, i: (b, 0, h)),
                pl.BlockSpec((tq, LANES), lambda b, h, i: (i, 0)),
                pl.BlockSpec((tq, LANES), lambda b, h, i: (i, 0)),
                pl.BlockSpec((seq_len, LANES), const), pl.BlockSpec((seq_len, LANES), const),
                pl.BlockSpec(lam_p.shape, const), pl.BlockSpec((1, LANES), const)]
    return pl.pallas_call(
        functools.partial(_attn_kernel, lam_init, hps),
        grid=(n_seq, ngrp, nq),
        in_specs=in_specs,
        out_specs=pl.BlockSpec((tq, hw), lambda b, h, i: (b * nq + i, h)),
        out_shape=jax.ShapeDtypeStruct((n_seq * seq_len, width), BF16),
        scratch_shapes=[pltpu.VMEM((seq_len, hw), BF16), pltpu.VMEM((seq_len, hw), BF16)],
        compiler_params=_params(("arbitrary", "arbitrary", "arbitrary")),
        name="diff_attn_lat",
    )(proj, proj, proj, cache_k, cache_v, tables[0], tables[1], tables[0], tables[1], lam_p,
      norm_g.reshape(1, LANES))


def _rope_tables(n_tokens, grid_w):
    n_freq = LANES // 8
    pos = jnp.arange(n_tokens)
    r = (pos // grid_w).astype(F32)
    cpos = (pos % grid_w).astype(F32)
    inv = ROPE_BASE ** (-jnp.arange(n_freq, dtype=F32) / n_freq)
    ang_r, ang_c = r[:, None] * inv, cpos[:, None] * inv
    cos32 = lambda a: jnp.concatenate([jnp.cos(a), jnp.cos(a)], axis=1)
    sin32 = lambda a: jnp.concatenate([-jnp.sin(a), jnp.sin(a)], axis=1)
    cos64 = jnp.concatenate([cos32(ang_r), cos32(ang_c)], axis=1)
    sin64 = jnp.concatenate([sin32(ang_r), sin32(ang_c)], axis=1)
    return jnp.concatenate([cos64, cos64], axis=1), jnp.concatenate([sin64, sin64], axis=1)


def _gdn_masks(rb, sb, fwd):
    shift = GDN_CHUNK.bit_length() - 1

    def tri(n):
        row = jnp.arange(n)[:, None]
        col = jnp.arange(n)[None, :]
        blk = (row >> shift) == (col >> shift)
        return row, col, blk, blk & ((row >= col) if fwd else (row <= col))

    row, col, blk, incl = tri(sb)
    strict = blk & ((row > col) if fwd else (row < col))
    ms = [incl, strict, row == col, (row >> 1) == (col >> 1)]
    for lvl in range(1, shift):
        ms.append(((row >> (lvl + 1)) == (col >> (lvl + 1))) & ((row >> lvl) != (col >> lvl)))
    _, _, blk_rb, incl_rb = tri(rb)
    return jnp.stack(ms).astype(F32), jnp.stack([incl_rb, blk_rb]).astype(BF16)


def _gdn_kernel(fwd, geom, heads, hps, *refs):
    (q_ref, qp_ref, qn_ref, k_ref, kp_ref, kn_ref, v_ref, vp_ref, vn_ref, ab_ref,
     cwq_ref, cbq_ref, cwk_ref, cbk_ref, cwv_ref, cbv_ref, alog_ref, dtb_ref, mf_ref, mb_ref, s0_ref) = refs[:21]
    if fwd:
        z_ref, ob_ref, gn_ref, out_ref, st_ref, s_ref = refs[21:]
    else:
        out_ref, st_ref, s_ref = refs[21:]
    hg = pl.program_id(0)
    rb = q_ref.shape[0]
    seq, cpos, nb, c, _ = geom.chunk_pos(pl.program_id(1), rb, fwd)
    first, last = c == 0, c == nb - 1
    dk = q_ref.shape[1] // hps
    ch = GDN_CHUNK
    nsub = rb // ch
    n_lvl = ch.bit_length() - 1

    @pl.when(cpos == 0)
    def _():
        s_ref[...] = jnp.where(seq < geom.n_ctx, 0.0, s0_ref[...])

    ab = ab_ref[...]
    d0 = 0 if fwd else 2 * heads
    g_all = -jnp.exp(alog_ref[...]) * _softplus(ab + dtb_ref[...])
    b_all = 1.0 / (1.0 + jnp.exp(-ab))
    gc_all = _mask_sum(mb_ref[0], g_all)
    gt_all = _mask_sum(mb_ref[1], g_all)
    gct_all = gc_all.T
    lane = lax.broadcasted_iota(jnp.int32, ab.shape, 1)
    sub = lax.broadcasted_iota(jnp.int32, gct_all.shape, 0)
    incl_f, strict_f, eye, pair = mf_ref[0], mf_ref[1], mf_ref[2], mf_ref[3]
    sb = mf_ref.shape[1]
    cps = sb // ch

    def pick(a, l):
        return jnp.sum(jnp.where(lane == l, a, 0.0), axis=1, keepdims=True)

    hs = range(hps)
    sls = [slice(hh * dk, (hh + 1) * dk) for hh in hs]
    cs = [(hh, blk) for hh in hs for blk in range(rb // sb)]
    nch = range(len(cs))
    kn, qn, vb, gc, gtot, beta, dec = ([] for _ in range(7))
    for hh in hs:
        hd = hg * hps + hh
        sl = sls[hh]

        def conv(x_ref, xp_ref, xn_ref, cw_ref, cb_ref):
            return _conv3_silu(x_ref[:, sl], xp_ref[:, sl], xn_ref[:, sl], cw_ref[:, sl], cb_ref[:, sl], first, last)

        qc = conv(q_ref, qp_ref, qn_ref, cwq_ref, cbq_ref)
        kc = conv(k_ref, kp_ref, kn_ref, cwk_ref, cbk_ref)
        vh = conv(v_ref, vp_ref, vn_ref, cwv_ref, cbv_ref)
        qh = qc * lax.rsqrt(jnp.sum(qc * qc, axis=1, keepdims=True) + EPS) * (float(dk) ** -0.5)
        kh = kc * lax.rsqrt(jnp.sum(kc * kc, axis=1, keepdims=True) + EPS)
        gch, gth = pick(gc_all, d0 + hd), pick(gt_all, d0 + hd)
        bh = pick(b_all, d0 + heads + hd)
        g_row = jnp.sum(jnp.where(sub == d0 + hd, gct_all, 0.0), axis=0, keepdims=True)
        for blk in range(rb // sb):
            r = slice(blk * sb, (blk + 1) * sb)
            qn.append(qh[r]); kn.append(kh[r]); gc.append(gch[r]); gtot.append(gth[r]); beta.append(bh[r])
            vb.append(vh[r] * bh[r])
            dec.append(jnp.exp(jnp.minimum(gch[r] - g_row[:, r], 0.0)) * incl_f)
    kk = [_dot_nt(kn[c], kn[c]) for c in nch]
    m = [(kk[c] * beta[c]) * (dec[c] * strict_f) for c in nch]
    t = [eye - m[c] * pair for c in nch]
    for lvl in range(1, n_lvl):
        off = mf_ref[3 + lvl]
        tm = [_dot(t[c], m[c] * off) for c in nch]
        tmt = [_dot(tm[c], t[c]) for c in nch]
        t = [t[c] - tmt[c] for c in nch]
    eg = [jnp.exp(gc[c]) for c in nch]
    sol = [_dot(t[c], jnp.concatenate([vb[c], kn[c] * (beta[c] * eg[c])], axis=1)) for c in nch]
    qk = [_dot_nt(qn[c], kn[c]) for c in nch]
    qg = [qn[c] * eg[c] for c in nch]
    kg = [kn[c] * jnp.exp(gtot[c] - gc[c]) for c in nch]
    egl = [jnp.exp(gtot[c]) for c in nch]
    s = [s_ref[hh] for hh in hs]
    vnew = [[None] * cps for _ in nch]
    qs = [[None] * cps for _ in nch]
    for i in (range(nsub) if fwd else reversed(range(nsub))):
        blk, li = divmod(i, cps)
        r = slice(li * ch, (li + 1) * ch)
        idx = [hh * (rb // sb) + blk for hh in hs]
        ws = [_dot(jnp.concatenate([sol[c][r, dk:], qg[c][r]], axis=0), s[hh]) for hh, c in zip(hs, idx)]
        for hh, c in zip(hs, idx):
            vnew[c][li] = sol[c][r, :dk] - ws[hh][:ch]
            qs[c][li] = ws[hh][ch:]
        upd = [_dot_tn(kg[c][r], vnew[c][li]) for c in idx]
        s = [s[hh] * egl[c][li * ch:li * ch + 1, :] + upd[hh] for hh, c in zip(hs, idx)]
    intra = [_dot(qk[c] * dec[c], jnp.concatenate(vnew[c], axis=0)) for c in nch]
    for c, (hh, blk) in enumerate(cs):
        sl = sls[hh]
        r = slice(blk * sb, (blk + 1) * sb)
        o = jnp.concatenate(qs[c], axis=0) + intra[c]
        if fwd:
            o = o + ob_ref[r, sl]
            out_ref[r, sl] = (_rms(o, gn_ref[...]) * _silu(z_ref[r, sl])).astype(BF16)
        else:
            out_ref[r, sl] = o
    for hh in hs:
        s_ref[hh] = s[hh]

    @pl.when(cpos == nb - 1)
    def _():
        st_ref[...] = s_ref[...]


def _gdn_call(fwd, proj, ob, prm, s0, geom, heads, width):
    rows = proj.shape[0]
    rb = geom.scan_chunk()
    nhalo = rows // HALO
    dk = width // heads
    hps = GDN_HEADS_PER_STEP
    hw = hps * dk
    ngrp = heads // hps
    ab_blk = 4 * width // LANES
    mf, mb = _gdn_masks(rb, min(GDN_SYSTEM_ROWS, rb), fwd)

    def seq(t):
        return geom.chunk_pos(t, rb, fwd)[0]

    def rowblk(t):
        return geom.chunk_pos(t, rb, fwd)[4]

    def prev(t):
        return jnp.maximum(rowblk(t) * (rb // HALO) - 1, 0)

    def nxt(t):
        return jnp.minimum((rowblk(t) + 1) * (rb // HALO), nhalo - 1)

    in_specs, args = [], []
    for sec in range(3):
        in_specs += [pl.BlockSpec((rb, hw), lambda h, t, sec=sec: (rowblk(t), sec * ngrp + h)),
                     pl.BlockSpec((HALO, hw), lambda h, t, sec=sec: (prev(t), sec * ngrp + h)),
                     pl.BlockSpec((HALO, hw), lambda h, t, sec=sec: (nxt(t), sec * ngrp + h))]
        args += [proj, proj, proj]
    in_specs.append(pl.BlockSpec((rb, LANES), lambda h, t: (rowblk(t), ab_blk)))
    args.append(proj)
    for sec in range(3):
        in_specs += [pl.BlockSpec((3, hw), lambda h, t, sec=sec: (0, sec * ngrp + h)),
                     pl.BlockSpec((1, hw), lambda h, t, sec=sec: (0, sec * ngrp + h))]
        args += [prm["cw"], prm["cb"]]
    in_specs += [pl.BlockSpec((1, LANES), lambda h, t: (0, 0)), pl.BlockSpec((1, LANES), lambda h, t: (0, 0)),
                 pl.BlockSpec(mf.shape, lambda h, t: (0, 0, 0)), pl.BlockSpec(mb.shape, lambda h, t: (0, 0, 0)),
                 pl.BlockSpec((None, hps, dk, dk), lambda h, t: (jnp.maximum(seq(t) - geom.n_ctx, 0), h, 0, 0))]
    args += [prm["alog"], prm["dtb"], mf, mb, s0]
    if fwd:
        in_specs += [pl.BlockSpec((rb, hw), lambda h, t: (rowblk(t), 3 * ngrp + h)),
                     pl.BlockSpec((rb, hw), lambda h, t: (rowblk(t), h)),
                     pl.BlockSpec((1, dk), lambda h, t: (0, 0))]
        args += [proj, ob, prm["gn"]]
    return pl.pallas_call(
        functools.partial(_gdn_kernel, fwd, geom, heads, hps),
        grid=(ngrp, rows // rb),
        in_specs=in_specs,
        out_specs=[pl.BlockSpec((rb, hw), lambda h, t: (rowblk(t), h)),
                   pl.BlockSpec((None, hps, dk, dk), lambda h, t: (seq(t), h, 0, 0))],
        out_shape=[jax.ShapeDtypeStruct((rows, width), BF16 if fwd else F32),
                   jax.ShapeDtypeStruct((geom.n_ctx + geom.n_lat, heads, dk, dk), F32)],
        scratch_shapes=[pltpu.VMEM((hps, dk, dk), F32)],
        compiler_params=_params(("arbitrary", "arbitrary")),
        name="gdn_fwd" if fwd else "gdn_bwd",
    )(*args)


def _pack_states_kernel(n_in, *refs):
    o_ref = refs[n_in]
    for i in range(n_in):
        o_ref[i] = refs[i][...]


def _pack_states_call(states, n_keep):
    h, a, b = states[0].shape[1:]
    n = len(states)
    return pl.pallas_call(
        functools.partial(_pack_states_kernel, n),
        grid=(n_keep,),
        in_specs=[pl.BlockSpec((None, h, a, b), lambda i: (i, 0, 0, 0))] * n,
        out_specs=pl.BlockSpec((None, n, h, a, b), lambda i: (i, 0, 0, 0, 0)),
        out_shape=jax.ShapeDtypeStruct((n_keep, n, h, a, b), F32),
        compiler_params=_params(("arbitrary",)),
        name="pack_states",
    )(*states)


def _pad_cols(w, n):
    return jnp.pad(w, ((0, 0), (0, n - w.shape[1])))


def _lane_row(v):
    v = v.reshape(1, -1).astype(F32)
    return _pad_cols(v, LANES)


def _even_layer(x, mods, layer, e, geom, norm_g, w_in, conv_w, conv_b, dt_bias, a_log, d_skip, ssd_g,
                lam_p, da_g, w_out, cache_k, cache_v, state_ssd, tables):
    d = x.shape[1]
    heads, p_dim, n_state = state_ssd.shape[3], state_ssd.shape[4], state_ssd.shape[5]
    width = heads * p_dim
    da_heads = cache_k.shape[3]
    assert width == d and da_heads * LANES == d and 2 * n_state == LANES and 2 * p_dim == LANES
    bc_w = 2 * LANES
    n_dt = 2 * heads
    o_z, o_xs, o_bc, o_dt = 0, width, 2 * width, 2 * width + bc_w
    o_q = o_dt + n_dt
    cols = [w_in[:, o_xs:o_xs + width], w_in[:, o_z:o_z + width], w_in[:, o_q:o_q + 3 * d],
            w_in[:, o_bc:o_bc + bc_w], w_in[:, o_dt:o_dt + n_dt]]
    n_used = 5 * width + bc_w + n_dt
    n_pad = -(-n_used // IN_COL_TILE) * IN_COL_TILE
    w_perm = _pad_cols(jnp.concatenate(cols, axis=1), n_pad).astype(BF16)
    proj = _inproj_call(x, mods, layer, norm_g, w_perm, geom)

    hl = jnp.arange(LANES)[:, None]
    cl = jnp.arange(width)[None, :]
    prm = {
        "cwx": conv_w[:, :width], "cbx": conv_b[None, :width],
        "cwb": conv_w[:, width:], "cbb": conv_b[None, width:],
        "dtb": _lane_row(dt_bias), "alog": _lane_row(a_log),
        "pf": (hl == cl // p_dim).astype(BF16), "pb": (hl == heads + cl // p_dim).astype(BF16),
        "bd": (hl // n_state == cl // (width // 2)).astype(F32),
        "dx": jnp.repeat(d_skip, p_dim)[None, :], "gn": ssd_g[None, :],
    }

    def state_in(st):
        t = st.transpose(0, 3, 1, 2).reshape(st.shape[0], n_state, width)
        return jnp.concatenate([t, t], axis=1) * prm["bd"]

    h0f, h0b = state_in(state_ssd[:, e, 0].astype(F32)), state_in(state_ssd[:, e, 1].astype(F32))
    yb, st_b = _ssd_call(False, proj, None, prm, h0b, geom, heads, width)
    y, st_f = _ssd_call(True, proj, yb, prm, h0f, geom, heads, width)
    new_states = [st_f, st_b]
    lam_init = 0.8 - 0.6 * math.exp(-0.3 * layer)
    o_ctx, k_ctx, v_ctx = _attn_ctx_call(lam_init, proj, lam_p, da_g, geom.n_ctx, geom.s_ctx, da_heads, d)
    o_lat = _attn_call(lam_init, proj, cache_k[:, e].reshape(geom.n_lat, -1, d),
                       cache_v[:, e].reshape(geom.n_lat, -1, d), tables, lam_p, da_g, geom.ctx_rows,
                       geom.n_lat, geom.s_lat, da_heads, d)
    w_o = w_out.astype(BF16)
    x = _outproj_call(x, mods, layer, [y, (o_ctx, o_lat)], [w_o[:width], w_o[width:]], geom)
    return x, k_ctx, v_ctx, new_states


def _odd_layer(x, mods, layer, o, geom, norm_g, w_in, conv_w, conv_b, a_log, dt_bias, g_norm, w_out, state_gdn):
    d = x.shape[1]
    heads, dk, dv = state_gdn.shape[3], state_gdn.shape[4], state_gdn.shape[5]
    width = heads * dv
    assert width == d and dk == LANES and dv == LANES and 4 * heads <= LANES
    n_used = 4 * width + 4 * heads
    n_pad = -(-n_used // IN_COL_TILE) * IN_COL_TILE
    w_perm = _pad_cols(w_in, n_pad).astype(BF16)
    proj = _inproj_call(x, mods, layer, norm_g, w_perm, geom)

    def ab_row(v):
        r = jnp.zeros((2, 2 * heads), F32).at[:, :heads].set(v.astype(F32))
        return _lane_row(r)

    prm = {"cw": conv_w, "cb": conv_b[None, :], "alog": ab_row(a_log), "dtb": ab_row(dt_bias), "gn": g_norm[None, :]}
    s0f, s0b = state_gdn[:, o, 0].astype(F32), state_gdn[:, o, 1].astype(F32)
    ob, st_b = _gdn_call(False, proj, None, prm, s0b, geom, heads, width)
    og, st_f = _gdn_call(True, proj, ob, prm, s0f, geom, heads, width)
    new_states = [st_f, st_b]
    x = _outproj_call(x, mods, layer, [og], [w_out.astype(BF16)], geom)
    return x, new_states


def kernel(x_prompt, x_sample, cache_attn_k, cache_attn_v, state_ssd, state_gdn, c, c_ctx, ada_w, ada_b, norm_g, ev_w_in, ev_conv_w, ev_conv_b, ssd_dt_bias, ssd_a_log, ssd_d, ssd_norm_g, da_lambda, da_norm_g, ev_w_out, od_w_in, od_conv_w, od_conv_b, gdn_a_log, gdn_dt_bias, gdn_norm_g, od_w_out, ffn_w_up, ffn_conv_w, ffn_conv_b, ffn_w_down, final_norm_g):
    n_ctx, s_ctx, d = x_prompt.shape
    n_lat, s_lat, _ = x_sample.shape
    depth = ada_w.shape[0]
    assert n_lat + 1 <= MOD_ROWS and s_ctx & (s_ctx - 1) == 0 and s_lat & (s_lat - 1) == 0
    geom = _Geom(n_ctx, s_ctx, n_lat, s_lat)
    dtype = x_prompt.dtype
    x = jnp.concatenate([x_prompt.reshape(-1, d), x_sample.reshape(-1, d)], axis=0).astype(F32)

    cvec = jnp.zeros((MOD_ROWS, d), F32).at[0].set(c_ctx).at[1:1 + n_lat].set(c)
    mods = _mod_call(cvec, ada_w, ada_b)
    mods = mods.reshape(depth, MOD_ROWS, N_MOD, d).transpose(0, 2, 1, 3).reshape(depth * N_MOD * MOD_ROWS, 1, d)

    grid_w = 64
    tables = _rope_tables(s_lat, grid_w)
    ks_out, vs_out, ssd_out, gdn_out = [], [], [], []
    for l in range(depth):
        if l % 2 == 0:
            e = l // 2
            x, k_new, v_new, st = _even_layer(
                x, mods, l, e, geom, norm_g[l, 0], ev_w_in[e], ev_conv_w[e], ev_conv_b[e], ssd_dt_bias[e],
                ssd_a_log[e], ssd_d[e], ssd_norm_g[e], da_lambda[e], da_norm_g[e], ev_w_out[e],
                cache_attn_k, cache_attn_v, state_ssd, tables)
            ks_out.append(k_new)
            vs_out.append(v_new)
            ssd_out += st
        else:
            o = l // 2
            x, st = _odd_layer(x, mods, l, o, geom, norm_g[l, 0], od_w_in[o], od_conv_w[o], od_conv_b[o],
                               gdn_a_log[o], gdn_dt_bias[o], gdn_norm_g[o], od_w_out[o], state_gdn)
            gdn_out += st
        x = _ffn_call(x, mods, l, norm_g.reshape(2 * depth, 1, d), ffn_w_up, ffn_conv_w,
                      ffn_conv_b.reshape(depth, 1, -1), ffn_w_down, geom)
    y_prompt = _final_call(x, final_norm_g, 0, geom.ctx_rows).reshape(n_ctx, s_ctx, d)
    y_sample = _final_call(x, final_norm_g, geom.ctx_rows, n_lat * s_lat).reshape(n_lat, s_lat, d)
    def states(parts):
        packed = _pack_states_call(parts, n_ctx)
        return packed.reshape((n_ctx, len(parts) // 2, 2) + packed.shape[2:])

    return (y_prompt.astype(dtype), y_sample.astype(dtype),
            jnp.stack(ks_out, axis=1).astype(dtype), jnp.stack(vs_out, axis=1).astype(dtype),
            states(ssd_out).astype(dtype), states(gdn_out).astype(dtype))
```

```python
import functools
import math

import jax
import jax.numpy as jnp
from jax import lax
from jax.experimental import pallas as pl
from jax.experimental.pallas import tpu as pltpu

F32 = jnp.float32
BF16 = jnp.bfloat16
EPS = 1e-6
ROPE_BASE = 10000.0
LOG2E = 1.4426950408889634

LANES = 128
HALO = 8
VMEM_LIMIT_BYTES = 56 * 1024 * 1024

N_MOD = 6
MOD_ROWS = 8
SCAN_CHUNK = 256
GDN_CHUNK = 64
GDN_HEADS_PER_STEP = 8
GDN_SYSTEM_ROWS = 128
Q_TILE = 256
ATTN_HEADS_PER_STEP = 2
ROW_TILE = 1024
OUT_ROW_TILE = 512
FFN_ROW_TILE = 1024
FF_TILE = 256
FFN_ROW_CHUNK = 1024
IN_COL_STEPS = 2
IN_COL_ALIGN = IN_COL_STEPS * LANES


def _silu(x):
    return x / (1.0 + jnp.exp(-x))


def _softplus(x):
    return jnp.maximum(x, 0.0) + jnp.log1p(jnp.exp(-jnp.abs(x)))


def _dot(a, b):
    return jnp.dot(a.astype(BF16), b.astype(BF16), preferred_element_type=F32)


def _dot_nt(a, b):
    return lax.dot_general(a.astype(BF16), b.astype(BF16), (((1,), (1,)), ((), ())),
                           preferred_element_type=F32)


def _dot_tn(a, b):
    return lax.dot_general(a.astype(BF16), b.astype(BF16), (((0,), (0,)), ((), ())),
                           preferred_element_type=F32)


def _split3(a):
    hi = a.astype(BF16)
    r1 = a - hi.astype(F32)
    mid = r1.astype(BF16)
    return hi, mid, (r1 - mid.astype(F32)).astype(BF16)


def _expand(a, p, pieces=3):
    d = functools.partial(jnp.dot, preferred_element_type=F32)
    hi, mid, lo = _split3(a)
    out = d(hi, p) + d(mid, p)
    return out + d(lo, p) if pieces == 3 else out


def _mask_sum(m, a):
    d = functools.partial(jnp.dot, preferred_element_type=F32)
    hi, mid, lo = _split3(a)
    return d(m, hi) + d(m, mid) + d(m, lo)


def _rms(x, g):
    ms = jnp.mean(x * x, axis=-1, keepdims=True)
    return x * lax.rsqrt(ms + EPS) * g


def _params(sem):
    return pltpu.CompilerParams(dimension_semantics=sem, vmem_limit_bytes=VMEM_LIMIT_BYTES)


def _conv3_silu(x, xp, xn, w, b, first, last):
    rows = x.shape[0]
    row = lax.broadcasted_iota(jnp.int32, x.shape, 0)
    prev_row = jnp.where(first, 0.0, xp[HALO - 1:HALO, :])
    next_row = jnp.where(last, 0.0, xn[0:1, :])
    x_prev = jnp.where(row == 0, prev_row, pltpu.roll(x, 1, 0))
    x_next = jnp.where(row == rows - 1, next_row, pltpu.roll(x, rows - 1, 0))
    y = x_prev * w[0:1, :] + x * w[1:2, :] + x_next * w[2:3, :] + b
    return _silu(y)


def _mod_kernel(c_ref, w_ref, b_ref, o_ref):
    s = _silu(c_ref[...])
    o_ref[...] = _dot(s, w_ref[...]) + b_ref[...]


def _mod_call(cvec, ada_w, ada_b):
    depth, d, n = ada_w.shape
    tn = d
    return pl.pallas_call(
        _mod_kernel,
        grid=(depth, n // tn),
        in_specs=[pl.BlockSpec((MOD_ROWS, d), lambda l, j: (0, 0)),
                  pl.BlockSpec((None, d, tn), lambda l, j: (l, 0, j)),
                  pl.BlockSpec((None, 1, tn), lambda l, j: (l, 0, j))],
        out_specs=pl.BlockSpec((None, MOD_ROWS, tn), lambda l, j: (l, 0, j)),
        out_shape=jax.ShapeDtypeStruct((depth, MOD_ROWS, n), F32),
        compiler_params=_params(("arbitrary", "arbitrary")),
        name="ada_mod",
    )(cvec, ada_w, ada_b.reshape(depth, 1, n))


class _Geom:
    def __init__(self, n_ctx, s_ctx, n_lat, s_lat):
        self.n_ctx, self.s_ctx, self.n_lat, self.s_lat = n_ctx, s_ctx, n_lat, s_lat
        self.ctx_rows = n_ctx * s_ctx
        self.rows = self.ctx_rows + n_lat * s_lat

    def row_tile(self, want):
        t = want
        while self.ctx_rows % t or self.s_lat % t:
            t //= 2
        return t

    def scan_chunk(self):
        return min(SCAN_CHUNK, self.s_ctx, self.s_lat)

    def chunk_pos(self, t, q, fwd):
        ctx_blocks = self.ctx_rows // q
        nc_ctx, nc_lat = self.s_ctx // q, self.s_lat // q
        is_ctx = t < ctx_blocks
        tl = jnp.maximum(t - ctx_blocks, 0)
        seq = jnp.where(is_ctx, t >> (nc_ctx.bit_length() - 1), self.n_ctx + (tl >> (nc_lat.bit_length() - 1)))
        cpos = jnp.where(is_ctx, t & (nc_ctx - 1), tl & (nc_lat - 1))
        nc = jnp.where(is_ctx, nc_ctx, nc_lat)
        c = cpos if fwd else nc - 1 - cpos
        return seq, cpos, nc, c, t - cpos + c

    def group(self, tile):
        def fn(i):
            r = i * tile
            return jnp.where(r < self.ctx_rows, 0,
                             1 + (jnp.maximum(r - self.ctx_rows, 0) >> (self.s_lat.bit_length() - 1)))
        return fn


def _mod_spec(d, layer, which, group_fn):
    return pl.BlockSpec((None, 1, d), lambda i, *_: ((layer * N_MOD + which) * MOD_ROWS + group_fn(i), 0, 0))


def _inproj_kernel(x_ref, g_ref, sh_ref, sc_ref, w_ref, o_ref, h_ref):
    @pl.when(pl.program_id(1) == 0)
    def _():
        h_ref[...] = (_rms(x_ref[...], g_ref[...]) * (1.0 + sc_ref[...]) + sh_ref[...]).astype(BF16)

    o_ref[...] = jnp.dot(h_ref[...], w_ref[...], preferred_element_type=F32)


def _inproj_call(x, mods, layer, norm_g, w, geom):
    rows, d = x.shape
    n = w.shape[1]
    tm = geom.row_tile(ROW_TILE)
    tn = n // IN_COL_STEPS
    grp = geom.group(tm)
    return pl.pallas_call(
        _inproj_kernel,
        grid=(rows // tm, n // tn),
        in_specs=[pl.BlockSpec((tm, d), lambda i, j: (i, 0)),
                  pl.BlockSpec((1, d), lambda i, j: (0, 0)),
                  _mod_spec(d, layer, 0, grp),
                  _mod_spec(d, layer, 1, grp),
                  pl.BlockSpec((d, tn), lambda i, j: (0, j))],
        out_specs=pl.BlockSpec((tm, tn), lambda i, j: (i, j)),
        out_shape=jax.ShapeDtypeStruct((rows, n), F32),
        scratch_shapes=[pltpu.VMEM((tm, d), BF16)],
        compiler_params=_params(("arbitrary", "arbitrary")),
        name="in_proj",
    )(x, norm_g.reshape(1, d), mods, mods, w)


def _outproj_kernel(split, ctx_tiles, x_ref, gate_ref, *refs):
    n_a = sum(2 if s else 1 for s in split)
    a_refs, w_refs, o_ref = list(refs[:n_a]), refs[n_a:n_a + len(split)], refs[n_a + len(split)]
    is_ctx = pl.program_id(0) < ctx_tiles
    acc = None
    for s, w_ref in zip(split, w_refs):
        a = a_refs.pop(0)[...]
        if s:
            a = jnp.where(is_ctx, a, a_refs.pop(0)[...])
        part = jnp.dot(a, w_ref[...], preferred_element_type=F32)
        acc = part if acc is None else acc + part
    o_ref[...] = x_ref[...] + gate_ref[...] * acc


def _outproj_call(x, mods, layer, acts, ws, geom):
    rows, d = x.shape
    tm = geom.row_tile(OUT_ROW_TILE)
    grp = geom.group(tm)
    ctx_tiles = geom.ctx_rows // tm
    split = tuple(isinstance(a, tuple) for a in acts)
    in_specs = [pl.BlockSpec((tm, d), lambda i: (i, 0)), _mod_spec(d, layer, 2, grp)]
    flat = []
    for a in acts:
        if isinstance(a, tuple):
            in_specs += [pl.BlockSpec((tm, a[0].shape[1]), lambda i: (jnp.minimum(i, ctx_tiles - 1), 0)),
                         pl.BlockSpec((tm, a[1].shape[1]), lambda i: (jnp.maximum(i - ctx_tiles, 0), 0))]
            flat += list(a)
        else:
            in_specs.append(pl.BlockSpec((tm, a.shape[1]), lambda i: (i, 0)))
            flat.append(a)
    in_specs += [pl.BlockSpec(w.shape, lambda i: (0, 0)) for w in ws]
    return pl.pallas_call(
        functools.partial(_outproj_kernel, split, ctx_tiles),
        grid=(rows // tm,),
        in_specs=in_specs,
        out_specs=pl.BlockSpec((tm, d), lambda i: (i, 0)),
        out_shape=jax.ShapeDtypeStruct((rows, d), F32),
        compiler_params=_params(("arbitrary",)),
        name="out_proj",
    )(x, mods, *flat, *ws)


def _ffn_kernel(geom, nf, x_ref, xp_ref, xn_ref, g_ref, sh_ref, sc_ref, gate_ref,
                wg_ref, wv_ref, cwg_ref, cwv_ref, cbg_ref, cbv_ref, wd_ref, o_ref, h_ref, acc_ref, *u_refs):
    i, j = pl.program_id(0), pl.program_id(1)
    tm = x_ref.shape[0]
    bufs = (u_refs[0:2], u_refs[2:4])

    def up(buf):
        h = h_ref[...]
        tf = wg_ref.shape[1]
        w = jnp.concatenate([wg_ref[...].astype(BF16), wv_ref[...].astype(BF16)], axis=1)
        u = jnp.dot(h, w, preferred_element_type=F32)
        buf[0][...] = u[:, :tf]
        buf[1][...] = u[:, tf:]

    def down(buf):
        rc = min(FFN_ROW_CHUNK, tm)
        row0 = i * tm
        is_ctx = row0 < geom.ctx_rows
        base = jnp.where(is_ctx, row0, row0 - geom.ctx_rows)
        lseq = jnp.where(is_ctx, geom.s_ctx, geom.s_lat)
        rowi = lax.broadcasted_iota(jnp.int32, (rc, LANES), 0)
        wd = wd_ref[...].astype(BF16)
        for r in range(0, tm, rc):
            pos = (base + r + rowi) & (lseq - 1)
            at_start, at_end = pos == 0, pos == lseq - 1

            def conv(u_ref, cw_ref, cb_ref):
                cols = []
                for c in range(0, u_ref.shape[1], LANES):
                    cs = slice(c, c + LANES)
                    up_ = jnp.where(at_start, 0.0, u_ref[HALO + r - 1:HALO + r - 1 + rc, cs])
                    un = jnp.where(at_end, 0.0, u_ref[HALO + r + 1:HALO + r + 1 + rc, cs])
                    cols.append(up_ * cw_ref[0:1, cs] + u_ref[HALO + r:HALO + r + rc, cs] * cw_ref[1:2, cs]
                                + un * cw_ref[2:3, cs] + cb_ref[:, cs])
                return jnp.concatenate(cols, axis=1)

            a = _silu(conv(buf[0], cwg_ref, cbg_ref)) * conv(buf[1], cwv_ref, cbv_ref)
            acc_ref[r:r + rc, :] += jnp.dot(a.astype(BF16), wd, preferred_element_type=F32)

    @pl.when(j == 0)
    def _():
        def hn(x):
            return (_rms(x, g_ref[...]) * (1.0 + sc_ref[...]) + sh_ref[...]).astype(BF16)
        h_ref[0:HALO, :] = hn(xp_ref[...])
        h_ref[HALO:HALO + tm, :] = hn(x_ref[...])
        h_ref[HALO + tm:, :] = hn(xn_ref[...])
        acc_ref[...] = jnp.zeros_like(acc_ref)
        up(bufs[0])

    for parity in range(2):
        @pl.when((j > 0) & (j < nf) & (j % 2 == parity))
        def _():
            up(bufs[parity])
            down(bufs[1 - parity])

    @pl.when(j == nf)
    def _():
        down(bufs[(nf - 1) % 2])
        o_ref[...] = x_ref[...] + gate_ref[...] * acc_ref[...]


def _ffn_call(x, mods, layer, norm_g, w_up, conv_w, conv_b, w_down, geom):
    rows, d = x.shape
    dff = w_down.shape[1]
    tm = geom.row_tile(FFN_ROW_TILE)
    tf = FF_TILE
    nf = dff // tf
    grp = geom.group(tm)
    nhalo = rows // HALO
    cur = lambda j: jnp.minimum(j, nf - 1)
    prv = lambda j: jnp.maximum(j - 1, 0)
    return pl.pallas_call(
        functools.partial(_ffn_kernel, geom, nf),
        grid=(rows // tm, nf + 1),
        in_specs=[pl.BlockSpec((tm, d), lambda i, j: (i, 0)),
                  pl.BlockSpec((HALO, d), lambda i, j: (jnp.maximum(i * (tm // HALO) - 1, 0), 0)),
                  pl.BlockSpec((HALO, d), lambda i, j: (jnp.minimum((i + 1) * (tm // HALO), nhalo - 1), 0)),
                  pl.BlockSpec((None, 1, d), lambda i, j: (2 * layer + 1, 0, 0)),
                  _mod_spec(d, layer, 3, grp), _mod_spec(d, layer, 4, grp), _mod_spec(d, layer, 5, grp),
                  pl.BlockSpec((None, d, tf), lambda i, j: (layer, 0, cur(j))),
                  pl.BlockSpec((None, d, tf), lambda i, j: (layer, 0, nf + cur(j))),
                  pl.BlockSpec((None, 3, tf), lambda i, j: (layer, 0, prv(j))),
                  pl.BlockSpec((None, 3, tf), lambda i, j: (layer, 0, nf + prv(j))),
                  pl.BlockSpec((None, 1, tf), lambda i, j: (layer, 0, prv(j))),
                  pl.BlockSpec((None, 1, tf), lambda i, j: (layer, 0, nf + prv(j))),
                  pl.BlockSpec((None, tf, d), lambda i, j: (layer, prv(j), 0))],
        out_specs=pl.BlockSpec((tm, d), lambda i, j: (i, 0)),
        out_shape=jax.ShapeDtypeStruct((rows, d), F32),
        scratch_shapes=[pltpu.VMEM((tm + 2 * HALO, d), BF16), pltpu.VMEM((tm, d), F32)]
        + [pltpu.VMEM((tm + 2 * HALO, tf), F32) for _ in range(4)],
        compiler_params=_params(("arbitrary", "arbitrary")),
        name="conv_ffn",
    )(x, x, x, norm_g, mods, mods, mods, w_up, w_up, conv_w, conv_w, conv_b, conv_b, w_down)


def _final_kernel(x_ref, g_ref, o_ref):
    o_ref[...] = _rms(x_ref[...], g_ref[...])


def _final_call(x, g, row0, nrows):
    d = x.shape[1]
    tm = OUT_ROW_TILE
    while nrows % tm or row0 % tm:
        tm //= 2
    return pl.pallas_call(
        _final_kernel,
        grid=(nrows // tm,),
        in_specs=[pl.BlockSpec((tm, d), lambda i: (row0 // tm + i, 0)), pl.BlockSpec((1, d), lambda i: (0, 0))],
        out_specs=pl.BlockSpec((tm, d), lambda i: (i, 0)),
        out_shape=jax.ShapeDtypeStruct((nrows, d), F32),
        compiler_params=_params(("arbitrary",)),
        name="final_norm",
    )(x, g.reshape(1, d))


def _ssd_kernel(fwd, geom, heads, *refs):
    (xs_ref, xsp_ref, xsn_ref, bc_ref, bcp_ref, bcn_ref, dt_ref, cwx_ref, cbx_ref, cwb_ref, cbb_ref,
     dtb_ref, alog_ref, pe_ref, bd_ref, h0_ref) = refs[:16]
    if fwd:
        z_ref, yb_ref, dx_ref, gn_ref, out_ref, st_ref, s_ref, yacc_ref = refs[16:]
    else:
        out_ref, st_ref, s_ref = refs[16:]
    q = xs_ref.shape[0]
    seq, cpos, nc, c, _ = geom.chunk_pos(pl.program_id(0), q, fwd)
    first, last = c == 0, c == nc - 1
    half = LANES // 2

    @pl.when(cpos == 0)
    def _():
        s_ref[...] = jnp.where(seq < geom.n_ctx, 0.0, h0_ref[...])

    xs = _conv3_silu(xs_ref[...], xsp_ref[...], xsn_ref[...], cwx_ref[...], cbx_ref[...], first, last)
    bc = _conv3_silu(bc_ref[...], bcp_ref[...], bcn_ref[...], cwb_ref[...], cbb_ref[...], first, last)
    bm, cm = bc[:, :LANES], bc[:, LANES:]
    dt = _softplus(dt_ref[...] + dtb_ref[...])
    la = dt * (-jnp.exp(alog_ref[...]))
    row = lax.broadcasted_iota(jnp.int32, (q, q), 0)
    col = lax.broadcasted_iota(jnp.int32, (q, q), 1)
    tril, triu = row >= col, row <= col
    cs_p = _mask_sum(tril.astype(F32).astype(BF16), la)
    cs_s = _mask_sum(triu.astype(F32).astype(BF16), la)
    cs = cs_p if fwd else cs_s
    edge = q - 1 if fwd else 0
    ecs = jnp.exp(cs)
    ecs_x = _expand(ecs, pe_ref[...], pieces=2)
    wd_x = _expand(dt * jnp.exp(cs[edge:edge + 1, :] - cs), pe_ref[...], pieces=2)
    carry = _expand(jnp.broadcast_to(ecs[edge:edge + 1, :], (HALO, LANES)), pe_ref[...])[0:1, :]
    s_old = s_ref[...]
    y_off = _dot(cm, s_old) * ecs_x
    s_new = (s_old * carry + _dot_tn(bm, xs * wd_x)) * bd_ref[...]
    s_ref[...] = s_new

    @pl.when(cpos == nc - 1)
    def _():
        s_t = s_new.T
        p_dim = s_t.shape[0] // heads
        for h in range(heads):
            grp = h // (heads // 2)
            st_ref[h] = s_t[h * p_dim:(h + 1) * p_dim, grp * half:(grp + 1) * half]

    if not fwd:
        out_ref[...] = y_off
        return

    c2_p, c2_s = cs_p * LOG2E, cs_s * LOG2E
    ldt = jnp.log2(dt)
    rt_p, rt_s = (c2_p - ldt).T, (c2_s - ldt).T
    neg = -jnp.inf
    lane = lax.broadcasted_iota(jnp.int32, (q, LANES), 1)
    lo, hi = lane < half, lane >= half
    gmat = (_dot_nt(jnp.where(lo, cm, 0.0), bm), _dot_nt(jnp.where(hi, cm, 0.0), bm))
    for p in range(heads // 2):
        xp = xs[:, p * LANES:(p + 1) * LANES]
        acc = None
        for hh in range(2):
            h = 2 * p + hh
            hb = heads + h
            lf = jnp.exp2(jnp.where(tril, c2_p[:, h:h + 1] - rt_p[h:h + 1, :], neg))
            lb = jnp.exp2(jnp.where(triu, c2_s[:, hb:hb + 1] - rt_s[hb:hb + 1, :], neg))
            w = gmat[h // (heads // 2)] * (lf + lb)
            part = _dot(w, jnp.where(lo if hh == 0 else hi, xp, 0.0))
            acc = part if acc is None else acc + part
        yacc_ref[:, p * LANES:(p + 1) * LANES] = acc
    y = yacc_ref[...] + y_off + yb_ref[...] + dx_ref[...] * xs
    yz = y * _silu(z_ref[...])
    out_ref[...] = _rms(yz, gn_ref[...]).astype(BF16)


def _ssd_call(fwd, proj, yb, prm, h0, geom, heads, width):
    rows = proj.shape[0]
    q = geom.scan_chunk()
    nhalo = rows // HALO
    bc_w = 2 * LANES
    xs_blk, z_blk, bc_blk, dt_blk = 0, 1, 5 * width // bc_w, (5 * width + bc_w) // LANES

    def seq(t):
        return geom.chunk_pos(t, q, fwd)[0]

    def rowblk(t):
        return geom.chunk_pos(t, q, fwd)[4]

    def prev(t):
        return jnp.maximum(rowblk(t) * (q // HALO) - 1, 0)

    def nxt(t):
        return jnp.minimum((rowblk(t) + 1) * (q // HALO), nhalo - 1)

    const = lambda t: (0, 0)
    in_specs = [pl.BlockSpec((q, width), lambda t: (rowblk(t), xs_blk)),
                pl.BlockSpec((HALO, width), lambda t: (prev(t), xs_blk)),
                pl.BlockSpec((HALO, width), lambda t: (nxt(t), xs_blk)),
                pl.BlockSpec((q, bc_w), lambda t: (rowblk(t), bc_blk)),
                pl.BlockSpec((HALO, bc_w), lambda t: (prev(t), bc_blk)),
                pl.BlockSpec((HALO, bc_w), lambda t: (nxt(t), bc_blk)),
                pl.BlockSpec((q, LANES), lambda t: (rowblk(t), dt_blk)),
                pl.BlockSpec((3, width), const), pl.BlockSpec((1, width), const),
                pl.BlockSpec((3, bc_w), const), pl.BlockSpec((1, bc_w), const),
                pl.BlockSpec((1, LANES), const), pl.BlockSpec((1, LANES), const),
                pl.BlockSpec((LANES, width), const), pl.BlockSpec((LANES, width), const),
                pl.BlockSpec((None, LANES, width), lambda t: (jnp.maximum(seq(t) - geom.n_ctx, 0), 0, 0))]
    args = [proj, proj, proj, proj, proj, proj, proj, prm["cwx"], prm["cbx"], prm["cwb"], prm["cbb"],
            prm["dtb"], prm["alog"], prm["pf"] if fwd else prm["pb"], prm["bd"], h0]
    scratch = [pltpu.VMEM((LANES, width), F32)]
    if fwd:
        in_specs += [pl.BlockSpec((q, width), lambda t: (rowblk(t), z_blk)),
                     pl.BlockSpec((q, width), lambda t: (rowblk(t), 0)),
                     pl.BlockSpec((1, width), const), pl.BlockSpec((1, width), const)]
        args += [proj, yb, prm["dx"], prm["gn"]]
        scratch.append(pltpu.VMEM((q, width), F32))
        out_dtype = BF16
    else:
        out_dtype = F32
    return pl.pallas_call(
        functools.partial(_ssd_kernel, fwd, geom, heads),
        grid=(rows // q,),
        in_specs=in_specs,
        out_specs=[pl.BlockSpec((q, width), lambda t: (rowblk(t), 0)),
                   pl.BlockSpec((None, heads, width // heads, LANES // 2), lambda t: (seq(t), 0, 0, 0))],
        out_shape=[jax.ShapeDtypeStruct((rows, width), out_dtype),
                   jax.ShapeDtypeStruct((geom.n_ctx + geom.n_lat, heads, width // heads, LANES // 2), F32)],
        scratch_shapes=scratch,
        compiler_params=_params(("arbitrary",)),
        name="ssd_fwd" if fwd else "ssd_bwd",
    )(*args)


def _rope(x, cos, sin_signed):
    lane = lax.broadcasted_iota(jnp.int32, x.shape, 1)
    partner = jnp.where((lane & 31) < 16, pltpu.roll(x, LANES - 16, 1), pltpu.roll(x, 16, 1))
    return x * cos + partner * sin_signed


def _attn_kernel(lam_init, hps, q_ref, k_ref, v_ref, kc_ref, vc_ref, cq_ref, sq_ref, ck_ref, sk_ref, lp_ref, g_ref,
                 o_ref, kr_ref, vr_ref):
    hs = range(hps)
    sls = [slice(h * LANES, (h + 1) * LANES) for h in hs]

    @pl.when(pl.program_id(2) == 0)
    def _():
        for sl in sls:
            kr_ref[:, sl] = _rope(k_ref[:, sl], ck_ref[...], sk_ref[...]).astype(BF16)
        vr_ref[...] = v_ref[...].astype(BF16)

    scale = float(LANES // 2) ** -0.5 * LOG2E
    lane = lax.broadcasted_iota(jnp.int32, (q_ref.shape[0], LANES), 1)
    lo = lane < LANES // 2
    lp = lp_ref[...]
    lam = (jnp.exp(jnp.sum(lp[0:1] * lp[1:2], axis=1, keepdims=True))
           - jnp.exp(jnp.sum(lp[2:3] * lp[3:4], axis=1, keepdims=True)) + lam_init)
    q = [_rope(q_ref[:, sl], cq_ref[...], sq_ref[...]) * scale for sl in sls]
    qm = [(jnp.where(lo, q[h], 0.0).astype(BF16), jnp.where(lo, 0.0, q[h]).astype(BF16)) for h in hs]
    nt = (((1,), (1,)), ((), ()))
    s_own = [[lax.dot_general(qm[h][m], kr_ref[:, sls[h]], nt, preferred_element_type=F32) for m in range(2)]
             for h in hs]
    s_ctx = [[lax.dot_general(qm[h][m], kc_ref[:, sls[h]].astype(BF16), nt, preferred_element_type=F32)
              for m in range(2)] for h in hs]
    pd_own, pd_ctx = [], []
    for h in hs:
        po, pc, coef = [], [], []
        for m in range(2):
            mx = jnp.maximum(jnp.max(s_own[h][m], axis=1, keepdims=True), jnp.max(s_ctx[h][m], axis=1, keepdims=True))
            po.append(jnp.exp2(s_own[h][m] - mx))
            pc.append(jnp.exp2(s_ctx[h][m] - mx))
            den = jnp.sum(po[m], axis=1, keepdims=True) + jnp.sum(pc[m], axis=1, keepdims=True)
            coef.append(1.0 / den if m == 0 else lam / den)
        pd_own.append((po[0] * coef[0] - po[1] * coef[1]).astype(BF16))
        pd_ctx.append((pc[0] * coef[0] - pc[1] * coef[1]).astype(BF16))
    o = [jnp.dot(pd_own[h], vr_ref[:, sls[h]], preferred_element_type=F32)
         + jnp.dot(pd_ctx[h], vc_ref[:, sls[h]].astype(BF16), preferred_element_type=F32) for h in hs]
    for h in hs:
        o_ref[:, sls[h]] = (_rms(o[h], g_ref[...]) * (1.0 - lam_init)).astype(BF16)


def _attn_ctx_kernel(lam_init, heads, q_ref, k_ref, v_ref, lp_ref, g_ref, o_ref, ko_ref, vo_ref):
    for h in range(heads):
        ko_ref[:, h, :] = k_ref[:, h * LANES:(h + 1) * LANES]
        vo_ref[:, h, :] = v_ref[:, h * LANES:(h + 1) * LANES]
    rows = q_ref.shape[0]
    lane = lax.broadcasted_iota(jnp.int32, (rows, LANES), 1)
    lo = lane < LANES // 2
    scale = float(LANES // 2) ** -0.5 * LOG2E
    lp = lp_ref[...]
    lam = (jnp.exp(jnp.sum(lp[0:1] * lp[1:2], axis=1, keepdims=True))
           - jnp.exp(jnp.sum(lp[2:3] * lp[3:4], axis=1, keepdims=True)) + lam_init)
    hs = range(heads)
    sls = [slice(h * LANES, (h + 1) * LANES) for h in hs]
    q = [q_ref[:, sl] * scale for sl in sls]
    k = [k_ref[:, sl].astype(BF16) for sl in sls]
    s0 = [_dot_nt(jnp.where(lo, q[h], 0.0), k[h]) for h in hs]
    s1 = [_dot_nt(jnp.where(lo, 0.0, q[h]), k[h]) for h in hs]
    pd = []
    for h in hs:
        p0 = jnp.exp2(s0[h] - jnp.max(s0[h], axis=1, keepdims=True))
        p1 = jnp.exp2(s1[h] - jnp.max(s1[h], axis=1, keepdims=True))
        c0 = 1.0 / jnp.sum(p0, axis=1, keepdims=True)
        c1 = lam / jnp.sum(p1, axis=1, keepdims=True)
        pd.append(p0 * c0 - p1 * c1)
    o = [_dot(pd[h], v_ref[:, sls[h]]) for h in hs]
    for h in hs:
        o_ref[:, sls[h]] = (_rms(o[h], g_ref[...]) * (1.0 - lam_init)).astype(BF16)


def _attn_ctx_call(lam_init, proj, lam_p, norm_g, n_seq, seq_len, heads, width):
    spec = lambda blk: pl.BlockSpec((seq_len, width), lambda b: (b, blk))
    return pl.pallas_call(
        functools.partial(_attn_ctx_kernel, lam_init, heads),
        grid=(n_seq,),
        in_specs=[spec(2), spec(3), spec(4), pl.BlockSpec(lam_p.shape, lambda b: (0, 0)),
                  pl.BlockSpec((1, LANES), lambda b: (0, 0))],
        out_specs=[pl.BlockSpec((seq_len, width), lambda b: (b, 0)),
                   pl.BlockSpec((None, seq_len, heads, LANES), lambda b: (b, 0, 0, 0)),
                   pl.BlockSpec((None, seq_len, heads, LANES), lambda b: (b, 0, 0, 0))],
        out_shape=[jax.ShapeDtypeStruct((n_seq * seq_len, width), BF16),
                   jax.ShapeDtypeStruct((n_seq, seq_len, heads, LANES), F32),
                   jax.ShapeDtypeStruct((n_seq, seq_len, heads, LANES), F32)],
        compiler_params=_params(("arbitrary",)),
        name="diff_attn_ctx",
    )(proj, proj, proj, lam_p, norm_g.reshape(1, LANES))


def _attn_call(lam_init, proj, cache_k, cache_v, tables, lam_p, norm_g, row0, n_seq, seq_len, heads, width):
    tq = min(Q_TILE, seq_len)
    nq = seq_len // tq
    hps = ATTN_HEADS_PER_STEP
    hw = hps * LANES
    ngrp = heads // hps
    q_blk, k_blk, v_blk = 2 * ngrp, 3 * ngrp, 4 * ngrp
    qb, sb = row0 // tq, row0 // seq_len
    past = cache_k.shape[1]
    const = lambda b, h, i: (0, 0)
    in_specs = [pl.BlockSpec((tq, hw), lambda b, h, i: (qb + b * nq + i, q_blk + h)),
                pl.BlockSpec((seq_len, hw), lambda b, h, i: (sb + b, k_blk + h)),
                pl.BlockSpec((seq_len, hw), lambda b, h, i: (sb + b, v_blk + h)),
                pl.BlockSpec((None, past, hw), lambda b, h, i: (b, 0, h)),
                pl.BlockSpec((None, past, hw), lambda b, h, i: (b, 0, h)),
                pl.BlockSpec((tq, LANES), lambda b, h, i: (i, 0)),
                pl.BlockSpec((tq, LANES), lambda b, h, i: (i, 0)),
                pl.BlockSpec((seq_len, LANES), const), pl.BlockSpec((seq_len, LANES), const),
                pl.BlockSpec(lam_p.shape, const), pl.BlockSpec((1, LANES), const)]
    return pl.pallas_call(
        functools.partial(_attn_kernel, lam_init, hps),
        grid=(n_seq, ngrp, nq),
        in_specs=in_specs,
        out_specs=pl.BlockSpec((tq, hw), lambda b, h, i: (b * nq + i, h)),
        out_shape=jax.ShapeDtypeStruct((n_seq * seq_len, width), BF16),
        scratch_shapes=[pltpu.VMEM((seq_len, hw), BF16), pltpu.VMEM((seq_len, hw), BF16)],
        compiler_params=_params(("arbitrary", "arbitrary", "arbitrary")),
        name="diff_attn_lat",
    )(proj, proj, proj, cache_k, cache_v, tables[0], tables[1], tables[0], tables[1], lam_p,
      norm_g.reshape(1, LANES))


def _rope_tables(n_tokens, grid_w):
    n_freq = LANES // 8
    pos = jnp.arange(n_tokens)
    r = (pos // grid_w).astype(F32)
    cpos = (pos % grid_w).astype(F32)
    inv = ROPE_BASE ** (-jnp.arange(n_freq, dtype=F32) / n_freq)
    ang_r, ang_c = r[:, None] * inv, cpos[:, None] * inv
    cos32 = lambda a: jnp.concatenate([jnp.cos(a), jnp.cos(a)], axis=1)
    sin32 = lambda a: jnp.concatenate([-jnp.sin(a), jnp.sin(a)], axis=1)
    cos64 = jnp.concatenate([cos32(ang_r), cos32(ang_c)], axis=1)
    sin64 = jnp.concatenate([sin32(ang_r), sin32(ang_c)], axis=1)
    return jnp.concatenate([cos64, cos64], axis=1), jnp.concatenate([sin64, sin64], axis=1)


def _gdn_masks(rb, sb, fwd):
    shift = GDN_CHUNK.bit_length() - 1

    def tri(n):
        row = jnp.arange(n)[:, None]
        col = jnp.arange(n)[None, :]
        blk = (row >> shift) == (col >> shift)
        return row, col, blk, blk & ((row >= col) if fwd else (row <= col))

    row, col, blk, incl = tri(sb)
    strict = blk & ((row > col) if fwd else (row < col))
    ms = [incl, strict, row == col, (row >> 1) == (col >> 1)]
    for lvl in range(1, shift):
        ms.append(((row >> (lvl + 1)) == (col >> (lvl + 1))) & ((row >> lvl) != (col >> lvl)))
    _, _, blk_rb, incl_rb = tri(rb)
    return jnp.stack(ms).astype(F32), jnp.stack([incl_rb, blk_rb]).astype(BF16)


def _gdn_kernel(fwd, geom, heads, hps, *refs):
    (q_ref, qp_ref, qn_ref, k_ref, kp_ref, kn_ref, v_ref, vp_ref, vn_ref, ab_ref,
     cwq_ref, cbq_ref, cwk_ref, cbk_ref, cwv_ref, cbv_ref, alog_ref, dtb_ref, mf_ref, mb_ref, s0_ref) = refs[:21]
    if fwd:
        z_ref, ob_ref, gn_ref, out_ref, st_ref, s_ref = refs[21:]
    else:
        out_ref, st_ref, s_ref = refs[21:]
    hg = pl.program_id(0)
    rb = q_ref.shape[0]
    seq, cpos, nb, c, _ = geom.chunk_pos(pl.program_id(1), rb, fwd)
    first, last = c == 0, c == nb - 1
    dk = q_ref.shape[1] // hps
    ch = GDN_CHUNK
    nsub = rb // ch
    n_lvl = ch.bit_length() - 1

    @pl.when(cpos == 0)
    def _():
        s_ref[...] = jnp.where(seq < geom.n_ctx, 0.0, s0_ref[...])

    ab = ab_ref[...]
    d0 = 0 if fwd else 2 * heads
    g_all = -jnp.exp(alog_ref[...]) * _softplus(ab + dtb_ref[...])
    b_all = 1.0 / (1.0 + jnp.exp(-ab))
    gc_all = _mask_sum(mb_ref[0], g_all)
    gt_all = _mask_sum(mb_ref[1], g_all)
    gct_all = gc_all.T
    lane = lax.broadcasted_iota(jnp.int32, ab.shape, 1)
    sub = lax.broadcasted_iota(jnp.int32, gct_all.shape, 0)
    incl_f, strict_f, eye, pair = mf_ref[0], mf_ref[1], mf_ref[2], mf_ref[3]
    sb = mf_ref.shape[1]
    cps = sb // ch

    def pick(a, l):
        return jnp.sum(jnp.where(lane == l, a, 0.0), axis=1, keepdims=True)

    hs = range(hps)
    sls = [slice(hh * dk, (hh + 1) * dk) for hh in hs]
    cs = [(hh, blk) for hh in hs for blk in range(rb // sb)]
    nch = range(len(cs))
    kn, qn, vb, gc, gtot, beta, dec = ([] for _ in range(7))
    for hh in hs:
        hd = hg * hps + hh
        sl = sls[hh]

        def conv(x_ref, xp_ref, xn_ref, cw_ref, cb_ref):
            return _conv3_silu(x_ref[:, sl], xp_ref[:, sl], xn_ref[:, sl], cw_ref[:, sl], cb_ref[:, sl], first, last)

        qc = conv(q_ref, qp_ref, qn_ref, cwq_ref, cbq_ref)
        kc = conv(k_ref, kp_ref, kn_ref, cwk_ref, cbk_ref)
        vh = conv(v_ref, vp_ref, vn_ref, cwv_ref, cbv_ref)
        qh = qc * lax.rsqrt(jnp.sum(qc * qc, axis=1, keepdims=True) + EPS) * (float(dk) ** -0.5)
        kh = kc * lax.rsqrt(jnp.sum(kc * kc, axis=1, keepdims=True) + EPS)
        gch, gth = pick(gc_all, d0 + hd), pick(gt_all, d0 + hd)
        bh = pick(b_all, d0 + heads + hd)
        g_row = jnp.sum(jnp.where(sub == d0 + hd, gct_all, 0.0), axis=0, keepdims=True)
        for blk in range(rb // sb):
            r = slice(blk * sb, (blk + 1) * sb)
            qn.append(qh[r]); kn.append(kh[r]); gc.append(gch[r]); gtot.append(gth[r]); beta.append(bh[r])
            vb.append(vh[r] * bh[r])
            dec.append(jnp.exp(jnp.minimum(gch[r] - g_row[:, r], 0.0)) * incl_f)
    kk = [_dot_nt(kn[c], kn[c]) for c in nch]
    m = [(kk[c] * beta[c]) * (dec[c] * strict_f) for c in nch]
    t = [eye - m[c] * pair for c in nch]
    for lvl in range(1, n_lvl):
        off = mf_ref[3 + lvl]
        tm = [_dot(t[c], m[c] * off) for c in nch]
        tmt = [_dot(tm[c], t[c]) for c in nch]
        t = [t[c] - tmt[c] for c in nch]
    eg = [jnp.exp(gc[c]) for c in nch]
    sol = [_dot(t[c], jnp.concatenate([vb[c], kn[c] * (beta[c] * eg[c])], axis=1)) for c in nch]
    qk = [_dot_nt(qn[c], kn[c]) for c in nch]
    qg = [qn[c] * eg[c] for c in nch]
    kg = [kn[c] * jnp.exp(gtot[c] - gc[c]) for c in nch]
    egl = [jnp.exp(gtot[c]) for c in nch]
    s = [s_ref[hh] for hh in hs]
    vnew = [[None] * cps for _ in nch]
    qs = [[None] * cps for _ in nch]
    for i in (range(nsub) if fwd else reversed(range(nsub))):
        blk, li = divmod(i, cps)
        r = slice(li * ch, (li + 1) * ch)
        idx = [hh * (rb // sb) + blk for hh in hs]
        ws = [_dot(jnp.concatenate([sol[c][r, dk:], qg[c][r]], axis=0), s[hh]) for hh, c in zip(hs, idx)]
        for hh, c in zip(hs, idx):
            vnew[c][li] = sol[c][r, :dk] - ws[hh][:ch]
            qs[c][li] = ws[hh][ch:]
        upd = [_dot_tn(kg[c][r], vnew[c][li]) for c in idx]
        s = [s[hh] * egl[c][li * ch:li * ch + 1, :] + upd[hh] for hh, c in zip(hs, idx)]
    intra = [_dot(qk[c] * dec[c], jnp.concatenate(vnew[c], axis=0)) for c in nch]
    for c, (hh, blk) in enumerate(cs):
        sl = sls[hh]
        r = slice(blk * sb, (blk + 1) * sb)
        o = jnp.concatenate(qs[c], axis=0) + intra[c]
        if fwd:
            o = o + ob_ref[r, sl]
            out_ref[r, sl] = (_rms(o, gn_ref[...]) * _silu(z_ref[r, sl])).astype(BF16)
        else:
            out_ref[r, sl] = o
    for hh in hs:
        s_ref[hh] = s[hh]

    @pl.when(cpos == nb - 1)
    def _():
        st_ref[...] = s_ref[...]


def _gdn_call(fwd, proj, ob, prm, s0, geom, heads, width):
    rows = proj.shape[0]
    rb = geom.scan_chunk()
    nhalo = rows // HALO
    dk = width // heads
    hps = GDN_HEADS_PER_STEP
    hw = hps * dk
    ngrp = heads // hps
    ab_blk = 4 * width // LANES
    mf, mb = _gdn_masks(rb, min(GDN_SYSTEM_ROWS, rb), fwd)

    def seq(t):
        return geom.chunk_pos(t, rb, fwd)[0]

    def rowblk(t):
        return geom.chunk_pos(t, rb, fwd)[4]

    def prev(t):
        return jnp.maximum(rowblk(t) * (rb // HALO) - 1, 0)

    def nxt(t):
        return jnp.minimum((rowblk(t) + 1) * (rb // HALO), nhalo - 1)

    in_specs, args = [], []
    for sec in range(3):
        in_specs += [pl.BlockSpec((rb, hw), lambda h, t, sec=sec: (rowblk(t), sec * ngrp + h)),
                     pl.BlockSpec((HALO, hw), lambda h, t, sec=sec: (prev(t), sec * ngrp + h)),
                     pl.BlockSpec((HALO, hw), lambda h, t, sec=sec: (nxt(t), sec * ngrp + h))]
        args += [proj, proj, proj]
    in_specs.append(pl.BlockSpec((rb, LANES), lambda h, t: (rowblk(t), ab_blk)))
    args.append(proj)
    for sec in range(3):
        in_specs += [pl.BlockSpec((3, hw), lambda h, t, sec=sec: (0, sec * ngrp + h)),
                     pl.BlockSpec((1, hw), lambda h, t, sec=sec: (0, sec * ngrp + h))]
        args += [prm["cw"], prm["cb"]]
    in_specs += [pl.BlockSpec((1, LANES), lambda h, t: (0, 0)), pl.BlockSpec((1, LANES), lambda h, t: (0, 0)),
                 pl.BlockSpec(mf.shape, lambda h, t: (0, 0, 0)), pl.BlockSpec(mb.shape, lambda h, t: (0, 0, 0)),
                 pl.BlockSpec((None, hps, dk, dk), lambda h, t: (jnp.maximum(seq(t) - geom.n_ctx, 0), h, 0, 0))]
    args += [prm["alog"], prm["dtb"], mf, mb, s0]
    if fwd:
        in_specs += [pl.BlockSpec((rb, hw), lambda h, t: (rowblk(t), 3 * ngrp + h)),
                     pl.BlockSpec((rb, hw), lambda h, t: (rowblk(t), h)),
                     pl.BlockSpec((1, dk), lambda h, t: (0, 0))]
        args += [proj, ob, prm["gn"]]
    return pl.pallas_call(
        functools.partial(_gdn_kernel, fwd, geom, heads, hps),
        grid=(ngrp, rows // rb),
        in_specs=in_specs,
        out_specs=[pl.BlockSpec((rb, hw), lambda h, t: (rowblk(t), h)),
                   pl.BlockSpec((None, hps, dk, dk), lambda h, t: (seq(t), h, 0, 0))],
        out_shape=[jax.ShapeDtypeStruct((rows, width), BF16 if fwd else F32),
                   jax.ShapeDtypeStruct((geom.n_ctx + geom.n_lat, heads, dk, dk), F32)],
        scratch_shapes=[pltpu.VMEM((hps, dk, dk), F32)],
        compiler_params=_params(("arbitrary", "arbitrary")),
        name="gdn_fwd" if fwd else "gdn_bwd",
    )(*args)


def _pack_states_kernel(n_in, *refs):
    o_ref = refs[n_in]
    for i in range(n_in):
        o_ref[i] = refs[i][...]


def _pack_states_call(states, n_keep):
    h, a, b = states[0].shape[1:]
    n = len(states)
    return pl.pallas_call(
        functools.partial(_pack_states_kernel, n),
        grid=(n_keep,),
        in_specs=[pl.BlockSpec((None, h, a, b), lambda i: (i, 0, 0, 0))] * n,
        out_specs=pl.BlockSpec((None, n, h, a, b), lambda i: (i, 0, 0, 0, 0)),
        out_shape=jax.ShapeDtypeStruct((n_keep, n, h, a, b), F32),
        compiler_params=_params(("arbitrary",)),
        name="pack_states",
    )(*states)


def _pad_cols(w, n):
    return jnp.pad(w, ((0, 0), (0, n - w.shape[1])))


def _lane_row(v):
    v = v.reshape(1, -1).astype(F32)
    return _pad_cols(v, LANES)


def _even_layer(x, mods, layer, e, geom, norm_g, w_in, conv_w, conv_b, dt_bias, a_log, d_skip, ssd_g,
                lam_p, da_g, w_out, cache_k, cache_v, state_ssd, tables):
    d = x.shape[1]
    heads, p_dim, n_state = state_ssd.shape[3], state_ssd.shape[4], state_ssd.shape[5]
    width = heads * p_dim
    da_heads = cache_k.shape[3]
    assert width == d and da_heads * LANES == d and 2 * n_state == LANES and 2 * p_dim == LANES
    bc_w = 2 * LANES
    n_dt = 2 * heads
    o_z, o_xs, o_bc, o_dt = 0, width, 2 * width, 2 * width + bc_w
    o_q = o_dt + n_dt
    cols = [w_in[:, o_xs:o_xs + width], w_in[:, o_z:o_z + width], w_in[:, o_q:o_q + 3 * d],
            w_in[:, o_bc:o_bc + bc_w], w_in[:, o_dt:o_dt + n_dt]]
    n_used = 5 * width + bc_w + n_dt
    n_pad = -(-n_used // IN_COL_ALIGN) * IN_COL_ALIGN
    w_perm = _pad_cols(jnp.concatenate(cols, axis=1), n_pad).astype(BF16)
    proj = _inproj_call(x, mods, layer, norm_g, w_perm, geom)

    hl = jnp.arange(LANES)[:, None]
    cl = jnp.arange(width)[None, :]
    prm = {
        "cwx": conv_w[:, :width], "cbx": conv_b[None, :width],
        "cwb": conv_w[:, width:], "cbb": conv_b[None, width:],
        "dtb": _lane_row(dt_bias), "alog": _lane_row(a_log),
        "pf": (hl == cl // p_dim).astype(BF16), "pb": (hl == heads + cl // p_dim).astype(BF16),
        "bd": (hl // n_state == cl // (width // 2)).astype(F32),
        "dx": jnp.repeat(d_skip, p_dim)[None, :], "gn": ssd_g[None, :],
    }

    def state_in(st):
        t = st.transpose(0, 3, 1, 2).reshape(st.shape[0], n_state, width)
        return jnp.concatenate([t, t], axis=1) * prm["bd"]

    h0f, h0b = state_in(state_ssd[:, e, 0].astype(F32)), state_in(state_ssd[:, e, 1].astype(F32))
    yb, st_b = _ssd_call(False, proj, None, prm, h0b, geom, heads, width)
    y, st_f = _ssd_call(True, proj, yb, prm, h0f, geom, heads, width)
    new_states = [st_f, st_b]
    lam_init = 0.8 - 0.6 * math.exp(-0.3 * layer)
    o_ctx, k_ctx, v_ctx = _attn_ctx_call(lam_init, proj, lam_p, da_g, geom.n_ctx, geom.s_ctx, da_heads, d)
    o_lat = _attn_call(lam_init, proj, cache_k[:, e].reshape(geom.n_lat, -1, d),
                       cache_v[:, e].reshape(geom.n_lat, -1, d), tables, lam_p, da_g, geom.ctx_rows,
                       geom.n_lat, geom.s_lat, da_heads, d)
    w_o = w_out.astype(BF16)
    x = _outproj_call(x, mods, layer, [y, (o_ctx, o_lat)], [w_o[:width], w_o[width:]], geom)
    return x, k_ctx, v_ctx, new_states


def _odd_layer(x, mods, layer, o, geom, norm_g, w_in, conv_w, conv_b, a_log, dt_bias, g_norm, w_out, state_gdn):
    d = x.shape[1]
    heads, dk, dv = state_gdn.shape[3], state_gdn.shape[4], state_gdn.shape[5]
    width = heads * dv
    assert width == d and dk == LANES and dv == LANES and 4 * heads <= LANES
    n_used = 4 * width + 4 * heads
    n_pad = -(-n_used // IN_COL_ALIGN) * IN_COL_ALIGN
    w_perm = _pad_cols(w_in, n_pad).astype(BF16)
    proj = _inproj_call(x, mods, layer, norm_g, w_perm, geom)

    def ab_row(v):
        r = jnp.zeros((2, 2 * heads), F32).at[:, :heads].set(v.astype(F32))
        return _lane_row(r)

    prm = {"cw": conv_w, "cb": conv_b[None, :], "alog": ab_row(a_log), "dtb": ab_row(dt_bias), "gn": g_norm[None, :]}
    s0f, s0b = state_gdn[:, o, 0].astype(F32), state_gdn[:, o, 1].astype(F32)
    ob, st_b = _gdn_call(False, proj, None, prm, s0b, geom, heads, width)
    og, st_f = _gdn_call(True, proj, ob, prm, s0f, geom, heads, width)
    new_states = [st_f, st_b]
    x = _outproj_call(x, mods, layer, [og], [w_out.astype(BF16)], geom)
    return x, new_states


def kernel(x_prompt, x_sample, cache_attn_k, cache_attn_v, state_ssd, state_gdn, c, c_ctx, ada_w, ada_b, norm_g, ev_w_in, ev_conv_w, ev_conv_b, ssd_dt_bias, ssd_a_log, ssd_d, ssd_norm_g, da_lambda, da_norm_g, ev_w_out, od_w_in, od_conv_w, od_conv_b, gdn_a_log, gdn_dt_bias, gdn_norm_g, od_w_out, ffn_w_up, ffn_conv_w, ffn_conv_b, ffn_w_down, final_norm_g):
    n_ctx, s_ctx, d = x_prompt.shape
    n_lat, s_lat, _ = x_sample.shape
    depth = ada_w.shape[0]
    assert n_lat + 1 <= MOD_ROWS and s_ctx & (s_ctx - 1) == 0 and s_lat & (s_lat - 1) == 0
    geom = _Geom(n_ctx, s_ctx, n_lat, s_lat)
    dtype = x_prompt.dtype
    x = jnp.concatenate([x_prompt.reshape(-1, d), x_sample.reshape(-1, d)], axis=0).astype(F32)

    cvec = jnp.zeros((MOD_ROWS, d), F32).at[0].set(c_ctx).at[1:1 + n_lat].set(c)
    mods = _mod_call(cvec, ada_w, ada_b)
    mods = mods.reshape(depth, MOD_ROWS, N_MOD, d).transpose(0, 2, 1, 3).reshape(depth * N_MOD * MOD_ROWS, 1, d)

    grid_w = 64
    tables = _rope_tables(s_lat, grid_w)
    ks_out, vs_out, ssd_out, gdn_out = [], [], [], []
    for l in range(depth):
        if l % 2 == 0:
            e = l // 2
            x, k_new, v_new, st = _even_layer(
                x, mods, l, e, geom, norm_g[l, 0], ev_w_in[e], ev_conv_w[e], ev_conv_b[e], ssd_dt_bias[e],
                ssd_a_log[e], ssd_d[e], ssd_norm_g[e], da_lambda[e], da_norm_g[e], ev_w_out[e],
                cache_attn_k, cache_attn_v, state_ssd, tables)
            ks_out.append(k_new)
            vs_out.append(v_new)
            ssd_out += st
        else:
            o = l // 2
            x, st = _odd_layer(x, mods, l, o, geom, norm_g[l, 0], od_w_in[o], od_conv_w[o], od_conv_b[o],
                               gdn_a_log[o], gdn_dt_bias[o], gdn_norm_g[o], od_w_out[o], state_gdn)
            gdn_out += st
        x = _ffn_call(x, mods, l, norm_g.reshape(2 * depth, 1, d), ffn_w_up, ffn_conv_w,
                      ffn_conv_b.reshape(depth, 1, -1), ffn_w_down, geom)
    y_prompt = _final_call(x, final_norm_g, 0, geom.ctx_rows).reshape(n_ctx, s_ctx, d)
    y_sample = _final_call(x, final_norm_g, geom.ctx_rows, n_lat * s_lat).reshape(n_lat, s_lat, d)
    def states(parts):
        packed = _pack_states_call(parts, n_ctx)
        return packed.reshape((n_ctx, len(parts) // 2, 2) + packed.shape[2:])

    return (y_prompt.astype(dtype), y_sample.astype(dtype),
            jnp.stack(ks_out, axis=1).astype(dtype), jnp.stack(vs_out, axis=1).astype(dtype),
            states(ssd_out).astype(dtype), states(gdn_out).astype(dtype))
```

```python
import functools
import math

import jax
import jax.numpy as jnp
from jax import lax
from jax.experimental import pallas as pl
from jax.experimental.pallas import tpu as pltpu

F32 = jnp.float32
BF16 = jnp.bfloat16
EPS = 1e-6
ROPE_BASE = 10000.0
LOG2E = 1.4426950408889634

LANES = 128
HALO = 8
VMEM_LIMIT_BYTES = 56 * 1024 * 1024

N_MOD = 6
MOD_ROWS = 8
SCAN_CHUNK = 256
GDN_CHUNK = 64
GDN_HEADS_PER_STEP = 8
GDN_SYSTEM_ROWS = 128
Q_TILE = 256
ATTN_HEADS_PER_STEP = 2
ROW_TILE = 1024
OUT_ROW_TILE = 512
FFN_ROW_TILE = 1024
FF_TILE = 256
FFN_ROW_CHUNK = 256
IN_COL_STEPS = 2
IN_COL_ALIGN = IN_COL_STEPS * LANES


def _silu(x):
    return x / (1.0 + jnp.exp(-x))


def _softplus(x):
    return jnp.maximum(x, 0.0) + jnp.log1p(jnp.exp(-jnp.abs(x)))


def _dot(a, b):
    return jnp.dot(a.astype(BF16), b.astype(BF16), preferred_element_type=F32)


def _dot_nt(a, b):
    return lax.dot_general(a.astype(BF16), b.astype(BF16), (((1,), (1,)), ((), ())),
                           preferred_element_type=F32)


def _dot_tn(a, b):
    return lax.dot_general(a.astype(BF16), b.astype(BF16), (((0,), (0,)), ((), ())),
                           preferred_element_type=F32)


def _split3(a):
    hi = a.astype(BF16)
    r1 = a - hi.astype(F32)
    mid = r1.astype(BF16)
    return hi, mid, (r1 - mid.astype(F32)).astype(BF16)


def _expand(a, p, pieces=3):
    d = functools.partial(jnp.dot, preferred_element_type=F32)
    hi, mid, lo = _split3(a)
    out = d(hi, p) + d(mid, p)
    return out + d(lo, p) if pieces == 3 else out


def _mask_sum(m, a):
    d = functools.partial(jnp.dot, preferred_element_type=F32)
    hi, mid, lo = _split3(a)
    return d(m, hi) + d(m, mid) + d(m, lo)


def _rms(x, g):
    ms = jnp.mean(x * x, axis=-1, keepdims=True)
    return x * lax.rsqrt(ms + EPS) * g


def _params(sem):
    return pltpu.CompilerParams(dimension_semantics=sem, vmem_limit_bytes=VMEM_LIMIT_BYTES)


def _conv3_silu(x, xp, xn, w, b, first, last):
    rows = x.shape[0]
    row = lax.broadcasted_iota(jnp.int32, x.shape, 0)
    prev_row = jnp.where(first, 0.0, xp[HALO - 1:HALO, :])
    next_row = jnp.where(last, 0.0, xn[0:1, :])
    x_prev = jnp.where(row == 0, prev_row, pltpu.roll(x, 1, 0))
    x_next = jnp.where(row == rows - 1, next_row, pltpu.roll(x, rows - 1, 0))
    y = x_prev * w[0:1, :] + x * w[1:2, :] + x_next * w[2:3, :] + b
    return _silu(y)


def _mod_kernel(c_ref, w_ref, b_ref, o_ref):
    s = _silu(c_ref[...])
    o_ref[...] = _dot(s, w_ref[...]) + b_ref[...]


def _mod_call(cvec, ada_w, ada_b):
    depth, d, n = ada_w.shape
    tn = d
    return pl.pallas_call(
        _mod_kernel,
        grid=(depth, n // tn),
        in_specs=[pl.BlockSpec((MOD_ROWS, d), lambda l, j: (0, 0)),
                  pl.BlockSpec((None, d, tn), lambda l, j: (l, 0, j)),
                  pl.BlockSpec((None, 1, tn), lambda l, j: (l, 0, j))],
        out_specs=pl.BlockSpec((None, MOD_ROWS, tn), lambda l, j: (l, 0, j)),
        out_shape=jax.ShapeDtypeStruct((depth, MOD_ROWS, n), F32),
        compiler_params=_params(("arbitrary", "arbitrary")),
        name="ada_mod",
    )(cvec, ada_w, ada_b.reshape(depth, 1, n))


class _Geom:
    def __init__(self, n_ctx, s_ctx, n_lat, s_lat):
        self.n_ctx, self.s_ctx, self.n_lat, self.s_lat = n_ctx, s_ctx, n_lat, s_lat
        self.ctx_rows = n_ctx * s_ctx
        self.rows = self.ctx_rows + n_lat * s_lat

    def row_tile(self, want):
        t = want
        while self.ctx_rows % t or self.s_lat % t:
            t //= 2
        return t

    def scan_chunk(self):
        return min(SCAN_CHUNK, self.s_ctx, self.s_lat)

    def chunk_pos(self, t, q, fwd):
        ctx_blocks = self.ctx_rows // q
        nc_ctx, nc_lat = self.s_ctx // q, self.s_lat // q
        is_ctx = t < ctx_blocks
        tl = jnp.maximum(t - ctx_blocks, 0)
        seq = jnp.where(is_ctx, t >> (nc_ctx.bit_length() - 1), self.n_ctx + (tl >> (nc_lat.bit_length() - 1)))
        cpos = jnp.where(is_ctx, t & (nc_ctx - 1), tl & (nc_lat - 1))
        nc = jnp.where(is_ctx, nc_ctx, nc_lat)
        c = cpos if fwd else nc - 1 - cpos
        return seq, cpos, nc, c, t - cpos + c

    def group(self, tile):
        def fn(i):
            r = i * tile
            return jnp.where(r < self.ctx_rows, 0,
                             1 + (jnp.maximum(r - self.ctx_rows, 0) >> (self.s_lat.bit_length() - 1)))
        return fn


def _mod_spec(d, layer, which, group_fn):
    return pl.BlockSpec((None, 1, d), lambda i, *_: ((layer * N_MOD + which) * MOD_ROWS + group_fn(i), 0, 0))


def _inproj_kernel(x_ref, g_ref, sh_ref, sc_ref, w_ref, o_ref, h_ref):
    @pl.when(pl.program_id(1) == 0)
    def _():
        h_ref[...] = (_rms(x_ref[...], g_ref[...]) * (1.0 + sc_ref[...]) + sh_ref[...]).astype(BF16)

    o_ref[...] = jnp.dot(h_ref[...], w_ref[...], preferred_element_type=F32)


def _inproj_call(x, mods, layer, norm_g, w, geom):
    rows, d = x.shape
    n = w.shape[1]
    tm = geom.row_tile(ROW_TILE)
    tn = n // IN_COL_STEPS
    grp = geom.group(tm)
    return pl.pallas_call(
        _inproj_kernel,
        grid=(rows // tm, n // tn),
        in_specs=[pl.BlockSpec((tm, d), lambda i, j: (i, 0)),
                  pl.BlockSpec((1, d), lambda i, j: (0, 0)),
                  _mod_spec(d, layer, 0, grp),
                  _mod_spec(d, layer, 1, grp),
                  pl.BlockSpec((d, tn), lambda i, j: (0, j))],
        out_specs=pl.BlockSpec((tm, tn), lambda i, j: (i, j)),
        out_shape=jax.ShapeDtypeStruct((rows, n), F32),
        scratch_shapes=[pltpu.VMEM((tm, d), BF16)],
        compiler_params=_params(("arbitrary", "arbitrary")),
        name="in_proj",
    )(x, norm_g.reshape(1, d), mods, mods, w)


def _outproj_kernel(split, ctx_tiles, x_ref, gate_ref, *refs):
    n_a = sum(2 if s else 1 for s in split)
    a_refs, w_refs, o_ref = list(refs[:n_a]), refs[n_a:n_a + len(split)], refs[n_a + len(split)]
    is_ctx = pl.program_id(0) < ctx_tiles
    acc = None
    for s, w_ref in zip(split, w_refs):
        a = a_refs.pop(0)[...]
        if s:
            a = jnp.where(is_ctx, a, a_refs.pop(0)[...])
        part = jnp.dot(a, w_ref[...], preferred_element_type=F32)
        acc = part if acc is None else acc + part
    o_ref[...] = x_ref[...] + gate_ref[...] * acc


def _outproj_call(x, mods, layer, acts, ws, geom):
    rows, d = x.shape
    tm = geom.row_tile(OUT_ROW_TILE)
    grp = geom.group(tm)
    ctx_tiles = geom.ctx_rows // tm
    split = tuple(isinstance(a, tuple) for a in acts)
    in_specs = [pl.BlockSpec((tm, d), lambda i: (i, 0)), _mod_spec(d, layer, 2, grp)]
    flat = []
    for a in acts:
        if isinstance(a, tuple):
            in_specs += [pl.BlockSpec((tm, a[0].shape[1]), lambda i: (jnp.minimum(i, ctx_tiles - 1), 0)),
                         pl.BlockSpec((tm, a[1].shape[1]), lambda i: (jnp.maximum(i - ctx_tiles, 0), 0))]
            flat += list(a)
        else:
            in_specs.append(pl.BlockSpec((tm, a.shape[1]), lambda i: (i, 0)))
            flat.append(a)
    in_specs += [pl.BlockSpec(w.shape, lambda i: (0, 0)) for w in ws]
    return pl.pallas_call(
        functools.partial(_outproj_kernel, split, ctx_tiles),
        grid=(rows // tm,),
        in_specs=in_specs,
        out_specs=pl.BlockSpec((tm, d), lambda i: (i, 0)),
        out_shape=jax.ShapeDtypeStruct((rows, d), F32),
        compiler_params=_params(("arbitrary",)),
        name="out_proj",
    )(x, mods, *flat, *ws)


def _ffn_kernel(geom, nf, x_ref, xp_ref, xn_ref, g_ref, sh_ref, sc_ref, gate_ref,
                wg_ref, wv_ref, cwg_ref, cwv_ref, cbg_ref, cbv_ref, wd_ref, o_ref, h_ref, acc_ref, *u_refs):
    i, j = pl.program_id(0), pl.program_id(1)
    tm = x_ref.shape[0]
    bufs = (u_refs[0:2], u_refs[2:4])

    def up(buf):
        h = h_ref[...]
        tf = wg_ref.shape[1]
        w = jnp.concatenate([wg_ref[...].astype(BF16), wv_ref[...].astype(BF16)], axis=1)
        u = jnp.dot(h, w, preferred_element_type=F32)
        buf[0][...] = u[:, :tf]
        buf[1][...] = u[:, tf:]

    def down(buf):
        rc = min(FFN_ROW_CHUNK, tm)
        row0 = i * tm
        is_ctx = row0 < geom.ctx_rows
        base = jnp.where(is_ctx, row0, row0 - geom.ctx_rows)
        lseq = jnp.where(is_ctx, geom.s_ctx, geom.s_lat)
        rowi = lax.broadcasted_iota(jnp.int32, (rc, LANES), 0)
        wd = wd_ref[...].astype(BF16)
        for r in range(0, tm, rc):
            pos = (base + r + rowi) & (lseq - 1)
            at_start, at_end = pos == 0, pos == lseq - 1

            def conv(u_ref, cw_ref, cb_ref):
                cols = []
                for c in range(0, u_ref.shape[1], LANES):
                    cs = slice(c, c + LANES)
                    up_ = jnp.where(at_start, 0.0, u_ref[HALO + r - 1:HALO + r - 1 + rc, cs])
                    un = jnp.where(at_end, 0.0, u_ref[HALO + r + 1:HALO + r + 1 + rc, cs])
                    cols.append(up_ * cw_ref[0:1, cs] + u_ref[HALO + r:HALO + r + rc, cs] * cw_ref[1:2, cs]
                                + un * cw_ref[2:3, cs] + cb_ref[:, cs])
                return jnp.concatenate(cols, axis=1)

            a = _silu(conv(buf[0], cwg_ref, cbg_ref)) * conv(buf[1], cwv_ref, cbv_ref)
            acc_ref[r:r + rc, :] += jnp.dot(a.astype(BF16), wd, preferred_element_type=F32)

    @pl.when(j == 0)
    def _():
        def hn(x):
            return (_rms(x, g_ref[...]) * (1.0 + sc_ref[...]) + sh_ref[...]).astype(BF16)
        h_ref[0:HALO, :] = hn(xp_ref[...])
        h_ref[HALO:HALO + tm, :] = hn(x_ref[...])
        h_ref[HALO + tm:, :] = hn(xn_ref[...])
        acc_ref[...] = jnp.zeros_like(acc_ref)
        up(bufs[0])

    for parity in range(2):
        @pl.when((j > 0) & (j < nf) & (j % 2 == parity))
        def _():
            up(bufs[parity])
            down(bufs[1 - parity])

    @pl.when(j == nf)
    def _():
        down(bufs[(nf - 1) % 2])
        o_ref[...] = x_ref[...] + gate_ref[...] * acc_ref[...]


def _ffn_call(x, mods, layer, norm_g, w_up, conv_w, conv_b, w_down, geom):
    rows, d = x.shape
    dff = w_down.shape[1]
    tm = geom.row_tile(FFN_ROW_TILE)
    tf = FF_TILE
    nf = dff // tf
    grp = geom.group(tm)
    nhalo = rows // HALO
    cur = lambda j: jnp.minimum(j, nf - 1)
    prv = lambda j: jnp.maximum(j - 1, 0)
    return pl.pallas_call(
        functools.partial(_ffn_kernel, geom, nf),
        grid=(rows // tm, nf + 1),
        in_specs=[pl.BlockSpec((tm, d), lambda i, j: (i, 0)),
                  pl.BlockSpec((HALO, d), lambda i, j: (jnp.maximum(i * (tm // HALO) - 1, 0), 0)),
                  pl.BlockSpec((HALO, d), lambda i, j: (jnp.minimum((i + 1) * (tm // HALO), nhalo - 1), 0)),
                  pl.BlockSpec((None, 1, d), lambda i, j: (2 * layer + 1, 0, 0)),
                  _mod_spec(d, layer, 3, grp), _mod_spec(d, layer, 4, grp), _mod_spec(d, layer, 5, grp),
                  pl.BlockSpec((None, d, tf), lambda i, j: (layer, 0, cur(j))),
                  pl.BlockSpec((None, d, tf), lambda i, j: (layer, 0, nf + cur(j))),
                  pl.BlockSpec((None, 3, tf), lambda i, j: (layer, 0, prv(j))),
                  pl.BlockSpec((None, 3, tf), lambda i, j: (layer, 0, nf + prv(j))),
                  pl.BlockSpec((None, 1, tf), lambda i, j: (layer, 0, prv(j))),
                  pl.BlockSpec((None, 1, tf), lambda i, j: (layer, 0, nf + prv(j))),
                  pl.BlockSpec((None, tf, d), lambda i, j: (layer, prv(j), 0))],
        out_specs=pl.BlockSpec((tm, d), lambda i, j: (i, 0)),
        out_shape=jax.ShapeDtypeStruct((rows, d), F32),
        scratch_shapes=[pltpu.VMEM((tm + 2 * HALO, d), BF16), pltpu.VMEM((tm, d), F32)]
        + [pltpu.VMEM((tm + 2 * HALO, tf), F32) for _ in range(4)],
        compiler_params=_params(("arbitrary", "arbitrary")),
        name="conv_ffn",
    )(x, x, x, norm_g, mods, mods, mods, w_up, w_up, conv_w, conv_w, conv_b, conv_b, w_down)


def _final_kernel(x_ref, g_ref, o_ref):
    o_ref[...] = _rms(x_ref[...], g_ref[...])


def _final_call(x, g, row0, nrows):
    d = x.shape[1]
    tm = OUT_ROW_TILE
    while nrows % tm or row0 % tm:
        tm //= 2
    return pl.pallas_call(
        _final_kernel,
        grid=(nrows // tm,),
        in_specs=[pl.BlockSpec((tm, d), lambda i: (row0 // tm + i, 0)), pl.BlockSpec((1, d), lambda i: (0, 0))],
        out_specs=pl.BlockSpec((tm, d), lambda i: (i, 0)),
        out_shape=jax.ShapeDtypeStruct((nrows, d), F32),
        compiler_params=_params(("arbitrary",)),
        name="final_norm",
    )(x, g.reshape(1, d))


def _ssd_kernel(fwd, geom, heads, *refs):
    (xs_ref, xsp_ref, xsn_ref, bc_ref, bcp_ref, bcn_ref, dt_ref, cwx_ref, cbx_ref, cwb_ref, cbb_ref,
     dtb_ref, alog_ref, pe_ref, bd_ref, h0_ref) = refs[:16]
    if fwd:
        z_ref, yb_ref, dx_ref, gn_ref, out_ref, st_ref, s_ref, yacc_ref = refs[16:]
    else:
        out_ref, st_ref, s_ref = refs[16:]
    q = xs_ref.shape[0]
    seq, cpos, nc, c, _ = geom.chunk_pos(pl.program_id(0), q, fwd)
    first, last = c == 0, c == nc - 1
    half = LANES // 2

    @pl.when(cpos == 0)
    def _():
        s_ref[...] = jnp.where(seq < geom.n_ctx, 0.0, h0_ref[...])

    xs = _conv3_silu(xs_ref[...], xsp_ref[...], xsn_ref[...], cwx_ref[...], cbx_ref[...], first, last)
    bc = _conv3_silu(bc_ref[...], bcp_ref[...], bcn_ref[...], cwb_ref[...], cbb_ref[...], first, last)
    bm, cm = bc[:, :LANES], bc[:, LANES:]
    dt = _softplus(dt_ref[...] + dtb_ref[...])
    la = dt * (-jnp.exp(alog_ref[...]))
    row = lax.broadcasted_iota(jnp.int32, (q, q), 0)
    col = lax.broadcasted_iota(jnp.int32, (q, q), 1)
    tril, triu = row >= col, row <= col
    cs_p = _mask_sum(tril.astype(F32).astype(BF16), la)
    cs_s = _mask_sum(triu.astype(F32).astype(BF16), la)
    cs = cs_p if fwd else cs_s
    edge = q - 1 if fwd else 0
    ecs = jnp.exp(cs)
    ecs_x = _expand(ecs, pe_ref[...], pieces=2)
    wd_x = _expand(dt * jnp.exp(cs[edge:edge + 1, :] - cs), pe_ref[...], pieces=2)
    carry = _expand(jnp.broadcast_to(ecs[edge:edge + 1, :], (HALO, LANES)), pe_ref[...])[0:1, :]
    s_old = s_ref[...]
    y_off = _dot(cm, s_old) * ecs_x
    s_new = (s_old * carry + _dot_tn(bm, xs * wd_x)) * bd_ref[...]
    s_ref[...] = s_new

    @pl.when(cpos == nc - 1)
    def _():
        s_t = s_new.T
        p_dim = s_t.shape[0] // heads
        for h in range(heads):
            grp = h // (heads // 2)
            st_ref[h] = s_t[h * p_dim:(h + 1) * p_dim, grp * half:(grp + 1) * half]

    if not fwd:
        out_ref[...] = y_off
        return

    c2_p, c2_s = cs_p * LOG2E, cs_s * LOG2E
    ldt = jnp.log2(dt)
    rt_p, rt_s = (c2_p - ldt).T, (c2_s - ldt).T
    neg = -jnp.inf
    lane = lax.broadcasted_iota(jnp.int32, (q, LANES), 1)
    lo, hi = lane < half, lane >= half
    gmat = (_dot_nt(jnp.where(lo, cm, 0.0), bm), _dot_nt(jnp.where(hi, cm, 0.0), bm))
    for p in range(heads // 2):
        xp = xs[:, p * LANES:(p + 1) * LANES]
        acc = None
        for hh in range(2):
            h = 2 * p + hh
            hb = heads + h
            lf = jnp.exp2(jnp.where(tril, c2_p[:, h:h + 1] - rt_p[h:h + 1, :], neg))
            lb = jnp.exp2(jnp.where(triu, c2_s[:, hb:hb + 1] - rt_s[hb:hb + 1, :], neg))
            w = gmat[h // (heads // 2)] * (lf + lb)
            part = _dot(w, jnp.where(lo if hh == 0 else hi, xp, 0.0))
            acc = part if acc is None else acc + part
        yacc_ref[:, p * LANES:(p + 1) * LANES] = acc
    y = yacc_ref[...] + y_off + yb_ref[...] + dx_ref[...] * xs
    yz = y * _silu(z_ref[...])
    out_ref[...] = _rms(yz, gn_ref[...]).astype(BF16)


def _ssd_call(fwd, proj, yb, prm, h0, geom, heads, width):
    rows = proj.shape[0]
    q = geom.scan_chunk()
    nhalo = rows // HALO
    bc_w = 2 * LANES
    xs_blk, z_blk, bc_blk, dt_blk = 0, 1, 5 * width // bc_w, (5 * width + bc_w) // LANES

    def seq(t):
        return geom.chunk_pos(t, q, fwd)[0]

    def rowblk(t):
        return geom.chunk_pos(t, q, fwd)[4]

    def prev(t):
        return jnp.maximum(rowblk(t) * (q // HALO) - 1, 0)

    def nxt(t):
        return jnp.minimum((rowblk(t) + 1) * (q // HALO), nhalo - 1)

    const = lambda t: (0, 0)
    in_specs = [pl.BlockSpec((q, width), lambda t: (rowblk(t), xs_blk)),
                pl.BlockSpec((HALO, width), lambda t: (prev(t), xs_blk)),
                pl.BlockSpec((HALO, width), lambda t: (nxt(t), xs_blk)),
                pl.BlockSpec((q, bc_w), lambda t: (rowblk(t), bc_blk)),
                pl.BlockSpec((HALO, bc_w), lambda t: (prev(t), bc_blk)),
                pl.BlockSpec((HALO, bc_w), lambda t: (nxt(t), bc_blk)),
                pl.BlockSpec((q, LANES), lambda t: (rowblk(t), dt_blk)),
                pl.BlockSpec((3, width), const), pl.BlockSpec((1, width), const),
                pl.BlockSpec((3, bc_w), const), pl.BlockSpec((1, bc_w), const),
                pl.BlockSpec((1, LANES), const), pl.BlockSpec((1, LANES), const),
                pl.BlockSpec((LANES, width), const), pl.BlockSpec((LANES, width), const),
                pl.BlockSpec((None, LANES, width), lambda t: (jnp.maximum(seq(t) - geom.n_ctx, 0), 0, 0))]
    args = [proj, proj, proj, proj, proj, proj, proj, prm["cwx"], prm["cbx"], prm["cwb"], prm["cbb"],
            prm["dtb"], prm["alog"], prm["pf"] if fwd else prm["pb"], prm["bd"], h0]
    scratch = [pltpu.VMEM((LANES, width), F32)]
    if fwd:
        in_specs += [pl.BlockSpec((q, width), lambda t: (rowblk(t), z_blk)),
                     pl.BlockSpec((q, width), lambda t: (rowblk(t), 0)),
                     pl.BlockSpec((1, width), const), pl.BlockSpec((1, width), const)]
        args += [proj, yb, prm["dx"], prm["gn"]]
        scratch.append(pltpu.VMEM((q, width), F32))
        out_dtype = BF16
    else:
        out_dtype = F32
    return pl.pallas_call(
        functools.partial(_ssd_kernel, fwd, geom, heads),
        grid=(rows // q,),
        in_specs=in_specs,
        out_specs=[pl.BlockSpec((q, width), lambda t: (rowblk(t), 0)),
                   pl.BlockSpec((None, heads, width // heads, LANES // 2), lambda t: (seq(t), 0, 0, 0))],
        out_shape=[jax.ShapeDtypeStruct((rows, width), out_dtype),
                   jax.ShapeDtypeStruct((geom.n_ctx + geom.n_lat, heads, width // heads, LANES // 2), F32)],
        scratch_shapes=scratch,
        compiler_params=_params(("arbitrary",)),
        name="ssd_fwd" if fwd else "ssd_bwd",
    )(*args)


def _rope(x, cos, sin_signed):
    lane = lax.broadcasted_iota(jnp.int32, x.shape, 1)
    partner = jnp.where((lane & 31) < 16, pltpu.roll(x, LANES - 16, 1), pltpu.roll(x, 16, 1))
    return x * cos + partner * sin_signed


def _attn_kernel(lam_init, hps, q_ref, k_ref, v_ref, kc_ref, vc_ref, cq_ref, sq_ref, ck_ref, sk_ref, lp_ref, g_ref,
                 o_ref, kr_ref, vr_ref):
    hs = range(hps)
    sls = [slice(h * LANES, (h + 1) * LANES) for h in hs]

    @pl.when(pl.program_id(2) == 0)
    def _():
        for sl in sls:
            kr_ref[:, sl] = _rope(k_ref[:, sl], ck_ref[...], sk_ref[...]).astype(BF16)
        vr_ref[...] = v_ref[...].astype(BF16)

    scale = float(LANES // 2) ** -0.5 * LOG2E
    lane = lax.broadcasted_iota(jnp.int32, (q_ref.shape[0], LANES), 1)
    lo = lane < LANES // 2
    lp = lp_ref[...]
    lam = (jnp.exp(jnp.sum(lp[0:1] * lp[1:2], axis=1, keepdims=True))
           - jnp.exp(jnp.sum(lp[2:3] * lp[3:4], axis=1, keepdims=True)) + lam_init)
    q = [_rope(q_ref[:, sl], cq_ref[...], sq_ref[...]) * scale for sl in sls]
    qm = [(jnp.where(lo, q[h], 0.0).astype(BF16), jnp.where(lo, 0.0, q[h]).astype(BF16)) for h in hs]
    nt = (((1,), (1,)), ((), ()))
    s_own = [[lax.dot_general(qm[h][m], kr_ref[:, sls[h]], nt, preferred_element_type=F32) for m in range(2)]
             for h in hs]
    s_ctx = [[lax.dot_general(qm[h][m], kc_ref[:, sls[h]].astype(BF16), nt, preferred_element_type=F32)
              for m in range(2)] for h in hs]
    pd_own, pd_ctx = [], []
    for h in hs:
        po, pc, coef = [], [], []
        for m in range(2):
            mx = jnp.maximum(jnp.max(s_own[h][m], axis=1, keepdims=True), jnp.max(s_ctx[h][m], axis=1, keepdims=True))
            po.append(jnp.exp2(s_own[h][m] - mx))
            pc.append(jnp.exp2(s_ctx[h][m] - mx))
            den = jnp.sum(po[m], axis=1, keepdims=True) + jnp.sum(pc[m], axis=1, keepdims=True)
            coef.append(1.0 / den if m == 0 else lam / den)
        pd_own.append((po[0] * coef[0] - po[1] * coef[1]).astype(BF16))
        pd_ctx.append((pc[0] * coef[0] - pc[1] * coef[1]).astype(BF16))
    o = [jnp.dot(pd_own[h], vr_ref[:, sls[h]], preferred_element_type=F32)
         + jnp.dot(pd_ctx[h], vc_ref[:, sls[h]].astype(BF16), preferred_element_type=F32) for h in hs]
    for h in hs:
        o_ref[:, sls[h]] = (_rms(o[h], g_ref[...]) * (1.0 - lam_init)).astype(BF16)


def _attn_ctx_kernel(lam_init, heads, n_prev, q_ref, k_ref, v_ref, lp_ref, g_ref, *refs):
    if n_prev:
        kp_ref, vp_ref, o_ref, ko_ref, vo_ref = refs
        ko_ref[0:n_prev] = kp_ref[...]
        vo_ref[0:n_prev] = vp_ref[...]
    else:
        o_ref, ko_ref, vo_ref = refs
    for h in range(heads):
        ko_ref[n_prev, :, h, :] = k_ref[:, h * LANES:(h + 1) * LANES]
        vo_ref[n_prev, :, h, :] = v_ref[:, h * LANES:(h + 1) * LANES]
    rows = q_ref.shape[0]
    lane = lax.broadcasted_iota(jnp.int32, (rows, LANES), 1)
    lo = lane < LANES // 2
    scale = float(LANES // 2) ** -0.5 * LOG2E
    lp = lp_ref[...]
    lam = (jnp.exp(jnp.sum(lp[0:1] * lp[1:2], axis=1, keepdims=True))
           - jnp.exp(jnp.sum(lp[2:3] * lp[3:4], axis=1, keepdims=True)) + lam_init)
    hs = range(heads)
    sls = [slice(h * LANES, (h + 1) * LANES) for h in hs]
    q = [q_ref[:, sl] * scale for sl in sls]
    k = [k_ref[:, sl].astype(BF16) for sl in sls]
    s0 = [_dot_nt(jnp.where(lo, q[h], 0.0), k[h]) for h in hs]
    s1 = [_dot_nt(jnp.where(lo, 0.0, q[h]), k[h]) for h in hs]
    pd = []
    for h in hs:
        p0 = jnp.exp2(s0[h] - jnp.max(s0[h], axis=1, keepdims=True))
        p1 = jnp.exp2(s1[h] - jnp.max(s1[h], axis=1, keepdims=True))
        c0 = 1.0 / jnp.sum(p0, axis=1, keepdims=True)
        c1 = lam / jnp.sum(p1, axis=1, keepdims=True)
        pd.append(p0 * c0 - p1 * c1)
    o = [_dot(pd[h], v_ref[:, sls[h]]) for h in hs]
    for h in hs:
        o_ref[:, sls[h]] = (_rms(o[h], g_ref[...]) * (1.0 - lam_init)).astype(BF16)


def _attn_ctx_call(lam_init, proj, lam_p, norm_g, prev_kv, n_seq, seq_len, heads, width):
    spec = lambda blk: pl.BlockSpec((seq_len, width), lambda b: (b, blk))
    n_prev = 0 if prev_kv is None else prev_kv[0].shape[1]
    cache_spec = lambda n: pl.BlockSpec((None, n, seq_len, heads, LANES), lambda b: (b, 0, 0, 0, 0))
    cache_shape = jax.ShapeDtypeStruct((n_seq, n_prev + 1, seq_len, heads, LANES), F32)
    in_specs = [spec(2), spec(3), spec(4), pl.BlockSpec(lam_p.shape, lambda b: (0, 0)),
                pl.BlockSpec((1, LANES), lambda b: (0, 0))]
    args = [proj, proj, proj, lam_p, norm_g.reshape(1, LANES)]
    if n_prev:
        in_specs += [cache_spec(n_prev), cache_spec(n_prev)]
        args += list(prev_kv)
    return pl.pallas_call(
        functools.partial(_attn_ctx_kernel, lam_init, heads, n_prev),
        grid=(n_seq,),
        in_specs=in_specs,
        out_specs=[pl.BlockSpec((seq_len, width), lambda b: (b, 0)), cache_spec(n_prev + 1), cache_spec(n_prev + 1)],
        out_shape=[jax.ShapeDtypeStruct((n_seq * seq_len, width), BF16), cache_shape, cache_shape],
        compiler_params=_params(("arbitrary",)),
        name="diff_attn_ctx",
    )(*args)


def _attn_call(lam_init, proj, cache_k, cache_v, tables, lam_p, norm_g, row0, n_seq, seq_len, heads, width):
    tq = min(Q_TILE, seq_len)
    nq = seq_len // tq
    hps = ATTN_HEADS_PER_STEP
    hw = hps * LANES
    ngrp = heads // hps
    q_blk, k_blk, v_blk = 2 * ngrp, 3 * ngrp, 4 * ngrp
    qb, sb = row0 // tq, row0 // seq_len
    past = cache_k.shape[1]
    const = lambda b, h, i: (0, 0)
    in_specs = [pl.BlockSpec((tq, hw), lambda b, h, i: (qb + b * nq + i, q_blk + h)),
                pl.BlockSpec((seq_len, hw), lambda b, h, i: (sb + b, k_blk + h)),
                pl.BlockSpec((seq_len, hw), lambda b, h, i: (sb + b, v_blk + h)),
                pl.BlockSpec((None, past, hw), lambda b, h, i: (b, 0, h)),
                pl.BlockSpec((None, past, hw), lambda b, h, i: (b, 0, h)),
                pl.BlockSpec((tq, LANES), lambda b, h, i: (i, 0)),
                pl.BlockSpec((tq, LANES), lambda b, h, i: (i, 0)),
                pl.BlockSpec((seq_len, LANES), const), pl.BlockSpec((seq_len, LANES), const),
                pl.BlockSpec(lam_p.shape, const), pl.BlockSpec((1, LANES), const)]
    return pl.pallas_call(
        functools.partial(_attn_kernel, lam_init, hps),
        grid=(n_seq, ngrp, nq),
        in_specs=in_specs,
        out_specs=pl.BlockSpec((tq, hw), lambda b, h, i: (b * nq + i, h)),
        out_shape=jax.ShapeDtypeStruct((n_seq * seq_len, width), BF16),
        scratch_shapes=[pltpu.VMEM((seq_len, hw), BF16), pltpu.VMEM((seq_len, hw), BF16)],
        compiler_params=_params(("arbitrary", "arbitrary", "arbitrary")),
        name="diff_attn_lat",
    )(proj, proj, proj, cache_k, cache_v, tables[0], tables[1], tables[0], tables[1], lam_p,
      norm_g.reshape(1, LANES))


def _rope_tables(n_tokens, grid_w):
    n_freq = LANES // 8
    pos = jnp.arange(n_tokens)
    r = (pos // grid_w).astype(F32)
    cpos = (pos % grid_w).astype(F32)
    inv = ROPE_BASE ** (-jnp.arange(n_freq, dtype=F32) / n_freq)
    ang_r, ang_c = r[:, None] * inv, cpos[:, None] * inv
    cos32 = lambda a: jnp.concatenate([jnp.cos(a), jnp.cos(a)], axis=1)
    sin32 = lambda a: jnp.concatenate([-jnp.sin(a), jnp.sin(a)], axis=1)
    cos64 = jnp.concatenate([cos32(ang_r), cos32(ang_c)], axis=1)
    sin64 = jnp.concatenate([sin32(ang_r), sin32(ang_c)], axis=1)
    return jnp.concatenate([cos64, cos64], axis=1), jnp.concatenate([sin64, sin64], axis=1)


def _gdn_masks(rb, sb, fwd):
    shift = GDN_CHUNK.bit_length() - 1

    def tri(n):
        row = jnp.arange(n)[:, None]
        col = jnp.arange(n)[None, :]
        blk = (row >> shift) == (col >> shift)
        return row, col, blk, blk & ((row >= col) if fwd else (row <= col))

    row, col, blk, incl = tri(sb)
    strict = blk & ((row > col) if fwd else (row < col))
    ms = [incl, strict, row == col, (row >> 1) == (col >> 1)]
    for lvl in range(1, shift):
        ms.append(((row >> (lvl + 1)) == (col >> (lvl + 1))) & ((row >> lvl) != (col >> lvl)))
    _, _, blk_rb, incl_rb = tri(rb)
    return jnp.stack(ms).astype(F32), jnp.stack([incl_rb, blk_rb]).astype(BF16)


def _gdn_kernel(fwd, geom, heads, hps, *refs):
    (q_ref, qp_ref, qn_ref, k_ref, kp_ref, kn_ref, v_ref, vp_ref, vn_ref, ab_ref,
     cwq_ref, cbq_ref, cwk_ref, cbk_ref, cwv_ref, cbv_ref, alog_ref, dtb_ref, mf_ref, mb_ref, s0_ref) = refs[:21]
    if fwd:
        z_ref, ob_ref, gn_ref, out_ref, st_ref, s_ref = refs[21:]
    else:
        out_ref, st_ref, s_ref = refs[21:]
    hg = pl.program_id(0)
    rb = q_ref.shape[0]
    seq, cpos, nb, c, _ = geom.chunk_pos(pl.program_id(1), rb, fwd)
    first, last = c == 0, c == nb - 1
    dk = q_ref.shape[1] // hps
    ch = GDN_CHUNK
    nsub = rb // ch
    n_lvl = ch.bit_length() - 1

    @pl.when(cpos == 0)
    def _():
        s_ref[...] = jnp.where(seq < geom.n_ctx, 0.0, s0_ref[...])

    ab = ab_ref[...]
    d0 = 0 if fwd else 2 * heads
    g_all = -jnp.exp(alog_ref[...]) * _softplus(ab + dtb_ref[...])
    b_all = 1.0 / (1.0 + jnp.exp(-ab))
    gc_all = _mask_sum(mb_ref[0], g_all)
    gt_all = _mask_sum(mb_ref[1], g_all)
    gct_all = gc_all.T
    lane = lax.broadcasted_iota(jnp.int32, ab.shape, 1)
    sub = lax.broadcasted_iota(jnp.int32, gct_all.shape, 0)
    incl_f, strict_f, eye, pair = mf_ref[0], mf_ref[1], mf_ref[2], mf_ref[3]
    sb = mf_ref.shape[1]
    cps = sb // ch

    def pick(a, l):
        return jnp.sum(jnp.where(lane == l, a, 0.0), axis=1, keepdims=True)

    hs = range(hps)
    sls = [slice(hh * dk, (hh + 1) * dk) for hh in hs]
    cs = [(hh, blk) for hh in hs for blk in range(rb // sb)]
    nch = range(len(cs))
    kn, qn, vb, gc, gtot, beta, dec = ([] for _ in range(7))
    for hh in hs:
        hd = hg * hps + hh
        sl = sls[hh]

        def conv(x_ref, xp_ref, xn_ref, cw_ref, cb_ref):
            return _conv3_silu(x_ref[:, sl], xp_ref[:, sl], xn_ref[:, sl], cw_ref[:, sl], cb_ref[:, sl], first, last)

        qc = conv(q_ref, qp_ref, qn_ref, cwq_ref, cbq_ref)
        kc = conv(k_ref, kp_ref, kn_ref, cwk_ref, cbk_ref)
        vh = conv(v_ref, vp_ref, vn_ref, cwv_ref, cbv_ref)
        qh = qc * lax.rsqrt(jnp.sum(qc * qc, axis=1, keepdims=True) + EPS) * (float(dk) ** -0.5)
        kh = kc * lax.rsqrt(jnp.sum(kc * kc, axis=1, keepdims=True) + EPS)
        gch, gth = pick(gc_all, d0 + hd), pick(gt_all, d0 + hd)
        bh = pick(b_all, d0 + heads + hd)
        g_row = jnp.sum(jnp.where(sub == d0 + hd, gct_all, 0.0), axis=0, keepdims=True)
        for blk in range(rb // sb):
            r = slice(blk * sb, (blk + 1) * sb)
            qn.append(qh[r]); kn.append(kh[r]); gc.append(gch[r]); gtot.append(gth[r]); beta.append(bh[r])
            vb.append(vh[r] * bh[r])
            dec.append(jnp.exp(jnp.minimum(gch[r] - g_row[:, r], 0.0)) * incl_f)
    kk = [_dot_nt(kn[c], kn[c]) for c in nch]
    m = [(kk[c] * beta[c]) * (dec[c] * strict_f) for c in nch]
    t = [eye - m[c] * pair for c in nch]
    for lvl in range(1, n_lvl):
        off = mf_ref[3 + lvl]
        tm = [_dot(t[c], m[c] * off) for c in nch]
        tmt = [_dot(tm[c], t[c]) for c in nch]
        t = [t[c] - tmt[c] for c in nch]
    eg = [jnp.exp(gc[c]) for c in nch]
    sol = [_dot(t[c], jnp.concatenate([vb[c], kn[c] * (beta[c] * eg[c])], axis=1)) for c in nch]
    qk = [_dot_nt(qn[c], kn[c]) for c in nch]
    qg = [qn[c] * eg[c] for c in nch]
    kg = [kn[c] * jnp.exp(gtot[c] - gc[c]) for c in nch]
    egl = [jnp.exp(gtot[c]) for c in nch]
    s = [s_ref[hh] for hh in hs]
    vnew = [[None] * cps for _ in nch]
    qs = [[None] * cps for _ in nch]
    for i in (range(nsub) if fwd else reversed(range(nsub))):
        blk, li = divmod(i, cps)
        r = slice(li * ch, (li + 1) * ch)
        idx = [hh * (rb // sb) + blk for hh in hs]
        ws = [_dot(jnp.concatenate([sol[c][r, dk:], qg[c][r]], axis=0), s[hh]) for hh, c in zip(hs, idx)]
        for hh, c in zip(hs, idx):
            vnew[c][li] = sol[c][r, :dk] - ws[hh][:ch]
            qs[c][li] = ws[hh][ch:]
        upd = [_dot_tn(kg[c][r], vnew[c][li]) for c in idx]
        s = [s[hh] * egl[c][li * ch:li * ch + 1, :] + upd[hh] for hh, c in zip(hs, idx)]
    intra = [_dot(qk[c] * dec[c], jnp.concatenate(vnew[c], axis=0)) for c in nch]
    for c, (hh, blk) in enumerate(cs):
        sl = sls[hh]
        r = slice(blk * sb, (blk + 1) * sb)
        o = jnp.concatenate(qs[c], axis=0) + intra[c]
        if fwd:
            o = o + ob_ref[r, sl]
            out_ref[r, sl] = (_rms(o, gn_ref[...]) * _silu(z_ref[r, sl])).astype(BF16)
        else:
            out_ref[r, sl] = o
    for hh in hs:
        s_ref[hh] = s[hh]

    @pl.when(cpos == nb - 1)
    def _():
        st_ref[...] = s_ref[...]


def _gdn_call(fwd, proj, ob, prm, s0, geom, heads, width):
    rows = proj.shape[0]
    rb = geom.scan_chunk()
    nhalo = rows // HALO
    dk = width // heads
    hps = GDN_HEADS_PER_STEP
    hw = hps * dk
    ngrp = heads // hps
    ab_blk = 4 * width // LANES
    mf, mb = _gdn_masks(rb, min(GDN_SYSTEM_ROWS, rb), fwd)

    def seq(t):
        return geom.chunk_pos(t, rb, fwd)[0]

    def rowblk(t):
        return geom.chunk_pos(t, rb, fwd)[4]

    def prev(t):
        return jnp.maximum(rowblk(t) * (rb // HALO) - 1, 0)

    def nxt(t):
        return jnp.minimum((rowblk(t) + 1) * (rb // HALO), nhalo - 1)

    in_specs, args = [], []
    for sec in range(3):
        in_specs += [pl.BlockSpec((rb, hw), lambda h, t, sec=sec: (rowblk(t), sec * ngrp + h)),
                     pl.BlockSpec((HALO, hw), lambda h, t, sec=sec: (prev(t), sec * ngrp + h)),
                     pl.BlockSpec((HALO, hw), lambda h, t, sec=sec: (nxt(t), sec * ngrp + h))]
        args += [proj, proj, proj]
    in_specs.append(pl.BlockSpec((rb, LANES), lambda h, t: (rowblk(t), ab_blk)))
    args.append(proj)
    for sec in range(3):
        in_specs += [pl.BlockSpec((3, hw), lambda h, t, sec=sec: (0, sec * ngrp + h)),
                     pl.BlockSpec((1, hw), lambda h, t, sec=sec: (0, sec * ngrp + h))]
        args += [prm["cw"], prm["cb"]]
    in_specs += [pl.BlockSpec((1, LANES), lambda h, t: (0, 0)), pl.BlockSpec((1, LANES), lambda h, t: (0, 0)),
                 pl.BlockSpec(mf.shape, lambda h, t: (0, 0, 0)), pl.BlockSpec(mb.shape, lambda h, t: (0, 0, 0)),
                 pl.BlockSpec((None, hps, dk, dk), lambda h, t: (jnp.maximum(seq(t) - geom.n_ctx, 0), h, 0, 0))]
    args += [prm["alog"], prm["dtb"], mf, mb, s0]
    if fwd:
        in_specs += [pl.BlockSpec((rb, hw), lambda h, t: (rowblk(t), 3 * ngrp + h)),
                     pl.BlockSpec((rb, hw), lambda h, t: (rowblk(t), h)),
                     pl.BlockSpec((1, dk), lambda h, t: (0, 0))]
        args += [proj, ob, prm["gn"]]
    return pl.pallas_call(
        functools.partial(_gdn_kernel, fwd, geom, heads, hps),
        grid=(ngrp, rows // rb),
        in_specs=in_specs,
        out_specs=[pl.BlockSpec((rb, hw), lambda h, t: (rowblk(t), h)),
                   pl.BlockSpec((None, hps, dk, dk), lambda h, t: (seq(t), h, 0, 0))],
        out_shape=[jax.ShapeDtypeStruct((rows, width), BF16 if fwd else F32),
                   jax.ShapeDtypeStruct((geom.n_ctx + geom.n_lat, heads, dk, dk), F32)],
        scratch_shapes=[pltpu.VMEM((hps, dk, dk), F32)],
        compiler_params=_params(("arbitrary", "arbitrary")),
        name="gdn_fwd" if fwd else "gdn_bwd",
    )(*args)


def _pack_states_kernel(n_in, *refs):
    o_ref = refs[n_in]
    for i in range(n_in):
        o_ref[i] = refs[i][...]


def _pack_states_call(states, n_keep):
    h, a, b = states[0].shape[1:]
    n = len(states)
    return pl.pallas_call(
        functools.partial(_pack_states_kernel, n),
        grid=(n_keep,),
        in_specs=[pl.BlockSpec((None, h, a, b), lambda i: (i, 0, 0, 0))] * n,
        out_specs=pl.BlockSpec((None, n, h, a, b), lambda i: (i, 0, 0, 0, 0)),
        out_shape=jax.ShapeDtypeStruct((n_keep, n, h, a, b), F32),
        compiler_params=_params(("arbitrary",)),
        name="pack_states",
    )(*states)


def _pad_cols(w, n):
    return jnp.pad(w, ((0, 0), (0, n - w.shape[1])))


def _lane_row(v):
    v = v.reshape(1, -1).astype(F32)
    return _pad_cols(v, LANES)


def _even_layer(x, mods, layer, e, geom, norm_g, w_in, conv_w, conv_b, dt_bias, a_log, d_skip, ssd_g,
                lam_p, da_g, w_out, cache_k, cache_v, state_ssd, tables, prev_kv):
    d = x.shape[1]
    heads, p_dim, n_state = state_ssd.shape[3], state_ssd.shape[4], state_ssd.shape[5]
    width = heads * p_dim
    da_heads = cache_k.shape[3]
    assert width == d and da_heads * LANES == d and 2 * n_state == LANES and 2 * p_dim == LANES
    bc_w = 2 * LANES
    n_dt = 2 * heads
    o_z, o_xs, o_bc, o_dt = 0, width, 2 * width, 2 * width + bc_w
    o_q = o_dt + n_dt
    cols = [w_in[:, o_xs:o_xs + width], w_in[:, o_z:o_z + width], w_in[:, o_q:o_q + 3 * d],
            w_in[:, o_bc:o_bc + bc_w], w_in[:, o_dt:o_dt + n_dt]]
    n_used = 5 * width + bc_w + n_dt
    n_pad = -(-n_used // IN_COL_ALIGN) * IN_COL_ALIGN
    w_perm = _pad_cols(jnp.concatenate(cols, axis=1), n_pad).astype(BF16)
    proj = _inproj_call(x, mods, layer, norm_g, w_perm, geom)

    hl = jnp.arange(LANES)[:, None]
    cl = jnp.arange(width)[None, :]
    prm = {
        "cwx": conv_w[:, :width], "cbx": conv_b[None, :width],
        "cwb": conv_w[:, width:], "cbb": conv_b[None, width:],
        "dtb": _lane_row(dt_bias), "alog": _lane_row(a_log),
        "pf": (hl == cl // p_dim).astype(BF16), "pb": (hl == heads + cl // p_dim).astype(BF16),
        "bd": (hl // n_state == cl // (width // 2)).astype(F32),
        "dx": jnp.repeat(d_skip, p_dim)[None, :], "gn": ssd_g[None, :],
    }

    def state_in(st):
        t = st.transpose(0, 3, 1, 2).reshape(st.shape[0], n_state, width)
        return jnp.concatenate([t, t], axis=1) * prm["bd"]

    h0f, h0b = state_in(state_ssd[:, e, 0].astype(F32)), state_in(state_ssd[:, e, 1].astype(F32))
    yb, st_b = _ssd_call(False, proj, None, prm, h0b, geom, heads, width)
    y, st_f = _ssd_call(True, proj, yb, prm, h0f, geom, heads, width)
    new_states = [st_f, st_b]
    lam_init = 0.8 - 0.6 * math.exp(-0.3 * layer)
    o_ctx, k_ctx, v_ctx = _attn_ctx_call(lam_init, proj, lam_p, da_g, prev_kv, geom.n_ctx, geom.s_ctx, da_heads, d)
    o_lat = _attn_call(lam_init, proj, cache_k[:, e].reshape(geom.n_lat, -1, d),
                       cache_v[:, e].reshape(geom.n_lat, -1, d), tables, lam_p, da_g, geom.ctx_rows,
                       geom.n_lat, geom.s_lat, da_heads, d)
    w_o = w_out.astype(BF16)
    x = _outproj_call(x, mods, layer, [y, (o_ctx, o_lat)], [w_o[:width], w_o[width:]], geom)
    return x, (k_ctx, v_ctx), new_states


def _odd_layer(x, mods, layer, o, geom, norm_g, w_in, conv_w, conv_b, a_log, dt_bias, g_norm, w_out, state_gdn):
    d = x.shape[1]
    heads, dk, dv = state_gdn.shape[3], state_gdn.shape[4], state_gdn.shape[5]
    width = heads * dv
    assert width == d and dk == LANES and dv == LANES and 4 * heads <= LANES
    n_used = 4 * width + 4 * heads
    n_pad = -(-n_used // IN_COL_ALIGN) * IN_COL_ALIGN
    w_perm = _pad_cols(w_in, n_pad).astype(BF16)
    proj = _inproj_call(x, mods, layer, norm_g, w_perm, geom)

    def ab_row(v):
        r = jnp.zeros((2, 2 * heads), F32).at[:, :heads].set(v.astype(F32))
        return _lane_row(r)

    prm = {"cw": conv_w, "cb": conv_b[None, :], "alog": ab_row(a_log), "dtb": ab_row(dt_bias), "gn": g_norm[None, :]}
    s0f, s0b = state_gdn[:, o, 0].astype(F32), state_gdn[:, o, 1].astype(F32)
    ob, st_b = _gdn_call(False, proj, None, prm, s0b, geom, heads, width)
    og, st_f = _gdn_call(True, proj, ob, prm, s0f, geom, heads, width)
    new_states = [st_f, st_b]
    x = _outproj_call(x, mods, layer, [og], [w_out.astype(BF16)], geom)
    return x, new_states


def kernel(x_prompt, x_sample, cache_attn_k, cache_attn_v, state_ssd, state_gdn, c, c_ctx, ada_w, ada_b, norm_g, ev_w_in, ev_conv_w, ev_conv_b, ssd_dt_bias, ssd_a_log, ssd_d, ssd_norm_g, da_lambda, da_norm_g, ev_w_out, od_w_in, od_conv_w, od_conv_b, gdn_a_log, gdn_dt_bias, gdn_norm_g, od_w_out, ffn_w_up, ffn_conv_w, ffn_conv_b, ffn_w_down, final_norm_g):
    n_ctx, s_ctx, d = x_prompt.shape
    n_lat, s_lat, _ = x_sample.shape
    depth = ada_w.shape[0]
    assert n_lat + 1 <= MOD_ROWS and s_ctx & (s_ctx - 1) == 0 and s_lat & (s_lat - 1) == 0
    geom = _Geom(n_ctx, s_ctx, n_lat, s_lat)
    dtype = x_prompt.dtype
    x = jnp.concatenate([x_prompt.reshape(-1, d), x_sample.reshape(-1, d)], axis=0).astype(F32)

    cvec = jnp.zeros((MOD_ROWS, d), F32).at[0].set(c_ctx).at[1:1 + n_lat].set(c)
    mods = _mod_call(cvec, ada_w, ada_b)
    mods = mods.reshape(depth, MOD_ROWS, N_MOD, d).transpose(0, 2, 1, 3).reshape(depth * N_MOD * MOD_ROWS, 1, d)

    grid_w = 64
    tables = _rope_tables(s_lat, grid_w)
    new_kv, ssd_out, gdn_out = None, [], []
    for l in range(depth):
        if l % 2 == 0:
            e = l // 2
            x, new_kv, st = _even_layer(
                x, mods, l, e, geom, norm_g[l, 0], ev_w_in[e], ev_conv_w[e], ev_conv_b[e], ssd_dt_bias[e],
                ssd_a_log[e], ssd_d[e], ssd_norm_g[e], da_lambda[e], da_norm_g[e], ev_w_out[e],
                cache_attn_k, cache_attn_v, state_ssd, tables, new_kv)
            ssd_out += st
        else:
            o = l // 2
            x, st = _odd_layer(x, mods, l, o, geom, norm_g[l, 0], od_w_in[o], od_conv_w[o], od_conv_b[o],
                               gdn_a_log[o], gdn_dt_bias[o], gdn_norm_g[o], od_w_out[o], state_gdn)
            gdn_out += st
        x = _ffn_call(x, mods, l, norm_g.reshape(2 * depth, 1, d), ffn_w_up, ffn_conv_w,
                      ffn_conv_b.reshape(depth, 1, -1), ffn_w_down, geom)
    y_prompt = _final_call(x, final_norm_g, 0, geom.ctx_rows).reshape(n_ctx, s_ctx, d)
    y_sample = _final_call(x, final_norm_g, geom.ctx_rows, n_lat * s_lat).reshape(n_lat, s_lat, d)
    def states(parts):
        packed = _pack_states_call(parts, n_ctx)
        return packed.reshape((n_ctx, len(parts) // 2, 2) + packed.shape[2:])

    return (y_prompt.astype(dtype), y_sample.astype(dtype),
            new_kv[0].astype(dtype), new_kv[1].astype(dtype),
            states(ssd_out).astype(dtype), states(gdn_out).astype(dtype))
```

```python
import functools
import math

import jax
import jax.numpy as jnp
from jax import lax
from jax.experimental import pallas as pl
from jax.experimental.pallas import tpu as pltpu

F32 = jnp.float32
BF16 = jnp.bfloat16
EPS = 1e-6
ROPE_BASE = 10000.0
LOG2E = 1.4426950408889634

LANES = 128
HALO = 8
VMEM_LIMIT_BYTES = 56 * 1024 * 1024

N_MOD = 6
MOD_ROWS = 8
SCAN_CHUNK = 256
GDN_CHUNK = 64
GDN_HEADS_PER_STEP = 8
GDN_SYSTEM_ROWS = 128
Q_TILE = 256
ATTN_HEADS_PER_STEP = 2
ROW_TILE = 1024
OUT_ROW_TILE = 512
FFN_ROW_TILE = 1024
FF_TILE = 256
FFN_ROW_CHUNK = 128
IN_COL_STEPS = 2
IN_COL_ALIGN = IN_COL_STEPS * LANES


def _silu(x):
    return x / (1.0 + jnp.exp(-x))


def _softplus(x):
    return jnp.maximum(x, 0.0) + jnp.log1p(jnp.exp(-jnp.abs(x)))


def _dot(a, b):
    return jnp.dot(a.astype(BF16), b.astype(BF16), preferred_element_type=F32)


def _dot_nt(a, b):
    return lax.dot_general(a.astype(BF16), b.astype(BF16), (((1,), (1,)), ((), ())),
                           preferred_element_type=F32)


def _dot_tn(a, b):
    return lax.dot_general(a.astype(BF16), b.astype(BF16), (((0,), (0,)), ((), ())),
                           preferred_element_type=F32)


def _split3(a):
    hi = a.astype(BF16)
    r1 = a - hi.astype(F32)
    mid = r1.astype(BF16)
    return hi, mid, (r1 - mid.astype(F32)).astype(BF16)


def _expand(a, p, pieces=3):
    d = functools.partial(jnp.dot, preferred_element_type=F32)
    hi, mid, lo = _split3(a)
    out = d(hi, p) + d(mid, p)
    return out + d(lo, p) if pieces == 3 else out


def _mask_sum(m, a):
    d = functools.partial(jnp.dot, preferred_element_type=F32)
    hi, mid, lo = _split3(a)
    return d(m, hi) + d(m, mid) + d(m, lo)


def _rms(x, g):
    ms = jnp.mean(x * x, axis=-1, keepdims=True)
    return x * lax.rsqrt(ms + EPS) * g


def _params(sem):
    return pltpu.CompilerParams(dimension_semantics=sem, vmem_limit_bytes=VMEM_LIMIT_BYTES)


def _conv3_silu(x, xp, xn, w, b, first, last):
    rows = x.shape[0]
    row = lax.broadcasted_iota(jnp.int32, x.shape, 0)
    prev_row = jnp.where(first, 0.0, xp[HALO - 1:HALO, :])
    next_row = jnp.where(last, 0.0, xn[0:1, :])
    x_prev = jnp.where(row == 0, prev_row, pltpu.roll(x, 1, 0))
    x_next = jnp.where(row == rows - 1, next_row, pltpu.roll(x, rows - 1, 0))
    y = x_prev * w[0:1, :] + x * w[1:2, :] + x_next * w[2:3, :] + b
    return _silu(y)


def _mod_kernel(c_ref, w_ref, b_ref, o_ref):
    s = _silu(c_ref[...])
    o_ref[...] = _dot(s, w_ref[...]) + b_ref[...]


def _mod_call(cvec, ada_w, ada_b):
    depth, d, n = ada_w.shape
    tn = d
    return pl.pallas_call(
        _mod_kernel,
        grid=(depth, n // tn),
        in_specs=[pl.BlockSpec((MOD_ROWS, d), lambda l, j: (0, 0)),
                  pl.BlockSpec((None, d, tn), lambda l, j: (l, 0, j)),
                  pl.BlockSpec((None, 1, tn), lambda l, j: (l, 0, j))],
        out_specs=pl.BlockSpec((None, MOD_ROWS, tn), lambda l, j: (l, 0, j)),
        out_shape=jax.ShapeDtypeStruct((depth, MOD_ROWS, n), F32),
        compiler_params=_params(("arbitrary", "arbitrary")),
        name="ada_mod",
    )(cvec, ada_w, ada_b.reshape(depth, 1, n))


class _Geom:
    def __init__(self, n_ctx, s_ctx, n_lat, s_lat):
        self.n_ctx, self.s_ctx, self.n_lat, self.s_lat = n_ctx, s_ctx, n_lat, s_lat
        self.ctx_rows = n_ctx * s_ctx
        self.rows = self.ctx_rows + n_lat * s_lat

    def row_tile(self, want):
        t = want
        while self.ctx_rows % t or self.s_lat % t:
            t //= 2
        return t

    def scan_chunk(self):
        return min(SCAN_CHUNK, self.s_ctx, self.s_lat)

    def chunk_pos(self, t, q, fwd):
        ctx_blocks = self.ctx_rows // q
        nc_ctx, nc_lat = self.s_ctx // q, self.s_lat // q
        is_ctx = t < ctx_blocks
        tl = jnp.maximum(t - ctx_blocks, 0)
        seq = jnp.where(is_ctx, t >> (nc_ctx.bit_length() - 1), self.n_ctx + (tl >> (nc_lat.bit_length() - 1)))
        cpos = jnp.where(is_ctx, t & (nc_ctx - 1), tl & (nc_lat - 1))
        nc = jnp.where(is_ctx, nc_ctx, nc_lat)
        c = cpos if fwd else nc - 1 - cpos
        return seq, cpos, nc, c, t - cpos + c

    def group(self, tile):
        def fn(i):
            r = i * tile
            return jnp.where(r < self.ctx_rows, 0,
                             1 + (jnp.maximum(r - self.ctx_rows, 0) >> (self.s_lat.bit_length() - 1)))
        return fn


def _mod_spec(d, layer, which, group_fn):
    return pl.BlockSpec((None, 1, d), lambda i, *_: ((layer * N_MOD + which) * MOD_ROWS + group_fn(i), 0, 0))


def _inproj_kernel(x_ref, g_ref, sh_ref, sc_ref, w_ref, o_ref, h_ref):
    @pl.when(pl.program_id(1) == 0)
    def _():
        h_ref[...] = (_rms(x_ref[...], g_ref[...]) * (1.0 + sc_ref[...]) + sh_ref[...]).astype(BF16)

    o_ref[...] = jnp.dot(h_ref[...], w_ref[...], preferred_element_type=F32)


def _inproj_call(x, mods, layer, norm_g, w, geom):
    rows, d = x.shape
    n = w.shape[1]
    tm = geom.row_tile(ROW_TILE)
    tn = n // IN_COL_STEPS
    grp = geom.group(tm)
    return pl.pallas_call(
        _inproj_kernel,
        grid=(rows // tm, n // tn),
        in_specs=[pl.BlockSpec((tm, d), lambda i, j: (i, 0)),
                  pl.BlockSpec((1, d), lambda i, j: (0, 0)),
                  _mod_spec(d, layer, 0, grp),
                  _mod_spec(d, layer, 1, grp),
                  pl.BlockSpec((d, tn), lambda i, j: (0, j))],
        out_specs=pl.BlockSpec((tm, tn), lambda i, j: (i, j)),
        out_shape=jax.ShapeDtypeStruct((rows, n), F32),
        scratch_shapes=[pltpu.VMEM((tm, d), BF16)],
        compiler_params=_params(("arbitrary", "arbitrary")),
        name="in_proj",
    )(x, norm_g.reshape(1, d), mods, mods, w)


def _outproj_kernel(split, ctx_tiles, x_ref, gate_ref, *refs):
    n_a = sum(2 if s else 1 for s in split)
    a_refs, w_refs, o_ref = list(refs[:n_a]), refs[n_a:n_a + len(split)], refs[n_a + len(split)]
    is_ctx = pl.program_id(0) < ctx_tiles
    acc = None
    for s, w_ref in zip(split, w_refs):
        a = a_refs.pop(0)[...]
        if s:
            a = jnp.where(is_ctx, a, a_refs.pop(0)[...])
        part = jnp.dot(a, w_ref[...], preferred_element_type=F32)
        acc = part if acc is None else acc + part
    o_ref[...] = x_ref[...] + gate_ref[...] * acc


def _outproj_call(x, mods, layer, acts, ws, geom):
    rows, d = x.shape
    tm = geom.row_tile(OUT_ROW_TILE)
    grp = geom.group(tm)
    ctx_tiles = geom.ctx_rows // tm
    split = tuple(isinstance(a, tuple) for a in acts)
    in_specs = [pl.BlockSpec((tm, d), lambda i: (i, 0)), _mod_spec(d, layer, 2, grp)]
    flat = []
    for a in acts:
        if isinstance(a, tuple):
            in_specs += [pl.BlockSpec((tm, a[0].shape[1]), lambda i: (jnp.minimum(i, ctx_tiles - 1), 0)),
                         pl.BlockSpec((tm, a[1].shape[1]), lambda i: (jnp.maximum(i - ctx_tiles, 0), 0))]
            flat += list(a)
        else:
            in_specs.append(pl.BlockSpec((tm, a.shape[1]), lambda i: (i, 0)))
            flat.append(a)
    in_specs += [pl.BlockSpec(w.shape, lambda i: (0, 0)) for w in ws]
    return pl.pallas_call(
        functools.partial(_outproj_kernel, split, ctx_tiles),
        grid=(rows // tm,),
        in_specs=in_specs,
        out_specs=pl.BlockSpec((tm, d), lambda i: (i, 0)),
        out_shape=jax.ShapeDtypeStruct((rows, d), F32),
        compiler_params=_params(("arbitrary",)),
        name="out_proj",
    )(x, mods, *flat, *ws)


def _ffn_kernel(geom, nf, x_ref, xp_ref, xn_ref, g_ref, sh_ref, sc_ref, gate_ref,
                wg_ref, wv_ref, cwg_ref, cwv_ref, cbg_ref, cbv_ref, wd_ref, o_ref, h_ref, acc_ref, *u_refs):
    i, j = pl.program_id(0), pl.program_id(1)
    tm = x_ref.shape[0]
    bufs = (u_refs[0:2], u_refs[2:4])

    def up(buf):
        h = h_ref[...]
        tf = wg_ref.shape[1]
        w = jnp.concatenate([wg_ref[...].astype(BF16), wv_ref[...].astype(BF16)], axis=1)
        u = jnp.dot(h, w, preferred_element_type=F32)
        buf[0][...] = u[:, :tf]
        buf[1][...] = u[:, tf:]

    def down(buf):
        rc = min(FFN_ROW_CHUNK, tm)
        row0 = i * tm
        is_ctx = row0 < geom.ctx_rows
        base = jnp.where(is_ctx, row0, row0 - geom.ctx_rows)
        lseq = jnp.where(is_ctx, geom.s_ctx, geom.s_lat)
        rowi = lax.broadcasted_iota(jnp.int32, (rc, LANES), 0)
        wd = wd_ref[...].astype(BF16)
        for r in range(0, tm, rc):
            pos = (base + r + rowi) & (lseq - 1)
            at_start, at_end = pos == 0, pos == lseq - 1

            def conv(u_ref, cw_ref, cb_ref):
                cols = []
                for c in range(0, u_ref.shape[1], LANES):
                    cs = slice(c, c + LANES)
                    up_ = jnp.where(at_start, 0.0, u_ref[HALO + r - 1:HALO + r - 1 + rc, cs])
                    un = jnp.where(at_end, 0.0, u_ref[HALO + r + 1:HALO + r + 1 + rc, cs])
                    cols.append(up_ * cw_ref[0:1, cs] + u_ref[HALO + r:HALO + r + rc, cs] * cw_ref[1:2, cs]
                                + un * cw_ref[2:3, cs] + cb_ref[:, cs])
                return jnp.concatenate(cols, axis=1)

            a = _silu(conv(buf[0], cwg_ref, cbg_ref)) * conv(buf[1], cwv_ref, cbv_ref)
            acc_ref[r:r + rc, :] += jnp.dot(a.astype(BF16), wd, preferred_element_type=F32)

    @pl.when(j == 0)
    def _():
        def hn(x):
            return (_rms(x, g_ref[...]) * (1.0 + sc_ref[...]) + sh_ref[...]).astype(BF16)
        h_ref[0:HALO, :] = hn(xp_ref[...])
        h_ref[HALO:HALO + tm, :] = hn(x_ref[...])
        h_ref[HALO + tm:, :] = hn(xn_ref[...])
        acc_ref[...] = jnp.zeros_like(acc_ref)
        up(bufs[0])

    for parity in range(2):
        @pl.when((j > 0) & (j < nf) & (j % 2 == parity))
        def _():
            up(bufs[parity])
            down(bufs[1 - parity])

    @pl.when(j == nf)
    def _():
        down(bufs[(nf - 1) % 2])
        o_ref[...] = x_ref[...] + gate_ref[...] * acc_ref[...]


def _ffn_call(x, mods, layer, norm_g, w_up, conv_w, conv_b, w_down, geom):
    rows, d = x.shape
    dff = w_down.shape[1]
    tm = geom.row_tile(FFN_ROW_TILE)
    tf = FF_TILE
    nf = dff // tf
    grp = geom.group(tm)
    nhalo = rows // HALO
    cur = lambda j: jnp.minimum(j, nf - 1)
    prv = lambda j: jnp.maximum(j - 1, 0)
    return pl.pallas_call(
        functools.partial(_ffn_kernel, geom, nf),
        grid=(rows // tm, nf + 1),
        in_specs=[pl.BlockSpec((tm, d), lambda i, j: (i, 0)),
                  pl.BlockSpec((HALO, d), lambda i, j: (jnp.maximum(i * (tm // HALO) - 1, 0), 0)),
                  pl.BlockSpec((HALO, d), lambda i, j: (jnp.minimum((i + 1) * (tm // HALO), nhalo - 1), 0)),
                  pl.BlockSpec((None, 1, d), lambda i, j: (2 * layer + 1, 0, 0)),
                  _mod_spec(d, layer, 3, grp), _mod_spec(d, layer, 4, grp), _mod_spec(d, layer, 5, grp),
                  pl.BlockSpec((None, d, tf), lambda i, j: (layer, 0, cur(j))),
                  pl.BlockSpec((None, d, tf), lambda i, j: (layer, 0, nf + cur(j))),
                  pl.BlockSpec((None, 3, tf), lambda i, j: (layer, 0, prv(j))),
                  pl.BlockSpec((None, 3, tf), lambda i, j: (layer, 0, nf + prv(j))),
                  pl.BlockSpec((None, 1, tf), lambda i, j: (layer, 0, prv(j))),
                  pl.BlockSpec((None, 1, tf), lambda i, j: (layer, 0, nf + prv(j))),
                  pl.BlockSpec((None, tf, d), lambda i, j: (layer, prv(j), 0))],
        out_specs=pl.BlockSpec((tm, d), lambda i, j: (i, 0)),
        out_shape=jax.ShapeDtypeStruct((rows, d), F32),
        scratch_shapes=[pltpu.VMEM((tm + 2 * HALO, d), BF16), pltpu.VMEM((tm, d), F32)]
        + [pltpu.VMEM((tm + 2 * HALO, tf), F32) for _ in range(4)],
        compiler_params=_params(("arbitrary", "arbitrary")),
        name="conv_ffn",
    )(x, x, x, norm_g, mods, mods, mods, w_up, w_up, conv_w, conv_w, conv_b, conv_b, w_down)


def _final_kernel(x_ref, g_ref, o_ref):
    o_ref[...] = _rms(x_ref[...], g_ref[...])


def _final_call(x, g, row0, nrows):
    d = x.shape[1]
    tm = OUT_ROW_TILE
    while nrows % tm or row0 % tm:
        tm //= 2
    return pl.pallas_call(
        _final_kernel,
        grid=(nrows // tm,),
        in_specs=[pl.BlockSpec((tm, d), lambda i: (row0 // tm + i, 0)), pl.BlockSpec((1, d), lambda i: (0, 0))],
        out_specs=pl.BlockSpec((tm, d), lambda i: (i, 0)),
        out_shape=jax.ShapeDtypeStruct((nrows, d), F32),
        compiler_params=_params(("arbitrary",)),
        name="final_norm",
    )(x, g.reshape(1, d))


def _ssd_kernel(fwd, geom, heads, *refs):
    (xs_ref, xsp_ref, xsn_ref, bc_ref, bcp_ref, bcn_ref, dt_ref, cwx_ref, cbx_ref, cwb_ref, cbb_ref,
     dtb_ref, alog_ref, pe_ref, bd_ref, h0_ref) = refs[:16]
    if fwd:
        z_ref, yb_ref, dx_ref, gn_ref, out_ref, st_ref, s_ref, yacc_ref = refs[16:]
    else:
        out_ref, st_ref, s_ref = refs[16:]
    q = xs_ref.shape[0]
    seq, cpos, nc, c, _ = geom.chunk_pos(pl.program_id(0), q, fwd)
    first, last = c == 0, c == nc - 1
    half = LANES // 2

    @pl.when(cpos == 0)
    def _():
        s_ref[...] = jnp.where(seq < geom.n_ctx, 0.0, h0_ref[...])

    xs = _conv3_silu(xs_ref[...], xsp_ref[...], xsn_ref[...], cwx_ref[...], cbx_ref[...], first, last)
    bc = _conv3_silu(bc_ref[...], bcp_ref[...], bcn_ref[...], cwb_ref[...], cbb_ref[...], first, last)
    bm, cm = bc[:, :LANES], bc[:, LANES:]
    dt = _softplus(dt_ref[...] + dtb_ref[...])
    la = dt * (-jnp.exp(alog_ref[...]))
    row = lax.broadcasted_iota(jnp.int32, (q, q), 0)
    col = lax.broadcasted_iota(jnp.int32, (q, q), 1)
    tril, triu = row >= col, row <= col
    cs_p = _mask_sum(tril.astype(F32).astype(BF16), la)
    cs_s = _mask_sum(triu.astype(F32).astype(BF16), la)
    cs = cs_p if fwd else cs_s
    edge = q - 1 if fwd else 0
    ecs = jnp.exp(cs)
    ecs_x = _expand(ecs, pe_ref[...], pieces=2)
    wd_x = _expand(dt * jnp.exp(cs[edge:edge + 1, :] - cs), pe_ref[...], pieces=2)
    carry = _expand(jnp.broadcast_to(ecs[edge:edge + 1, :], (HALO, LANES)), pe_ref[...])[0:1, :]
    s_old = s_ref[...]
    y_off = _dot(cm, s_old) * ecs_x
    s_new = (s_old * carry + _dot_tn(bm, xs * wd_x)) * bd_ref[...]
    s_ref[...] = s_new

    @pl.when(cpos == nc - 1)
    def _():
        s_t = s_new.T
        p_dim = s_t.shape[0] // heads
        for h in range(heads):
            grp = h // (heads // 2)
            st_ref[h] = s_t[h * p_dim:(h + 1) * p_dim, grp * half:(grp + 1) * half]

    if not fwd:
        out_ref[...] = y_off
        return

    c2_p, c2_s = cs_p * LOG2E, cs_s * LOG2E
    ldt = jnp.log2(dt)
    rt_p, rt_s = (c2_p - ldt).T, (c2_s - ldt).T
    neg = -jnp.inf
    lane = lax.broadcasted_iota(jnp.int32, (q, LANES), 1)
    lo, hi = lane < half, lane >= half
    gmat = (_dot_nt(jnp.where(lo, cm, 0.0), bm), _dot_nt(jnp.where(hi, cm, 0.0), bm))
    for p in range(heads // 2):
        xp = xs[:, p * LANES:(p + 1) * LANES]
        acc = None
        for hh in range(2):
            h = 2 * p + hh
            hb = heads + h
            lf = jnp.exp2(jnp.where(tril, c2_p[:, h:h + 1] - rt_p[h:h + 1, :], neg))
            lb = jnp.exp2(jnp.where(triu, c2_s[:, hb:hb + 1] - rt_s[hb:hb + 1, :], neg))
            w = gmat[h // (heads // 2)] * (lf + lb)
            part = _dot(w, jnp.where(lo if hh == 0 else hi, xp, 0.0))
            acc = part if acc is None else acc + part
        yacc_ref[:, p * LANES:(p + 1) * LANES] = acc
    y = yacc_ref[...] + y_off + yb_ref[...] + dx_ref[...] * xs
    yz = y * _silu(z_ref[...])
    out_ref[...] = _rms(yz, gn_ref[...]).astype(BF16)


def _ssd_call(fwd, proj, yb, prm, h0, geom, heads, width):
    rows = proj.shape[0]
    q = geom.scan_chunk()
    nhalo = rows // HALO
    bc_w = 2 * LANES
    xs_blk, z_blk, bc_blk, dt_blk = 0, 1, 5 * width // bc_w, (5 * width + bc_w) // LANES

    def seq(t):
        return geom.chunk_pos(t, q, fwd)[0]

    def rowblk(t):
        return geom.chunk_pos(t, q, fwd)[4]

    def prev(t):
        return jnp.maximum(rowblk(t) * (q // HALO) - 1, 0)

    def nxt(t):
        return jnp.minimum((rowblk(t) + 1) * (q // HALO), nhalo - 1)

    const = lambda t: (0, 0)
    in_specs = [pl.BlockSpec((q, width), lambda t: (rowblk(t), xs_blk)),
                pl.BlockSpec((HALO, width), lambda t: (prev(t), xs_blk)),
                pl.BlockSpec((HALO, width), lambda t: (nxt(t), xs_blk)),
                pl.BlockSpec((q, bc_w), lambda t: (rowblk(t), bc_blk)),
                pl.BlockSpec((HALO, bc_w), lambda t: (prev(t), bc_blk)),
                pl.BlockSpec((HALO, bc_w), lambda t: (nxt(t), bc_blk)),
                pl.BlockSpec((q, LANES), lambda t: (rowblk(t), dt_blk)),
                pl.BlockSpec((3, width), const), pl.BlockSpec((1, width), const),
                pl.BlockSpec((3, bc_w), const), pl.BlockSpec((1, bc_w), const),
                pl.BlockSpec((1, LANES), const), pl.BlockSpec((1, LANES), const),
                pl.BlockSpec((LANES, width), const), pl.BlockSpec((LANES, width), const),
                pl.BlockSpec((None, LANES, width), lambda t: (jnp.maximum(seq(t) - geom.n_ctx, 0), 0, 0))]
    args = [proj, proj, proj, proj, proj, proj, proj, prm["cwx"], prm["cbx"], prm["cwb"], prm["cbb"],
            prm["dtb"], prm["alog"], prm["pf"] if fwd else prm["pb"], prm["bd"], h0]
    scratch = [pltpu.VMEM((LANES, width), F32)]
    if fwd:
        in_specs += [pl.BlockSpec((q, width), lambda t: (rowblk(t), z_blk)),
                     pl.BlockSpec((q, width), lambda t: (rowblk(t), 0)),
                     pl.BlockSpec((1, width), const), pl.BlockSpec((1, width), const)]
        args += [proj, yb, prm["dx"], prm["gn"]]
        scratch.append(pltpu.VMEM((q, width), F32))
        out_dtype = BF16
    else:
        out_dtype = F32
    return pl.pallas_call(
        functools.partial(_ssd_kernel, fwd, geom, heads),
        grid=(rows // q,),
        in_specs=in_specs,
        out_specs=[pl.BlockSpec((q, width), lambda t: (rowblk(t), 0)),
                   pl.BlockSpec((None, heads, width // heads, LANES // 2), lambda t: (seq(t), 0, 0, 0))],
        out_shape=[jax.ShapeDtypeStruct((rows, width), out_dtype),
                   jax.ShapeDtypeStruct((geom.n_ctx + geom.n_lat, heads, width // heads, LANES // 2), F32)],
        scratch_shapes=scratch,
        compiler_params=_params(("arbitrary",)),
        name="ssd_fwd" if fwd else "ssd_bwd",
    )(*args)


def _rope(x, cos, sin_signed):
    lane = lax.broadcasted_iota(jnp.int32, x.shape, 1)
    partner = jnp.where((lane & 31) < 16, pltpu.roll(x, LANES - 16, 1), pltpu.roll(x, 16, 1))
    return x * cos + partner * sin_signed


def _attn_kernel(lam_init, hps, q_ref, k_ref, v_ref, kc_ref, vc_ref, cq_ref, sq_ref, ck_ref, sk_ref, lp_ref, g_ref,
                 o_ref, kr_ref, vr_ref):
    hs = range(hps)
    sls = [slice(h * LANES, (h + 1) * LANES) for h in hs]

    @pl.when(pl.program_id(2) == 0)
    def _():
        for sl in sls:
            kr_ref[:, sl] = _rope(k_ref[:, sl], ck_ref[...], sk_ref[...]).astype(BF16)
        vr_ref[...] = v_ref[...].astype(BF16)

    scale = float(LANES // 2) ** -0.5 * LOG2E
    lane = lax.broadcasted_iota(jnp.int32, (q_ref.shape[0], LANES), 1)
    lo = lane < LANES // 2
    lp = lp_ref[...]
    lam = (jnp.exp(jnp.sum(lp[0:1] * lp[1:2], axis=1, keepdims=True))
           - jnp.exp(jnp.sum(lp[2:3] * lp[3:4], axis=1, keepdims=True)) + lam_init)
    q = [_rope(q_ref[:, sl], cq_ref[...], sq_ref[...]) * scale for sl in sls]
    qm = [(jnp.where(lo, q[h], 0.0).astype(BF16), jnp.where(lo, 0.0, q[h]).astype(BF16)) for h in hs]
    nt = (((1,), (1,)), ((), ()))
    s_own = [[lax.dot_general(qm[h][m], kr_ref[:, sls[h]], nt, preferred_element_type=F32) for m in range(2)]
             for h in hs]
    s_ctx = [[lax.dot_general(qm[h][m], kc_ref[:, sls[h]].astype(BF16), nt, preferred_element_type=F32)
              for m in range(2)] for h in hs]
    pd_own, pd_ctx = [], []
    for h in hs:
        po, pc, coef = [], [], []
        for m in range(2):
            mx = jnp.maximum(jnp.max(s_own[h][m], axis=1, keepdims=True), jnp.max(s_ctx[h][m], axis=1, keepdims=True))
            po.append(jnp.exp2(s_own[h][m] - mx))
            pc.append(jnp.exp2(s_ctx[h][m] - mx))
            den = jnp.sum(po[m], axis=1, keepdims=True) + jnp.sum(pc[m], axis=1, keepdims=True)
            coef.append(1.0 / den if m == 0 else lam / den)
        pd_own.append((po[0] * coef[0] - po[1] * coef[1]).astype(BF16))
        pd_ctx.append((pc[0] * coef[0] - pc[1] * coef[1]).astype(BF16))
    o = [jnp.dot(pd_own[h], vr_ref[:, sls[h]], preferred_element_type=F32)
         + jnp.dot(pd_ctx[h], vc_ref[:, sls[h]].astype(BF16), preferred_element_type=F32) for h in hs]
    for h in hs:
        o_ref[:, sls[h]] = (_rms(o[h], g_ref[...]) * (1.0 - lam_init)).astype(BF16)


def _attn_ctx_kernel(lam_init, heads, n_prev, q_ref, k_ref, v_ref, lp_ref, g_ref, *refs):
    if n_prev:
        kp_ref, vp_ref, o_ref, ko_ref, vo_ref = refs
        ko_ref[0:n_prev] = kp_ref[...]
        vo_ref[0:n_prev] = vp_ref[...]
    else:
        o_ref, ko_ref, vo_ref = refs
    for h in range(heads):
        ko_ref[n_prev, :, h, :] = k_ref[:, h * LANES:(h + 1) * LANES]
        vo_ref[n_prev, :, h, :] = v_ref[:, h * LANES:(h + 1) * LANES]
    rows = q_ref.shape[0]
    lane = lax.broadcasted_iota(jnp.int32, (rows, LANES), 1)
    lo = lane < LANES // 2
    scale = float(LANES // 2) ** -0.5 * LOG2E
    lp = lp_ref[...]
    lam = (jnp.exp(jnp.sum(lp[0:1] * lp[1:2], axis=1, keepdims=True))
           - jnp.exp(jnp.sum(lp[2:3] * lp[3:4], axis=1, keepdims=True)) + lam_init)
    hs = range(heads)
    sls = [slice(h * LANES, (h + 1) * LANES) for h in hs]
    q = [q_ref[:, sl] * scale for sl in sls]
    k = [k_ref[:, sl].astype(BF16) for sl in sls]
    s0 = [_dot_nt(jnp.where(lo, q[h], 0.0), k[h]) for h in hs]
    s1 = [_dot_nt(jnp.where(lo, 0.0, q[h]), k[h]) for h in hs]
    pd = []
    for h in hs:
        p0 = jnp.exp2(s0[h] - jnp.max(s0[h], axis=1, keepdims=True))
        p1 = jnp.exp2(s1[h] - jnp.max(s1[h], axis=1, keepdims=True))
        c0 = 1.0 / jnp.sum(p0, axis=1, keepdims=True)
        c1 = lam / jnp.sum(p1, axis=1, keepdims=True)
        pd.append(p0 * c0 - p1 * c1)
    o = [_dot(pd[h], v_ref[:, sls[h]]) for h in hs]
    for h in hs:
        o_ref[:, sls[h]] = (_rms(o[h], g_ref[...]) * (1.0 - lam_init)).astype(BF16)


def _attn_ctx_call(lam_init, proj, lam_p, norm_g, prev_kv, n_seq, seq_len, heads, width):
    spec = lambda blk: pl.BlockSpec((seq_len, width), lambda b: (b, blk))
    n_prev = 0 if prev_kv is None else prev_kv[0].shape[1]
    cache_spec = lambda n: pl.BlockSpec((None, n, seq_len, heads, LANES), lambda b: (b, 0, 0, 0, 0))
    cache_shape = jax.ShapeDtypeStruct((n_seq, n_prev + 1, seq_len, heads, LANES), F32)
    in_specs = [spec(2), spec(3), spec(4), pl.BlockSpec(lam_p.shape, lambda b: (0, 0)),
                pl.BlockSpec((1, LANES), lambda b: (0, 0))]
    args = [proj, proj, proj, lam_p, norm_g.reshape(1, LANES)]
    if n_prev:
        in_specs += [cache_spec(n_prev), cache_spec(n_prev)]
        args += list(prev_kv)
    return pl.pallas_call(
        functools.partial(_attn_ctx_kernel, lam_init, heads, n_prev),
        grid=(n_seq,),
        in_specs=in_specs,
        out_specs=[pl.BlockSpec((seq_len, width), lambda b: (b, 0)), cache_spec(n_prev + 1), cache_spec(n_prev + 1)],
        out_shape=[jax.ShapeDtypeStruct((n_seq * seq_len, width), BF16), cache_shape, cache_shape],
        compiler_params=_params(("arbitrary",)),
        name="diff_attn_ctx",
    )(*args)


def _attn_call(lam_init, proj, cache_k, cache_v, tables, lam_p, norm_g, row0, n_seq, seq_len, heads, width):
    tq = min(Q_TILE, seq_len)
    nq = seq_len // tq
    hps = ATTN_HEADS_PER_STEP
    hw = hps * LANES
    ngrp = heads // hps
    q_blk, k_blk, v_blk = 2 * ngrp, 3 * ngrp, 4 * ngrp
    qb, sb = row0 // tq, row0 // seq_len
    past = cache_k.shape[1]
    const = lambda b, h, i: (0, 0)
    in_specs = [pl.BlockSpec((tq, hw), lambda b, h, i: (qb + b * nq + i, q_blk + h)),
                pl.BlockSpec((seq_len, hw), lambda b, h, i: (sb + b, k_blk + h)),
                pl.BlockSpec((seq_len, hw), lambda b, h, i: (sb + b, v_blk + h)),
                pl.BlockSpec((None, past, hw), lambda b, h, i: (b, 0, h)),
                pl.BlockSpec((None, past, hw), lambda b, h, i: (b, 0, h)),
                pl.BlockSpec((tq, LANES), lambda b, h, i: (i, 0)),
                pl.BlockSpec((tq, LANES), lambda b, h, i: (i, 0)),
                pl.BlockSpec((seq_len, LANES), const), pl.BlockSpec((seq_len, LANES), const),
                pl.BlockSpec(lam_p.shape, const), pl.BlockSpec((1, LANES), const)]
    return pl.pallas_call(
        functools.partial(_attn_kernel, lam_init, hps),
        grid=(n_seq, ngrp, nq),
        in_specs=in_specs,
        out_specs=pl.BlockSpec((tq, hw), lambda b, h, i: (b * nq + i, h)),
        out_shape=jax.ShapeDtypeStruct((n_seq * seq_len, width), BF16),
        scratch_shapes=[pltpu.VMEM((seq_len, hw), BF16), pltpu.VMEM((seq_len, hw), BF16)],
        compiler_params=_params(("arbitrary", "arbitrary", "arbitrary")),
        name="diff_attn_lat",
    )(proj, proj, proj, cache_k, cache_v, tables[0], tables[1], tables[0], tables[1], lam_p,
      norm_g.reshape(1, LANES))


def _rope_tables(n_tokens, grid_w):
    n_freq = LANES // 8
    pos = jnp.arange(n_tokens)
    r = (pos // grid_w).astype(F32)
    cpos = (pos % grid_w).astype(F32)
    inv = ROPE_BASE ** (-jnp.arange(n_freq, dtype=F32) / n_freq)
    ang_r, ang_c = r[:, None] * inv, cpos[:, None] * inv
    cos32 = lambda a: jnp.concatenate([jnp.cos(a), jnp.cos(a)], axis=1)
    sin32 = lambda a: jnp.concatenate([-jnp.sin(a), jnp.sin(a)], axis=1)
    cos64 = jnp.concatenate([cos32(ang_r), cos32(ang_c)], axis=1)
    sin64 = jnp.concatenate([sin32(ang_r), sin32(ang_c)], axis=1)
    return jnp.concatenate([cos64, cos64], axis=1), jnp.concatenate([sin64, sin64], axis=1)


def _gdn_masks(rb, sb, fwd):
    shift = GDN_CHUNK.bit_length() - 1

    def tri(n):
        row = jnp.arange(n)[:, None]
        col = jnp.arange(n)[None, :]
        blk = (row >> shift) == (col >> shift)
        return row, col, blk, blk & ((row >= col) if fwd else (row <= col))

    row, col, blk, incl = tri(sb)
    strict = blk & ((row > col) if fwd else (row < col))
    ms = [incl, strict, row == col, (row >> 1) == (col >> 1)]
    for lvl in range(1, shift):
        ms.append(((row >> (lvl + 1)) == (col >> (lvl + 1))) & ((row >> lvl) != (col >> lvl)))
    _, _, blk_rb, incl_rb = tri(rb)
    return jnp.stack(ms).astype(F32), jnp.stack([incl_rb, blk_rb]).astype(BF16)


def _gdn_kernel(fwd, geom, heads, hps, *refs):
    (q_ref, qp_ref, qn_ref, k_ref, kp_ref, kn_ref, v_ref, vp_ref, vn_ref, ab_ref,
     cwq_ref, cbq_ref, cwk_ref, cbk_ref, cwv_ref, cbv_ref, alog_ref, dtb_ref, mf_ref, mb_ref, s0_ref) = refs[:21]
    if fwd:
        z_ref, ob_ref, gn_ref, out_ref, st_ref, s_ref = refs[21:]
    else:
        out_ref, st_ref, s_ref = refs[21:]
    hg = pl.program_id(0)
    rb = q_ref.shape[0]
    seq, cpos, nb, c, _ = geom.chunk_pos(pl.program_id(1), rb, fwd)
    first, last = c == 0, c == nb - 1
    dk = q_ref.shape[1] // hps
    ch = GDN_CHUNK
    nsub = rb // ch
    n_lvl = ch.bit_length() - 1

    @pl.when(cpos == 0)
    def _():
        s_ref[...] = jnp.where(seq < geom.n_ctx, 0.0, s0_ref[...])

    ab = ab_ref[...]
    d0 = 0 if fwd else 2 * heads
    g_all = -jnp.exp(alog_ref[...]) * _softplus(ab + dtb_ref[...])
    b_all = 1.0 / (1.0 + jnp.exp(-ab))
    gc_all = _mask_sum(mb_ref[0], g_all)
    gt_all = _mask_sum(mb_ref[1], g_all)
    gct_all = gc_all.T
    lane = lax.broadcasted_iota(jnp.int32, ab.shape, 1)
    sub = lax.broadcasted_iota(jnp.int32, gct_all.shape, 0)
    incl_f, strict_f, eye, pair = mf_ref[0], mf_ref[1], mf_ref[2], mf_ref[3]
    sb = mf_ref.shape[1]
    cps = sb // ch

    def pick(a, l):
        return jnp.sum(jnp.where(lane == l, a, 0.0), axis=1, keepdims=True)

    hs = range(hps)
    sls = [slice(hh * dk, (hh + 1) * dk) for hh in hs]
    cs = [(hh, blk) for hh in hs for blk in range(rb // sb)]
    nch = range(len(cs))
    kn, qn, vb, gc, gtot, beta, dec = ([] for _ in range(7))
    for hh in hs:
        hd = hg * hps + hh
        sl = sls[hh]

        def conv(x_ref, xp_ref, xn_ref, cw_ref, cb_ref):
            return _conv3_silu(x_ref[:, sl], xp_ref[:, sl], xn_ref[:, sl], cw_ref[:, sl], cb_ref[:, sl], first, last)

        qc = conv(q_ref, qp_ref, qn_ref, cwq_ref, cbq_ref)
        kc = conv(k_ref, kp_ref, kn_ref, cwk_ref, cbk_ref)
        vh = conv(v_ref, vp_ref, vn_ref, cwv_ref, cbv_ref)
        qh = qc * lax.rsqrt(jnp.sum(qc * qc, axis=1, keepdims=True) + EPS) * (float(dk) ** -0.5)
        kh = kc * lax.rsqrt(jnp.sum(kc * kc, axis=1, keepdims=True) + EPS)
        gch, gth = pick(gc_all, d0 + hd), pick(gt_all, d0 + hd)
        bh = pick(b_all, d0 + heads + hd)
        g_row = jnp.sum(jnp.where(sub == d0 + hd, gct_all, 0.0), axis=0, keepdims=True)
        for blk in range(rb // sb):
            r = slice(blk * sb, (blk + 1) * sb)
            qn.append(qh[r]); kn.append(kh[r]); gc.append(gch[r]); gtot.append(gth[r]); beta.append(bh[r])
            vb.append(vh[r] * bh[r])
            dec.append(jnp.exp(jnp.minimum(gch[r] - g_row[:, r], 0.0)) * incl_f)
    kk = [_dot_nt(kn[c], kn[c]) for c in nch]
    m = [(kk[c] * beta[c]) * (dec[c] * strict_f) for c in nch]
    t = [eye - m[c] * pair for c in nch]
    for lvl in range(1, n_lvl):
        off = mf_ref[3 + lvl]
        tm = [_dot(t[c], m[c] * off) for c in nch]
        tmt = [_dot(tm[c], t[c]) for c in nch]
        t = [t[c] - tmt[c] for c in nch]
    eg = [jnp.exp(gc[c]) for c in nch]
    sol = [_dot(t[c], jnp.concatenate([vb[c], kn[c] * (beta[c] * eg[c])], axis=1)) for c in nch]
    qk = [_dot_nt(qn[c], kn[c]) for c in nch]
    qg = [qn[c] * eg[c] for c in nch]
    kg = [kn[c] * jnp.exp(gtot[c] - gc[c]) for c in nch]
    egl = [jnp.exp(gtot[c]) for c in nch]
    s = [s_ref[hh] for hh in hs]
    vnew = [[None] * cps for _ in nch]
    qs = [[None] * cps for _ in nch]
    for i in (range(nsub) if fwd else reversed(range(nsub))):
        blk, li = divmod(i, cps)
        r = slice(li * ch, (li + 1) * ch)
        idx = [hh * (rb // sb) + blk for hh in hs]
        ws = [_dot(jnp.concatenate([sol[c][r, dk:], qg[c][r]], axis=0), s[hh]) for hh, c in zip(hs, idx)]
        for hh, c in zip(hs, idx):
            vnew[c][li] = sol[c][r, :dk] - ws[hh][:ch]
            qs[c][li] = ws[hh][ch:]
        upd = [_dot_tn(kg[c][r], vnew[c][li]) for c in idx]
        s = [s[hh] * egl[c][li * ch:li * ch + 1, :] + upd[hh] for hh, c in zip(hs, idx)]
    intra = [_dot(qk[c] * dec[c], jnp.concatenate(vnew[c], axis=0)) for c in nch]
    for c, (hh, blk) in enumerate(cs):
        sl = sls[hh]
        r = slice(blk * sb, (blk + 1) * sb)
        o = jnp.concatenate(qs[c], axis=0) + intra[c]
        if fwd:
            o = o + ob_ref[r, sl]
            out_ref[r, sl] = (_rms(o, gn_ref[...]) * _silu(z_ref[r, sl])).astype(BF16)
        else:
            out_ref[r, sl] = o
    for hh in hs:
        s_ref[hh] = s[hh]

    @pl.when(cpos == nb - 1)
    def _():
        st_ref[...] = s_ref[...]


def _gdn_call(fwd, proj, ob, prm, s0, geom, heads, width):
    rows = proj.shape[0]
    rb = geom.scan_chunk()
    nhalo = rows // HALO
    dk = width // heads
    hps = GDN_HEADS_PER_STEP
    hw = hps * dk
    ngrp = heads // hps
    ab_blk = 4 * width // LANES
    mf, mb = _gdn_masks(rb, min(GDN_SYSTEM_ROWS, rb), fwd)

    def seq(t):
        return geom.chunk_pos(t, rb, fwd)[0]

    def rowblk(t):
        return geom.chunk_pos(t, rb, fwd)[4]

    def prev(t):
        return jnp.maximum(rowblk(t) * (rb // HALO) - 1, 0)

    def nxt(t):
        return jnp.minimum((rowblk(t) + 1) * (rb // HALO), nhalo - 1)

    in_specs, args = [], []
    for sec in range(3):
        in_specs += [pl.BlockSpec((rb, hw), lambda h, t, sec=sec: (rowblk(t), sec * ngrp + h)),
                     pl.BlockSpec((HALO, hw), lambda h, t, sec=sec: (prev(t), sec * ngrp + h)),
                     pl.BlockSpec((HALO, hw), lambda h, t, sec=sec: (nxt(t), sec * ngrp + h))]
        args += [proj, proj, proj]
    in_specs.append(pl.BlockSpec((rb, LANES), lambda h, t: (rowblk(t), ab_blk)))
    args.append(proj)
    for sec in range(3):
        in_specs += [pl.BlockSpec((3, hw), lambda h, t, sec=sec: (0, sec * ngrp + h)),
                     pl.BlockSpec((1, hw), lambda h, t, sec=sec: (0, sec * ngrp + h))]
        args += [prm["cw"], prm["cb"]]
    in_specs += [pl.BlockSpec((1, LANES), lambda h, t: (0, 0)), pl.BlockSpec((1, LANES), lambda h, t: (0, 0)),
                 pl.BlockSpec(mf.shape, lambda h, t: (0, 0, 0)), pl.BlockSpec(mb.shape, lambda h, t: (0, 0, 0)),
                 pl.BlockSpec((None, hps, dk, dk), lambda h, t: (jnp.maximum(seq(t) - geom.n_ctx, 0), h, 0, 0))]
    args += [prm["alog"], prm["dtb"], mf, mb, s0]
    if fwd:
        in_specs += [pl.BlockSpec((rb, hw), lambda h, t: (rowblk(t), 3 * ngrp + h)),
                     pl.BlockSpec((rb, hw), lambda h, t: (rowblk(t), h)),
                     pl.BlockSpec((1, dk), lambda h, t: (0, 0))]
        args += [proj, ob, prm["gn"]]
    return pl.pallas_call(
        functools.partial(_gdn_kernel, fwd, geom, heads, hps),
        grid=(ngrp, rows // rb),
        in_specs=in_specs,
        out_specs=[pl.BlockSpec((rb, hw), lambda h, t: (rowblk(t), h)),
                   pl.BlockSpec((None, hps, dk, dk), lambda h, t: (seq(t), h, 0, 0))],
        out_shape=[jax.ShapeDtypeStruct((rows, width), BF16 if fwd else F32),
                   jax.ShapeDtypeStruct((geom.n_ctx + geom.n_lat, heads, dk, dk), F32)],
        scratch_shapes=[pltpu.VMEM((hps, dk, dk), F32)],
        compiler_params=_params(("arbitrary", "arbitrary")),
        name="gdn_fwd" if fwd else "gdn_bwd",
    )(*args)


def _pack_states_kernel(n_in, *refs):
    o_ref = refs[n_in]
    for i in range(n_in):
        o_ref[i] = refs[i][...]


def _pack_states_call(states, n_keep):
    h, a, b = states[0].shape[1:]
    n = len(states)
    return pl.pallas_call(
        functools.partial(_pack_states_kernel, n),
        grid=(n_keep,),
        in_specs=[pl.BlockSpec((None, h, a, b), lambda i: (i, 0, 0, 0))] * n,
        out_specs=pl.BlockSpec((None, n, h, a, b), lambda i: (i, 0, 0, 0, 0)),
        out_shape=jax.ShapeDtypeStruct((n_keep, n, h, a, b), F32),
        compiler_params=_params(("arbitrary",)),
        name="pack_states",
    )(*states)


def _pad_cols(w, n):
    return jnp.pad(w, ((0, 0), (0, n - w.shape[1])))


def _lane_row(v):
    v = v.reshape(1, -1).astype(F32)
    return _pad_cols(v, LANES)


def _even_layer(x, mods, layer, e, geom, norm_g, w_in, conv_w, conv_b, dt_bias, a_log, d_skip, ssd_g,
                lam_p, da_g, w_out, cache_k, cache_v, state_ssd, tables, prev_kv):
    d = x.shape[1]
    heads, p_dim, n_state = state_ssd.shape[3], state_ssd.shape[4], state_ssd.shape[5]
    width = heads * p_dim
    da_heads = cache_k.shape[3]
    assert width == d and da_heads * LANES == d and 2 * n_state == LANES and 2 * p_dim == LANES
    bc_w = 2 * LANES
    n_dt = 2 * heads
    o_z, o_xs, o_bc, o_dt = 0, width, 2 * width, 2 * width + bc_w
    o_q = o_dt + n_dt
    cols = [w_in[:, o_xs:o_xs + width], w_in[:, o_z:o_z + width], w_in[:, o_q:o_q + 3 * d],
            w_in[:, o_bc:o_bc + bc_w], w_in[:, o_dt:o_dt + n_dt]]
    n_used = 5 * width + bc_w + n_dt
    n_pad = -(-n_used // IN_COL_ALIGN) * IN_COL_ALIGN
    w_perm = _pad_cols(jnp.concatenate(cols, axis=1), n_pad).astype(BF16)
    proj = _inproj_call(x, mods, layer, norm_g, w_perm, geom)

    hl = jnp.arange(LANES)[:, None]
    cl = jnp.arange(width)[None, :]
    prm = {
        "cwx": conv_w[:, :width], "cbx": conv_b[None, :width],
        "cwb": conv_w[:, width:], "cbb": conv_b[None, width:],
        "dtb": _lane_row(dt_bias), "alog": _lane_row(a_log),
        "pf": (hl == cl // p_dim).astype(BF16), "pb": (hl == heads + cl // p_dim).astype(BF16),
        "bd": (hl // n_state == cl // (width // 2)).astype(F32),
        "dx": jnp.repeat(d_skip, p_dim)[None, :], "gn": ssd_g[None, :],
    }

    def state_in(st):
        t = st.transpose(0, 3, 1, 2).reshape(st.shape[0], n_state, width)
        return jnp.concatenate([t, t], axis=1) * prm["bd"]

    h0f, h0b = state_in(state_ssd[:, e, 0].astype(F32)), state_in(state_ssd[:, e, 1].astype(F32))
    yb, st_b = _ssd_call(False, proj, None, prm, h0b, geom, heads, width)
    y, st_f = _ssd_call(True, proj, yb, prm, h0f, geom, heads, width)
    new_states = [st_f, st_b]
    lam_init = 0.8 - 0.6 * math.exp(-0.3 * layer)
    o_ctx, k_ctx, v_ctx = _attn_ctx_call(lam_init, proj, lam_p, da_g, prev_kv, geom.n_ctx, geom.s_ctx, da_heads, d)
    o_lat = _attn_call(lam_init, proj, cache_k[:, e].reshape(geom.n_lat, -1, d),
                       cache_v[:, e].reshape(geom.n_lat, -1, d), tables, lam_p, da_g, geom.ctx_rows,
                       geom.n_lat, geom.s_lat, da_heads, d)
    w_o = w_out.astype(BF16)
    x = _outproj_call(x, mods, layer, [y, (o_ctx, o_lat)], [w_o[:width], w_o[width:]], geom)
    return x, (k_ctx, v_ctx), new_states


def _odd_layer(x, mods, layer, o, geom, norm_g, w_in, conv_w, conv_b, a_log, dt_bias, g_norm, w_out, state_gdn):
    d = x.shape[1]
    heads, dk, dv = state_gdn.shape[3], state_gdn.shape[4], state_gdn.shape[5]
    width = heads * dv
    assert width == d and dk == LANES and dv == LANES and 4 * heads <= LANES
    n_used = 4 * width + 4 * heads
    n_pad = -(-n_used // IN_COL_ALIGN) * IN_COL_ALIGN
    w_perm = _pad_cols(w_in, n_pad).astype(BF16)
    proj = _inproj_call(x, mods, layer, norm_g, w_perm, geom)

    def ab_row(v):
        r = jnp.zeros((2, 2 * heads), F32).at[:, :heads].set(v.astype(F32))
        return _lane_row(r)

    prm = {"cw": conv_w, "cb": conv_b[None, :], "alog": ab_row(a_log), "dtb": ab_row(dt_bias), "gn": g_norm[None, :]}
    s0f, s0b = state_gdn[:, o, 0].astype(F32), state_gdn[:, o, 1].astype(F32)
    ob, st_b = _gdn_call(False, proj, None, prm, s0b, geom, heads, width)
    og, st_f = _gdn_call(True, proj, ob, prm, s0f, geom, heads, width)
    new_states = [st_f, st_b]
    x = _outproj_call(x, mods, layer, [og], [w_out.astype(BF16)], geom)
    return x, new_states


def kernel(x_prompt, x_sample, cache_attn_k, cache_attn_v, state_ssd, state_gdn, c, c_ctx, ada_w, ada_b, norm_g, ev_w_in, ev_conv_w, ev_conv_b, ssd_dt_bias, ssd_a_log, ssd_d, ssd_norm_g, da_lambda, da_norm_g, ev_w_out, od_w_in, od_conv_w, od_conv_b, gdn_a_log, gdn_dt_bias, gdn_norm_g, od_w_out, ffn_w_up, ffn_conv_w, ffn_conv_b, ffn_w_down, final_norm_g):
    n_ctx, s_ctx, d = x_prompt.shape
    n_lat, s_lat, _ = x_sample.shape
    depth = ada_w.shape[0]
    assert n_lat + 1 <= MOD_ROWS and s_ctx & (s_ctx - 1) == 0 and s_lat & (s_lat - 1) == 0
    geom = _Geom(n_ctx, s_ctx, n_lat, s_lat)
    dtype = x_prompt.dtype
    x = jnp.concatenate([x_prompt.reshape(-1, d), x_sample.reshape(-1, d)], axis=0).astype(F32)

    cvec = jnp.zeros((MOD_ROWS, d), F32).at[0].set(c_ctx).at[1:1 + n_lat].set(c)
    mods = _mod_call(cvec, ada_w, ada_b)
    mods = mods.reshape(depth, MOD_ROWS, N_MOD, d).transpose(0, 2, 1, 3).reshape(depth * N_MOD * MOD_ROWS, 1, d)

    grid_w = 64
    tables = _rope_tables(s_lat, grid_w)
    new_kv, ssd_out, gdn_out = None, [], []
    for l in range(depth):
        if l % 2 == 0:
            e = l // 2
            x, new_kv, st = _even_layer(
                x, mods, l, e, geom, norm_g[l, 0], ev_w_in[e], ev_conv_w[e], ev_conv_b[e], ssd_dt_bias[e],
                ssd_a_log[e], ssd_d[e], ssd_norm_g[e], da_lambda[e], da_norm_g[e], ev_w_out[e],
                cache_attn_k, cache_attn_v, state_ssd, tables, new_kv)
            ssd_out += st
        else:
            o = l // 2
            x, st = _odd_layer(x, mods, l, o, geom, norm_g[l, 0], od_w_in[o], od_conv_w[o], od_conv_b[o],
                               gdn_a_log[o], gdn_dt_bias[o], gdn_norm_g[o], od_w_out[o], state_gdn)
            gdn_out += st
        x = _ffn_call(x, mods, l, norm_g.reshape(2 * depth, 1, d), ffn_w_up, ffn_conv_w,
                      ffn_conv_b.reshape(depth, 1, -1), ffn_w_down, geom)
    y_prompt = _final_call(x, final_norm_g, 0, geom.ctx_rows).reshape(n_ctx, s_ctx, d)
    y_sample = _final_call(x, final_norm_g, geom.ctx_rows, n_lat * s_lat).reshape(n_lat, s_lat, d)
    def states(parts):
        packed = _pack_states_call(parts, n_ctx)
        return packed.reshape((n_ctx, len(parts) // 2, 2) + packed.shape[2:])

    return (y_prompt.astype(dtype), y_sample.astype(dtype),
            new_kv[0].astype(dtype), new_kv[1].astype(dtype),
            states(ssd_out).astype(dtype), states(gdn_out).astype(dtype))
```
